```python
import math
import jax, jax.numpy as jnp
from jax import lax
import numpy as np

D_MODEL = 1024
BATCH = 8
SEQ = 2048
DEPTH = 4
DEC_BATCH = 32
DEC_SEQ = 1
PAST_LEN = 8192
PAGE_SIZE = 128

N_A = DEPTH // 2
N_B = DEPTH - N_A
MIX_W = D_MODEL
MEM_W = D_MODEL // 4
TOK_W = MIX_W - MEM_W
H_A = 6
DH_A = TOK_W // H_A
H_B = 6
DV_B = TOK_W // H_B
DQK_B = DV_B // 2
H_M = 4
DH_M = MEM_W // H_M
N_MEM = 256
CHUNK = 128
Q_BLOCK = 128
N_GROUPS = 4
EXP_PER_GROUP = 4
N_EXPERTS = N_GROUPS * EXP_PER_GROUP
TOP_K_IN_GROUP = 2
D_EXPERT = D_MODEL // 4
ALPHA = (2.0 * DEPTH) ** 0.25
BETA = (8.0 * DEPTH) ** -0.25
LN_EPS = 1e-5
F32 = jnp.float32
A_SPLITS = (TOK_W, TOK_W, TOK_W, TOK_W, H_A, H_A, MEM_W)
B_SPLITS = (2 * H_B * DQK_B, MEM_W)
KV_SPLITS = (2 * H_B * DQK_B, H_B * DV_B)
A_IN = sum(A_SPLITS)
B_IN = sum(B_SPLITS)
KV_W = sum(KV_SPLITS)

kernel_name = 'yoco_mlstm_diffattn_hmoe_step'


def split_cols(a, sizes):
    return jnp.split(a, np.cumsum(sizes)[:-1].tolist(), axis=-1)


def layer_norm(x, g, b):
    xf = x.astype(F32)
    mu = jnp.mean(xf, axis=-1, keepdims=True)
    var = jnp.mean(jnp.square(xf - mu), axis=-1, keepdims=True)
    return ((xf - mu) * lax.rsqrt(var + LN_EPS)).astype(x.dtype) * g + b


def mlstm_chunk(state, inp):
    C, n, m = state
    q, k, v, ig, lf = inp
    L = q.shape[1]
    F = jnp.cumsum(lf, axis=1)
    src = ig - F
    logD = F[:, :, None, :] + src[:, None, :, :]
    causal = jnp.tril(jnp.ones((L, L), bool))
    logD = jnp.where(causal[None, :, :, None], logD, -jnp.inf)
    log_prev = F + m[:, None, :]
    m_t = jnp.maximum(log_prev, jnp.max(logD, axis=2))
    Dm = jnp.exp(logD - m_t[:, :, None, :])
    prev_scale = jnp.exp(log_prev - m_t)
    qk = jnp.einsum('bthd,bshd->btsh', q, k) * Dm
    num = jnp.einsum('btsh,bshd->bthd', qk, v) + prev_scale[..., None] * jnp.einsum('bthk,bhkv->bthv', q, C)
    den = jnp.sum(qk, axis=2) + prev_scale * jnp.einsum('bthk,bhk->bth', q, n)
    h = num / jnp.maximum(jnp.abs(den), jnp.exp(-m_t))[..., None]
    m_L = m_t[:, -1]
    w_src = jnp.exp(F[:, -1:, :] + src - m_L[:, None, :])
    c_scale = jnp.exp(F[:, -1] + m - m_L)
    C_new = c_scale[..., None, None] * C + jnp.einsum('bsh,bshk,bshv->bhkv', w_src, k, v)
    n_new = c_scale[..., None] * n + jnp.einsum('bsh,bshk->bhk', w_src, k)
    return (C_new, n_new, m_L), h


def mixer_a(x, w_in, b_i, b_f, state):
    B, T, _ = x.shape
    q, k, v, o, ig, fg, mq = split_cols(x @ w_in, A_SPLITS)
    q = q.reshape(B, T, H_A, DH_A).astype(F32)
    k = k.reshape(B, T, H_A, DH_A).astype(F32) * DH_A ** -0.5
    v = v.reshape(B, T, H_A, DH_A).astype(F32)
    ig = (ig + b_i).astype(F32)
    lf = jax.nn.log_sigmoid((fg + b_f).astype(F32))
    L = CHUNK if T % CHUNK == 0 else T
    nc = T // L

    def to_chunks(a):
        return jnp.moveaxis(a.reshape((B, nc, L) + a.shape[2:]), 1, 0)

    init = tuple(s.astype(F32) for s in state)
    final, h = lax.scan(mlstm_chunk, init, (to_chunks(q), to_chunks(k), to_chunks(v), to_chunks(ig), to_chunks(lf)))
    h = jnp.moveaxis(h, 0, 1).reshape(B, T, TOK_W)
    tok = (jax.nn.sigmoid(o.astype(F32)) * h).astype(x.dtype)
    return tok, mq, final


def memory_kv(mem, w_mem_kv):
    kv = jnp.einsum('bnd,ldc->lbnc', mem, w_mem_kv)
    k, v = jnp.split(kv, 2, axis=-1)
    shp = kv.shape[:3] + (H_M, DH_M)
    return k.reshape(shp), v.reshape(shp)


def mem_attention(mq, mem_k, mem_v):
    B, T, _ = mq.shape
    q = mq.reshape(B, T, H_M, DH_M)
    s = jnp.einsum('bthd,bmhd->bhtm', q, mem_k).astype(F32) * DH_M ** -0.5
    p = jax.nn.softmax(s, axis=-1).astype(mem_v.dtype)
    return jnp.einsum('bhtm,bmhd->bthd', p, mem_v).reshape(B, T, MEM_W)


def shared_kv(x, w_kv):
    B, T, _ = x.shape
    k, v = split_cols(x @ w_kv, KV_SPLITS)
    return k.reshape(B, T, 2, H_B, DQK_B), v.reshape(B, T, H_B, DV_B)


def diff_attn_prompt(q12, k12, v, lam):
    B, T = q12.shape[:2]
    nb = T // Q_BLOCK
    kpos = jnp.arange(T)

    def block(i):
        start = i * Q_BLOCK
        qb = lax.dynamic_slice_in_dim(q12, start, Q_BLOCK, axis=1)
        s = jnp.einsum('bqmhd,bkmhd->bmhqk', qb, k12).astype(F32) * DQK_B ** -0.5
        qpos = start + jnp.arange(Q_BLOCK)
        s = jnp.where(kpos[None, :] <= qpos[:, None], s, -jnp.inf)
        p = jax.nn.softmax(s, axis=-1)
        w = (p[:, 0] - lam * p[:, 1]).astype(v.dtype)
        return jnp.einsum('bhqk,bkhd->bqhd', w, v)

    o = lax.map(block, jnp.arange(nb))
    return jnp.moveaxis(o, 0, 1).reshape(B, T, H_B, DV_B)


def diff_attn_sample(q12, k12_new, v_new, k12_past, v_past, lam):
    T = q12.shape[1]
    P = k12_past.shape[1]
    sc = DQK_B ** -0.5
    s_past = jnp.einsum('bqmhd,bkmhd->bmhqk', q12, k12_past).astype(F32) * sc
    s_new = jnp.einsum('bqmhd,bkmhd->bmhqk', q12, k12_new).astype(F32) * sc
    s_new = jnp.where(jnp.tril(jnp.ones((T, T), bool)), s_new, -jnp.inf)
    p = jax.nn.softmax(jnp.concatenate([s_past, s_new], axis=-1), axis=-1)
    w = (p[:, 0] - lam * p[:, 1]).astype(v_new.dtype)
    return (jnp.einsum('bhqk,bkhd->bqhd', w[..., :P], v_past)
            + jnp.einsum('bhqk,bkhd->bqhd', w[..., P:], v_new))


def diff_lambda_init(li):
    return 0.8 - 0.6 * math.exp(-0.3 * li)


def mixer_b(x, w_in, lam_qk, g_sub, lam_init, attend):
    B, T, _ = x.shape
    q12, mq = split_cols(x @ w_in, B_SPLITS)
    q12 = q12.reshape(B, T, 2, H_B, DQK_B)
    lp = lam_qk.astype(F32)
    lam = jnp.exp(jnp.sum(lp[0] * lp[1])) - jnp.exp(jnp.sum(lp[2] * lp[3])) + lam_init
    o = attend(q12, lam).astype(F32)
    o = o * lax.rsqrt(jnp.mean(jnp.square(o), axis=-1, keepdims=True) + LN_EPS) * g_sub.astype(F32)
    o = (o * (1.0 - lam_init)).astype(x.dtype).reshape(B, T, TOK_W)
    return o, mq


def hier_moe(x, w_group, b_group, w_router, b_router, w1, w3, w2):
    shp = x.shape
    xt = x.reshape(-1, shp[-1])
    lg = (xt @ w_group + b_group).astype(F32)
    g = jnp.argmax(lg, axis=-1)
    p_g = jnp.take_along_axis(jax.nn.softmax(lg, axis=-1), g[:, None], axis=1)
    le = (xt @ w_router + b_router).astype(F32).reshape(-1, N_GROUPS, EXP_PER_GROUP)
    le_g = jnp.take_along_axis(le, g[:, None, None], axis=1)[:, 0]
    vals, idx = lax.top_k(le_g, TOP_K_IN_GROUP)
    wk = jax.nn.softmax(vals, axis=-1) * p_g
    eid = g[:, None] * EXP_PER_GROUP + idx
    combine = jnp.sum(jax.nn.one_hot(eid, N_EXPERTS, dtype=F32) * wk[..., None], axis=1)
    h = jax.nn.silu(jnp.einsum('nd,edf->nef', xt, w1)) * jnp.einsum('nd,edf->nef', xt, w3)
    y = jnp.einsum('nef,efd->nd', h * combine[..., None].astype(h.dtype), w2)
    return y.reshape(shp)


def layer_tail(x, tok, mq, mem_k, mem_v, li, P):
    m = mem_attention(mq, mem_k, mem_v).astype(tok.dtype)
    mix = jnp.concatenate([tok, m], axis=-1) @ P['w_out'][li]
    x = layer_norm(ALPHA * x + mix, P['ln_mix_g'][li], P['ln_mix_b'][li])
    ffn = hier_moe(x, P['w_group'][li], P['b_group'][li], P['w_router'][li], P['b_router'][li],
                   P['w_e1'][li], P['w_e3'][li], P['w_e2'][li])
    return layer_norm(ALPHA * x + ffn, P['ln_ffn_g'][li], P['ln_ffn_b'][li])


def trunk(x, mem_k, mem_v, a_state, make_attend, P):
    C_all, n_all, m_all = a_state
    new_C, new_n, new_m = [], [], []
    k_sh, v_sh, attend = None, None, None
    for li in range(DEPTH):
        if li < N_A:
            tok, mq, (C, n, m) = mixer_a(x, P['w_in_a'][li], P['b_igate'][li], P['b_fgate'][li],
                                         (C_all[li], n_all[li], m_all[li]))
            new_C.append(C)
            new_n.append(n)
            new_m.append(m)
        else:
            if li == N_A:
                k_sh, v_sh = shared_kv(x, P['w_kv_shared'])
                attend = make_attend(k_sh, v_sh)
            j = li - N_A
            tok, mq = mixer_b(x, P['w_in_b'][j], P['lambda_qk'][j], P['subln_g'][j], diff_lambda_init(li), attend)
        x = layer_tail(x, tok, mq, mem_k[li], mem_v[li], li, P)
    return x, jnp.stack(new_C), jnp.stack(new_n), jnp.stack(new_m), k_sh, v_sh


def setup_inputs(seed: int = 0) -> dict:
    key = jax.random.key(seed)
    ks = jax.random.split(key, 32)

    def nrm(i, shape, scale=1.0):
        return jax.random.normal(ks[i], shape, F32) * scale

    n_pages = PAST_LEN // PAGE_SIZE
    n_used = DEC_BATCH * n_pages
    n_phys = n_used + (n_used + 3) // 4
    page_table = jax.random.permutation(ks[0], n_phys)[:n_used].reshape(DEC_BATCH, n_pages).astype(jnp.int32)
    ds = D_MODEL ** -0.5
    w_in_a = nrm(11, (N_A, D_MODEL, A_IN), ds).at[:, :, 2 * TOK_W:3 * TOK_W].multiply(BETA)
    w_kv = nrm(17, (D_MODEL, KV_W), ds).at[:, 2 * H_B * DQK_B:].multiply(BETA)
    w_mem_kv = nrm(18, (DEPTH, D_MODEL, 2 * MEM_W), ds).at[:, :, MEM_W:].multiply(BETA)
    return {
        'x_prompt': nrm(1, (BATCH, SEQ, D_MODEL)),
        'x_sample': nrm(2, (DEC_BATCH, DEC_SEQ, D_MODEL)),
        'mem_prompt': nrm(3, (BATCH, N_MEM, D_MODEL)),
        'cache_mem_k': nrm(4, (DEPTH, DEC_BATCH, N_MEM, H_M, DH_M)),
        'cache_mem_v': nrm(5, (DEPTH, DEC_BATCH, N_MEM, H_M, DH_M), BETA),
        'state_mlstm_C': nrm(6, (N_A, DEC_BATCH, H_A, DH_A, DH_A), 0.2),
        'state_mlstm_n': nrm(7, (N_A, DEC_BATCH, H_A, DH_A), 0.2),
        'state_mlstm_m': nrm(8, (N_A, DEC_BATCH, H_A)),
        'cache_k': nrm(9, (n_phys, PAGE_SIZE, 2, H_B, DQK_B)),
        'cache_v': nrm(10, (n_phys, PAGE_SIZE, H_B, DV_B), BETA),
        'page_table': page_table,
        'w_in_a': w_in_a,
        'b_igate': nrm(12, (N_A, H_A), 0.1),
        'b_fgate': 3.0 + nrm(13, (N_A, H_A), 0.5),
        'w_in_b': nrm(14, (N_B, D_MODEL, B_IN), ds),
        'lambda_qk': nrm(15, (N_B, 4, DQK_B), 0.1),
        'subln_g': 1.0 + nrm(16, (N_B, DV_B), 0.02),
        'w_kv_shared': w_kv,
        'w_mem_kv': w_mem_kv,
        'w_out': nrm(19, (DEPTH, MIX_W, D_MODEL), MIX_W ** -0.5 * BETA),
        'ln_mix_g': 1.0 + nrm(20, (DEPTH, D_MODEL), 0.02),
        'ln_mix_b': nrm(21, (DEPTH, D_MODEL), 0.02),
        'ln_ffn_g': 1.0 + nrm(22, (DEPTH, D_MODEL), 0.02),
        'ln_ffn_b': nrm(23, (DEPTH, D_MODEL), 0.02),
        'w_group': nrm(24, (DEPTH, D_MODEL, N_GROUPS), ds),
        'b_group': nrm(25, (DEPTH, N_GROUPS), 0.01),
        'w_router': nrm(26, (DEPTH, D_MODEL, N_EXPERTS), ds),
        'b_router': nrm(27, (DEPTH, N_EXPERTS), 0.01),
        'w_e1': nrm(28, (DEPTH, N_EXPERTS, D_MODEL, D_EXPERT), ds),
        'w_e3': nrm(29, (DEPTH, N_EXPERTS, D_MODEL, D_EXPERT), ds),
        'w_e2': nrm(30, (DEPTH, N_EXPERTS, D_EXPERT, D_MODEL), D_EXPERT ** -0.5 * BETA),
    }


def reference(x_prompt, x_sample, mem_prompt, cache_mem_k, cache_mem_v, state_mlstm_C, state_mlstm_n,
              state_mlstm_m, cache_k, cache_v, page_table, w_in_a, b_igate, b_fgate, w_in_b, lambda_qk,
              subln_g, w_kv_shared, w_mem_kv, w_out, ln_mix_g, ln_mix_b, ln_ffn_g, ln_ffn_b, w_group,
              b_group, w_router, b_router, w_e1, w_e3, w_e2):
    P = {'w_in_a': w_in_a, 'b_igate': b_igate, 'b_fgate': b_fgate, 'w_in_b': w_in_b,
         'lambda_qk': lambda_qk, 'subln_g': subln_g, 'w_kv_shared': w_kv_shared, 'w_out': w_out,
         'ln_mix_g': ln_mix_g, 'ln_mix_b': ln_mix_b, 'ln_ffn_g': ln_ffn_g, 'ln_ffn_b': ln_ffn_b,
         'w_group': w_group, 'b_group': b_group, 'w_router': w_router, 'b_router': b_router,
         'w_e1': w_e1, 'w_e3': w_e3, 'w_e2': w_e2}

    Bp = x_prompt.shape[0]
    mem_k_p, mem_v_p = memory_kv(mem_prompt, w_mem_kv)
    zero_state = (jnp.zeros((N_A, Bp, H_A, DH_A, DH_A), F32), jnp.zeros((N_A, Bp, H_A, DH_A), F32),
                  jnp.zeros((N_A, Bp, H_A), F32))
    y_prompt, C_p, n_p, m_p, k_p, v_p = trunk(
        x_prompt, mem_k_p, mem_v_p, zero_state,
        lambda k, v: (lambda q, lam: diff_attn_prompt(q, k, v, lam)), P)

    Bs = x_sample.shape[0]
    n_pages = page_table.shape[1]
    k_past = cache_k[page_table].reshape((Bs, n_pages * PAGE_SIZE) + cache_k.shape[2:])
    v_past = cache_v[page_table].reshape((Bs, n_pages * PAGE_SIZE) + cache_v.shape[2:])
    y_sample, C_s, n_s, m_s, k_s, v_s = trunk(
        x_sample, cache_mem_k, cache_mem_v, (state_mlstm_C, state_mlstm_n, state_mlstm_m),
        lambda k, v: (lambda q, lam: diff_attn_sample(q, k, v, k_past, v_past, lam)), P)

    return (y_prompt, y_sample, mem_k_p, mem_v_p, C_p, n_p, m_p, k_p, v_p, C_s, n_s, m_s, k_s, v_s)
```

```python
import functools
import math

import jax
import jax.numpy as jnp
from jax import lax
from jax.experimental import pallas as pl
from jax.experimental.pallas import tpu as pltpu

F32 = jnp.float32
BF16 = jnp.bfloat16

D_MODEL = 1024
DEPTH = 4
N_A = 2
TOK_W = 768
MEM_W = 256
N_HEAD = 6
DH = 128
DQK = 64
H_M = 4
DH_M = 64
N_MEM = 256
CHUNK = 128
N_EXPERTS = 16
D_EXPERT = 256
ALPHA = (2.0 * DEPTH) ** 0.25
LN_EPS = 1e-5
PAGE = 128
LANES = 128
GATE_COL = 3328
A_COLS = 3456
VMEM_LIMIT = 48 * 1024 * 1024


def _cparams(sem):
    return pltpu.CompilerParams(dimension_semantics=sem, vmem_limit_bytes=VMEM_LIMIT)


def _bdot(a, b):
    return jnp.dot(a.astype(BF16), b.astype(BF16), preferred_element_type=F32)


def _bdot_nt(a, b):
    return lax.dot_general(a.astype(BF16), b.astype(BF16), (((1,), (1,)), ((), ())),
                           preferred_element_type=F32)


def _layer_norm(y, g, b):
    mu = jnp.mean(y, axis=-1, keepdims=True)
    d = y - mu
    var = jnp.mean(d * d, axis=-1, keepdims=True)
    return d * lax.rsqrt(var + LN_EPS) * g + b


def _sigmoid(x):
    return 1.0 / (1.0 + jnp.exp(-x))


def _log_sigmoid(x):
    return jnp.minimum(x, 0.0) - jnp.log1p(jnp.exp(-jnp.abs(x)))


def _proj_kernel(x_ref, w_ref, o_ref):
    o_ref[...] = jnp.dot(x_ref[...].astype(BF16), w_ref[...], preferred_element_type=F32)


def _proj(x, w, tm):
    m, k = x.shape
    n = w.shape[1]
    return pl.pallas_call(
        _proj_kernel,
        grid=(m // tm,),
        in_specs=[pl.BlockSpec((tm, k), lambda i: (i, 0)),
                  pl.BlockSpec((k, n), lambda i: (0, 0))],
        out_specs=pl.BlockSpec((tm, n), lambda i: (i, 0)),
        out_shape=jax.ShapeDtypeStruct((m, n), F32),
        compiler_params=_cparams(("arbitrary",)),
        name="proj",
    )(x, w)


def _proj2_kernel(x_ref, w_ref, o_ref, ob_ref):
    y = jnp.dot(x_ref[...].astype(BF16), w_ref[...], preferred_element_type=F32)
    o_ref[...] = y
    ob_ref[...] = y.astype(BF16)


def _proj_with_bf16_copy(x, w, tm):
    m, k = x.shape
    n = w.shape[1]
    return pl.pallas_call(
        _proj2_kernel,
        grid=(m // tm,),
        in_specs=[pl.BlockSpec((tm, k), lambda i: (i, 0)),
                  pl.BlockSpec((k, n), lambda i: (0, 0))],
        out_specs=[pl.BlockSpec((tm, n), lambda i: (i, 0)),
                   pl.BlockSpec((tm, n), lambda i: (i, 0))],
        out_shape=[jax.ShapeDtypeStruct((m, n), F32), jax.ShapeDtypeStruct((m, n), BF16)],
        compiler_params=_cparams(("arbitrary",)),
        name="proj_kv",
    )(x, w)


def _mlstm_chunk_kernel(q_ref, k_ref, v_ref, o_ref, g_ref, bias_ref,
                        tok_ref, c_ref, n_ref, m_ref):
    @pl.when(pl.program_id(1) == 0)
    def _():
        c_ref[...] = jnp.zeros_like(c_ref)
        n_ref[...] = jnp.zeros_like(n_ref)
        m_ref[...] = jnp.zeros_like(m_ref)

    L = CHUNK
    lane = lax.broadcasted_iota(jnp.int32, (L, LANES), 1)
    row = lax.broadcasted_iota(jnp.int32, (L, L), 0)
    col = lax.broadcasted_iota(jnp.int32, (L, L), 1)
    causal = col <= row

    gb = g_ref[...] + bias_ref[...]
    lf = _log_sigmoid(gb)
    tril = causal.astype(BF16)
    hi = lf.astype(BF16)
    r1 = lf - hi.astype(F32)
    mid = r1.astype(BF16)
    lo = (r1 - mid.astype(F32)).astype(BF16)
    fcum = (jnp.dot(tril, hi, preferred_element_type=F32)
            + jnp.dot(tril, mid, preferred_element_type=F32)
            + jnp.dot(tril, lo, preferred_element_type=F32))
    z = jnp.where(lane < N_HEAD, gb, fcum)
    zt = z.T

    m_row = m_ref[0]
    lane1 = lax.broadcasted_iota(jnp.int32, (1, LANES), 1)
    m_new_row = m_row
    scale = DH ** -0.5
    for h in range(N_HEAD):
        f_col = fcum[:, N_HEAD + h:N_HEAD + h + 1]
        f_row = zt[N_HEAD + h:N_HEAD + h + 1, :]
        ig_row = zt[h:h + 1, :]
        ig_col = gb[:, h:h + 1]
        m_prev = m_row[:, h:h + 1]
        log_d = jnp.where(causal, f_col - f_row + ig_row, -jnp.inf)
        log_prev = f_col + m_prev
        m_t = jnp.maximum(log_prev, jnp.max(log_d, axis=1, keepdims=True))
        dm = jnp.exp(log_d - m_t)
        prev_scale = jnp.exp(log_prev - m_t)

        sl = slice(h * DH, (h + 1) * DH)
        qh = q_ref[:, sl]
        kh = k_ref[:, sl] * scale
        vh = v_ref[:, sl]
        qb = qh.astype(BF16)
        vb = vh.astype(BF16)
        c_h = c_ref[0, h]
        n_h = n_ref[0, h:h + 1, :]
        qk = _bdot_nt(qb, kh) * dm
        num = _bdot(qk, vb) + prev_scale * _bdot(qb, c_h)
        den = (jnp.sum(qk, axis=1, keepdims=True)
               + prev_scale * jnp.sum(qh * n_h, axis=1, keepdims=True))
        hh = num / jnp.maximum(jnp.abs(den), jnp.exp(-m_t))
        tok_ref[:, sl] = _sigmoid(o_ref[:, sl]) * hh

        m_last = m_t[L - 1:L, :]
        f_last = f_col[L - 1:L, :]
        w_src = jnp.exp(f_last + ig_col - f_col - m_last)
        c_scale = jnp.exp(f_last + m_prev - m_last)
        kw = kh * w_src
        c_ref[0, h] = c_scale * c_h + _bdot(kw.T, vb)
        n_ref[0, h:h + 1, :] = c_scale * n_h + jnp.sum(kw, axis=0, keepdims=True)
        m_new_row = jnp.where(lane1 == h, m_last, m_new_row)
    m_ref[0] = m_new_row


def _mlstm_prompt(proj, bias_row, batch, seq):
    nc = seq // CHUNK
    m_tok = batch * seq

    def col_block(j):
        return pl.BlockSpec((CHUNK, TOK_W), lambda b, c: (b * nc + c, j))

    return pl.pallas_call(
        _mlstm_chunk_kernel,
        grid=(batch, nc),
        in_specs=[col_block(0), col_block(1), col_block(2), col_block(3),
                  pl.BlockSpec((CHUNK, LANES), lambda b, c: (b * nc + c, GATE_COL // LANES)),
                  pl.BlockSpec((1, LANES), lambda b, c: (0, 0))],
        out_specs=[pl.BlockSpec((CHUNK, TOK_W), lambda b, c: (b * nc + c, 0)),
                   pl.BlockSpec((1, N_HEAD, DH, DH), lambda b, c: (b, 0, 0, 0)),
                   pl.BlockSpec((1, N_HEAD, DH), lambda b, c: (b, 0, 0)),
                   pl.BlockSpec((1, 1, LANES), lambda b, c: (b, 0, 0))],
        out_shape=[jax.ShapeDtypeStruct((m_tok, TOK_W), F32),
                   jax.ShapeDtypeStruct((batch, N_HEAD, DH, DH), F32),
                   jax.ShapeDtypeStruct((batch, N_HEAD, DH), F32),
                   jax.ShapeDtypeStruct((batch, 1, LANES), F32)],
        compiler_params=_cparams(("arbitrary", "arbitrary")),
        name="mlstm_chunk",
    )(proj, proj, proj, proj, proj, bias_row)


def _mlstm_step_kernel(q_ref, k_ref, v_ref, o_ref, g_ref, bias_ref, c_ref, n_ref, m_ref,
                       tok_ref, c_out, n_out, m_out):
    gb = g_ref[0] + bias_ref[...]
    m_row = m_ref[0]
    lane1 = lax.broadcasted_iota(jnp.int32, (1, LANES), 1)
    row = lax.broadcasted_iota(jnp.int32, (DH, DH), 0)
    col = lax.broadcasted_iota(jnp.int32, (DH, DH), 1)
    eye = row == col
    m_new_row = m_row
    scale = DH ** -0.5
    for h in range(N_HEAD):
        ig = gb[:, h:h + 1]
        lf = _log_sigmoid(gb[:, N_HEAD + h:N_HEAD + h + 1])
        m_prev = m_row[:, h:h + 1]
        log_prev = lf + m_prev
        m_t = jnp.maximum(log_prev, ig)
        dm = jnp.exp(ig - m_t)
        prev_scale = jnp.exp(log_prev - m_t)

        sl = slice(h * DH, (h + 1) * DH)
        q_row = q_ref[0][:, sl]
        k_row = k_ref[0][:, sl] * scale
        v_row = v_ref[0][:, sl]
        c_h = c_ref[0, 0, h]
        n_h = n_ref[0, 0, h:h + 1, :]
        qk = jnp.sum(q_row * k_row, axis=1, keepdims=True) * dm
        q8 = jnp.broadcast_to(q_row, (8, DH))
        q_c = _bdot(q8, c_h)[0:1, :]
        num = qk * v_row + prev_scale * q_c
        den = qk + prev_scale * jnp.sum(q_row * n_h, axis=1, keepdims=True)
        hh = num / jnp.maximum(jnp.abs(den), jnp.exp(-m_t))
        tok_ref[0, :, sl] = _sigmoid(o_ref[0][:, sl]) * hh

        k_col = jnp.sum(jnp.where(eye, jnp.broadcast_to(k_row, (DH, DH)), 0.0), axis=1, keepdims=True)
        c_out[0, h] = prev_scale * c_h + (dm * k_col) * v_row
        n_out[0, h:h + 1, :] = prev_scale * n_h + dm * k_row
        m_new_row = jnp.where(lane1 == h, m_t, m_new_row)
    m_out[0] = m_new_row


def _mlstm_sample(proj3, bias_row, c_all, n_all, m_pad, li):
    batch = proj3.shape[0]

    def col_block(j):
        return pl.BlockSpec((1, 1, TOK_W), lambda b: (b, 0, j))

    return pl.pallas_call(
        _mlstm_step_kernel,
        grid=(batch,),
        in_specs=[col_block(0), col_block(1), col_block(2), col_block(3),
                  pl.BlockSpec((1, 1, LANES), lambda b: (b, 0, GATE_COL // LANES)),
                  pl.BlockSpec((1, LANES), lambda b: (0, 0)),
                  pl.BlockSpec((1, 1, N_HEAD, DH, DH), lambda b: (li, b, 0, 0, 0)),
                  pl.BlockSpec((1, 1, N_HEAD, DH), lambda b: (li, b, 0, 0)),
                  pl.BlockSpec((1, 1, LANES), lambda b: (b, 0, 0))],
        out_specs=[pl.BlockSpec((1, 1, TOK_W), lambda b: (b, 0, 0)),
                   pl.BlockSpec((1, N_HEAD, DH, DH), lambda b: (b, 0, 0, 0)),
                   pl.BlockSpec((1, N_HEAD, DH), lambda b: (b, 0, 0)),
                   pl.BlockSpec((1, 1, LANES), lambda b: (b, 0, 0))],
        out_shape=[jax.ShapeDtypeStruct((batch, 1, TOK_W), F32),
                   jax.ShapeDtypeStruct((batch, N_HEAD, DH, DH), F32),
                   jax.ShapeDtypeStruct((batch, N_HEAD, DH), F32),
                   jax.ShapeDtypeStruct((batch, 1, LANES), F32)],
        compiler_params=_cparams(("arbitrary",)),
        name="mlstm_step",
    )(proj3, proj3, proj3, proj3, proj3, bias_row, c_all, n_all, m_pad)


def _mem_attn_kernel(q_ref, k_ref, v_ref, o_ref):
    outs = []
    for h in range(H_M):
        sl = slice(h * DH_M, (h + 1) * DH_M)
        s = _bdot_nt(q_ref[:, sl], k_ref[0][:, sl]) * (DH_M ** -0.5)
        e = jnp.exp(s - jnp.max(s, axis=1, keepdims=True))
        p = e / jnp.sum(e, axis=1, keepdims=True)
        outs.append(_bdot(p, v_ref[0][:, sl]))
    o_ref[...] = jnp.concatenate(outs, axis=-1)


def _mem_attn_prompt(proj, q_col_block, mem_k, mem_v, seq, tq):
    m_tok = proj.shape[0]
    per_b = seq // tq
    return pl.pallas_call(
        _mem_attn_kernel,
        grid=(m_tok // tq,),
        in_specs=[pl.BlockSpec((tq, MEM_W), lambda i: (i, q_col_block)),
                  pl.BlockSpec((1, N_MEM, MEM_W), lambda i: (i // per_b, 0, 0)),
                  pl.BlockSpec((1, N_MEM, MEM_W), lambda i: (i // per_b, 0, 0))],
        out_specs=pl.BlockSpec((tq, MEM_W), lambda i: (i, 0)),
        out_shape=jax.ShapeDtypeStruct((m_tok, MEM_W), F32),
        compiler_params=_cparams(("arbitrary",)),
        name="mem_attn",
    )(proj, mem_k, mem_v)


def _mem_attn_step_kernel(q_ref, k_ref, v_ref, o_ref):
    rows = 8
    rid = lax.broadcasted_iota(jnp.int32, (rows, MEM_W), 0)
    lane = lax.broadcasted_iota(jnp.int32, (rows, MEM_W), 1)
    head_mask = (lane // DH_M) == rid
    q = jnp.broadcast_to(q_ref[0], (rows, MEM_W))
    qm = jnp.where(head_mask, q, 0.0)
    s = _bdot_nt(qm, k_ref[0, 0]) * (DH_M ** -0.5)
    e = jnp.exp(s - jnp.max(s, axis=1, keepdims=True))
    p = e / jnp.sum(e, axis=1, keepdims=True)
    o = _bdot(p, v_ref[0, 0])
    o_ref[0] = jnp.sum(jnp.where(head_mask, o, 0.0), axis=0, keepdims=True)


def _mem_attn_sample(mq3, cache_k, cache_v, li):
    batch = mq3.shape[0]
    return pl.pallas_call(
        _mem_attn_step_kernel,
        grid=(batch,),
        in_specs=[pl.BlockSpec((1, 1, MEM_W), lambda b: (b, 0, 0)),
                  pl.BlockSpec((1, 1, N_MEM, MEM_W), lambda b: (li, b, 0, 0)),
                  pl.BlockSpec((1, 1, N_MEM, MEM_W), lambda b: (li, b, 0, 0))],
        out_specs=pl.BlockSpec((1, 1, MEM_W), lambda b: (b, 0, 0)),
        out_shape=jax.ShapeDtypeStruct((batch, 1, MEM_W), F32),
        compiler_params=_cparams(("arbitrary",)),
        name="mem_attn_step",
    )(mq3, cache_k, cache_v)


def _routing(logits):
    lane = lax.broadcasted_iota(jnp.int32, logits.shape, 1)
    lane_f = lane.astype(F32)
    big = 1000.0
    is_g = (lane >= N_EXPERTS) & (lane < N_EXPERTS + 4)
    lg = jnp.where(is_g, logits, -jnp.inf)
    gmax = jnp.max(lg, axis=1, keepdims=True)
    gidx = jnp.min(jnp.where(lg == gmax, lane_f, big), axis=1, keepdims=True) - float(N_EXPERTS)
    p_g = 1.0 / jnp.sum(jnp.exp(lg - gmax), axis=1, keepdims=True)
    in_grp = (lane < N_EXPERTS) & ((lane >> 2).astype(F32) == gidx)
    le = jnp.where(in_grp, logits, -jnp.inf)
    v1 = jnp.max(le, axis=1, keepdims=True)
    i1 = jnp.min(jnp.where(le == v1, lane_f, big), axis=1, keepdims=True)
    le2 = jnp.where(lane_f == i1, -jnp.inf, le)
    v2 = jnp.max(le2, axis=1, keepdims=True)
    i2 = jnp.min(jnp.where(le2 == v2, lane_f, big), axis=1, keepdims=True)
    e2 = jnp.exp(v2 - v1)
    inv = 1.0 / (1.0 + e2)
    return jnp.where(lane_f == i1, inv * p_g, jnp.where(lane_f == i2, e2 * inv * p_g, 0.0))


def _mix_kernel(x_ref, tok_ref, mem_ref, wt_ref, wm_ref, g_ref, b_ref, wr_ref, br_ref,
                x1_ref, comb_ref):
    mix = (jnp.dot(tok_ref[...].astype(BF16), wt_ref[...], preferred_element_type=F32)
           + jnp.dot(mem_ref[...].astype(BF16), wm_ref[...], preferred_element_type=F32))
    x1 = _layer_norm(ALPHA * x_ref[...] + mix, g_ref[...], b_ref[...])
    x1_ref[...] = x1
    logits = jnp.dot(x1.astype(BF16), wr_ref[...], preferred_element_type=F32) + br_ref[...]
    comb_ref[...] = _routing(logits)


def _mix(x, tok, mem, wt, wm, g, b, wr, br, tm):
    m_tok = x.shape[0]
    const = lambda i: (0, 0)
    rows = lambda i: (i, 0)
    return pl.pallas_call(
        _mix_kernel,
        grid=(m_tok // tm,),
        in_specs=[pl.BlockSpec((tm, D_MODEL), rows),
                  pl.BlockSpec((tm, TOK_W), rows),
                  pl.BlockSpec((tm, MEM_W), rows),
                  pl.BlockSpec((TOK_W, D_MODEL), const),
                  pl.BlockSpec((MEM_W, D_MODEL), const),
                  pl.BlockSpec((1, D_MODEL), const),
                  pl.BlockSpec((1, D_MODEL), const),
                  pl.BlockSpec((D_MODEL, LANES), const),
                  pl.BlockSpec((1, LANES), const)],
        out_specs=[pl.BlockSpec((tm, D_MODEL), rows),
                   pl.BlockSpec((tm, LANES), rows)],
        out_shape=[jax.ShapeDtypeStruct((m_tok, D_MODEL), F32),
                   jax.ShapeDtypeStruct((m_tok, LANES), F32)],
        compiler_params=_cparams(("arbitrary",)),
        name="mix",
    )(x, tok, mem, wt, wm, g, b, wr, br)


def _moe_kernel(x_ref, comb_ref, w1_ref, w3_ref, w2_ref, g_ref, b_ref, o_ref, xb_ref, acc_ref):
    e = pl.program_id(1)

    @pl.when(e == 0)
    def _():
        xb_ref[...] = x_ref[...].astype(BF16)
        acc_ref[...] = jnp.zeros_like(acc_ref)

    xb = xb_ref[...]
    a = jnp.dot(xb, w1_ref[0, 0].astype(BF16), preferred_element_type=F32)
    b = jnp.dot(xb, w3_ref[0, 0].astype(BF16), preferred_element_type=F32)
    comb = comb_ref[...]
    lane = lax.broadcasted_iota(jnp.int32, comb.shape, 1)
    ce = jnp.sum(jnp.where(lane == e, comb, 0.0), axis=1, keepdims=True)
    hid = (a * _sigmoid(a)) * b * ce
    acc_ref[...] += jnp.dot(hid.astype(BF16), w2_ref[0, 0].astype(BF16), preferred_element_type=F32)

    @pl.when(e == N_EXPERTS - 1)
    def _():
        o_ref[...] = _layer_norm(ALPHA * x_ref[...] + acc_ref[...], g_ref[...], b_ref[...])


def _moe(x1, comb, w1, w3, w2, g, b, li, tm):
    m_tok = x1.shape[0]
    rows = lambda i, e: (i, 0)
    const = lambda i, e: (0, 0)
    return pl.pallas_call(
        _moe_kernel,
        grid=(m_tok // tm, N_EXPERTS),
        in_specs=[pl.BlockSpec((tm, D_MODEL), rows),
                  pl.BlockSpec((tm, LANES), rows),
                  pl.BlockSpec((1, 1, D_MODEL, D_EXPERT), lambda i, e: (li, e, 0, 0)),
                  pl.BlockSpec((1, 1, D_MODEL, D_EXPERT), lambda i, e: (li, e, 0, 0)),
                  pl.BlockSpec((1, 1, D_EXPERT, D_MODEL), lambda i, e: (li, e, 0, 0)),
                  pl.BlockSpec((1, D_MODEL), const),
                  pl.BlockSpec((1, D_MODEL), const)],
        out_specs=pl.BlockSpec((tm, D_MODEL), rows),
        out_shape=jax.ShapeDtypeStruct((m_tok, D_MODEL), F32),
        scratch_shapes=[pltpu.VMEM((tm, D_MODEL), BF16), pltpu.VMEM((tm, D_MODEL), F32)],
        compiler_params=_cparams(("arbitrary", "arbitrary")),
        name="moe",
    )(x1, comb, w1, w3, w2, g, b)


def _diff_lambda(lq_ref, lam_init):
    lq = lq_ref[0]
    a = jnp.sum(lq[0:1, :] * lq[1:2, :], axis=1, keepdims=True)
    b = jnp.sum(lq[2:3, :] * lq[3:4, :], axis=1, keepdims=True)
    return jnp.exp(a) - jnp.exp(b) + lam_init


def _sub_norm(o, g_row, lam_init):
    o = o * lax.rsqrt(jnp.mean(o * o, axis=-1, keepdims=True) + LN_EPS) * g_row
    return o * (1.0 - lam_init)


def _diff_attn_kernel(q1_ref, q2_ref, k1_ref, k2_ref, v_ref, lq_ref, gs_ref, o_ref,
                      m_scr, l_scr, acc_scr, *, tq, lam_init):
    i = pl.program_id(2)
    lam = _diff_lambda(lq_ref, lam_init)
    qs = [(q1_ref[...] * (DQK ** -0.5)).astype(BF16), (q2_ref[...] * (DQK ** -0.5)).astype(BF16)]
    k_refs = [k1_ref, k2_ref]
    m_scr[...] = jnp.full_like(m_scr, -jnp.inf)
    l_scr[...] = jnp.zeros_like(l_scr)
    acc_scr[...] = jnp.zeros_like(acc_scr)
    row = lax.broadcasted_iota(jnp.int32, (tq, tq), 0)
    col = lax.broadcasted_iota(jnp.int32, (tq, tq), 1)
    causal = col <= row

    def block(j, masked):
        start = pl.multiple_of(j * tq, tq)
        for hh in range(2):
            vb = v_ref[pl.ds(start, tq), hh * DH:(hh + 1) * DH]
            for mm in range(2):
                idx = hh * 2 + mm
                kb = k_refs[mm][pl.ds(start, tq), hh * DQK:(hh + 1) * DQK]
                s = lax.dot_general(qs[mm][:, hh * DQK:(hh + 1) * DQK], kb,
                                    (((1,), (1,)), ((), ())), preferred_element_type=F32)
                if masked:
                    s = jnp.where(causal, s, -jnp.inf)
                m_old = m_scr[idx]
                m_new = jnp.maximum(m_old, jnp.max(s, axis=1, keepdims=True))
                alpha = jnp.exp(m_old - m_new)
                p = jnp.exp(s - m_new)
                l_scr[idx] = alpha * l_scr[idx] + jnp.sum(p, axis=1, keepdims=True)
                acc_scr[idx] = alpha * acc_scr[idx] + jnp.dot(p.astype(BF16), vb,
                                                              preferred_element_type=F32)
                m_scr[idx] = m_new

    def body(j, carry):
        block(j, False)
        return carry

    lax.fori_loop(0, i, body, 0)
    block(i, True)

    for hh in range(2):
        o1 = acc_scr[hh * 2] / l_scr[hh * 2]
        o2 = acc_scr[hh * 2 + 1] / l_scr[hh * 2 + 1]
        o_ref[:, hh * DH:(hh + 1) * DH] = _sub_norm(o1 - lam * o2, gs_ref[...], lam_init)


def _diff_attn_prompt(proj_b, kv_b, lam_qk, g_sub, j, lam_init, batch, seq, tq):
    m_tok = batch * seq
    nq = seq // tq
    kern = functools.partial(_diff_attn_kernel, tq=tq, lam_init=lam_init)
    return pl.pallas_call(
        kern,
        grid=(batch, 3, nq),
        in_specs=[pl.BlockSpec((tq, 2 * DQK), lambda b, hp, i: (b * nq + i, hp)),
                  pl.BlockSpec((tq, 2 * DQK), lambda b, hp, i: (b * nq + i, 3 + hp)),
                  pl.BlockSpec((seq, 2 * DQK), lambda b, hp, i: (b, hp)),
                  pl.BlockSpec((seq, 2 * DQK), lambda b, hp, i: (b, 3 + hp)),
                  pl.BlockSpec((seq, 2 * DH), lambda b, hp, i: (b, 3 + hp)),
                  pl.BlockSpec((1, 4, DQK), lambda b, hp, i: (j, 0, 0)),
                  pl.BlockSpec((1, DH), lambda b, hp, i: (0, 0))],
        out_specs=pl.BlockSpec((tq, 2 * DH), lambda b, hp, i: (b * nq + i, hp)),
        out_shape=jax.ShapeDtypeStruct((m_tok, TOK_W), F32),
        scratch_shapes=[pltpu.VMEM((4, tq, 1), F32), pltpu.VMEM((4, tq, 1), F32),
                        pltpu.VMEM((4, tq, DH), F32)],
        compiler_params=_cparams(("arbitrary", "arbitrary", "arbitrary")),
        name="diff_attn",
    )(proj_b, proj_b, kv_b, kv_b, kv_b, lam_qk, g_sub)


def _diff_attn_step_kernel(pt_ref, q_ref, kn_ref, vn_ref, k_ref, v_ref, lq_ref, gs_ref, o_ref,
                           qm_scr, m_scr, l_scr, acc_scr, *, n_pages, lam_init):
    p_idx = pl.program_id(1)
    rows = 16
    rid = lax.broadcasted_iota(jnp.int32, (rows, TOK_W), 0)
    lane = lax.broadcasted_iota(jnp.int32, (rows, TOK_W), 1)

    @pl.when(p_idx == 0)
    def _():
        q = jnp.broadcast_to(q_ref[0] * (DQK ** -0.5), (rows, TOK_W))
        qm_scr[...] = jnp.where((lane // DQK) == rid, q, 0.0)
        m_scr[...] = jnp.full_like(m_scr, -jnp.inf)
        l_scr[...] = jnp.zeros_like(l_scr)
        acc_scr[...] = jnp.zeros_like(acc_scr)

    qm = qm_scr[...]
    s = _bdot_nt(qm, k_ref[0])
    m_old = m_scr[...]
    m_new = jnp.maximum(m_old, jnp.max(s, axis=1, keepdims=True))
    alpha = jnp.exp(m_old - m_new)
    p = jnp.exp(s - m_new)
    l_scr[...] = alpha * l_scr[...] + jnp.sum(p, axis=1, keepdims=True)
    acc_scr[...] = alpha * acc_scr[...] + _bdot(p, v_ref[0])
    m_scr[...] = m_new

    @pl.when(p_idx == n_pages - 1)
    def _():
        lam = _diff_lambda(lq_ref, lam_init)
        s_new = jnp.sum(qm * kn_ref[0], axis=1, keepdims=True)
        m_old2 = m_scr[...]
        m_fin = jnp.maximum(m_old2, s_new)
        a2 = jnp.exp(m_old2 - m_fin)
        p_new = jnp.exp(s_new - m_fin)
        l_fin = a2 * l_scr[...] + p_new
        acc = a2 * acc_scr[...] + p_new * vn_ref[0]
        r = acc / l_fin
        head_of_lane = lane // DH
        o1 = jnp.sum(jnp.where(head_of_lane == rid, r, 0.0), axis=0, keepdims=True)
        o2 = jnp.sum(jnp.where(head_of_lane == rid - N_HEAD, r, 0.0), axis=0, keepdims=True)
        o = o1 - lam * o2
        for h in range(N_HEAD):
            sl = slice(h * DH, (h + 1) * DH)
            o_ref[0, :, sl] = _sub_norm(o[:, sl], gs_ref[...], lam_init)


def _diff_attn_sample(page_table, q3, kn3, vn3, cache_k, cache_v, lam_qk, g_sub, j, lam_init):
    batch, n_pages = page_table.shape
    kern = functools.partial(_diff_attn_step_kernel, n_pages=n_pages, lam_init=lam_init)
    row3 = lambda b, p, pt: (b, 0, 0)
    grid_spec = pltpu.PrefetchScalarGridSpec(
        num_scalar_prefetch=1,
        grid=(batch, n_pages),
        in_specs=[pl.BlockSpec((1, 1, TOK_W), row3),
                  pl.BlockSpec((1, 1, TOK_W), row3),
                  pl.BlockSpec((1, 1, TOK_W), row3),
                  pl.BlockSpec((1, PAGE, TOK_W), lambda b, p, pt: (pt[b, p], 0, 0)),
                  pl.BlockSpec((1, PAGE, TOK_W), lambda b, p, pt: (pt[b, p], 0, 0)),
                  pl.BlockSpec((1, 4, DQK), lambda b, p, pt: (j, 0, 0)),
                  pl.BlockSpec((1, DH), lambda b, p, pt: (0, 0))],
        out_specs=pl.BlockSpec((1, 1, TOK_W), row3),
        scratch_shapes=[pltpu.VMEM((16, TOK_W), F32), pltpu.VMEM((16, 1), F32),
                        pltpu.VMEM((16, 1), F32), pltpu.VMEM((16, TOK_W), F32)],
    )
    return pl.pallas_call(
        kern,
        grid_spec=grid_spec,
        out_shape=jax.ShapeDtypeStruct((batch, 1, TOK_W), F32),
        compiler_params=_cparams(("arbitrary", "arbitrary")),
        name="diff_attn_step",
    )(page_table, q3, kn3, vn3, cache_k, cache_v, lam_qk, g_sub)


def _diff_lambda_init(li):
    return 0.8 - 0.6 * math.exp(-0.3 * li)


def _prep_weights(P):
    w = {}
    wa = P['w_in_a']
    gates = jnp.pad(wa[:, :, 4 * TOK_W:4 * TOK_W + 2 * N_HEAD], ((0, 0), (0, 0), (0, LANES - 2 * N_HEAD)))
    w['w_a'] = jnp.concatenate([wa[:, :, :4 * TOK_W], wa[:, :, 4 * TOK_W + 2 * N_HEAD:], gates],
                               axis=-1).astype(BF16)
    w['gate_bias'] = jnp.pad(jnp.concatenate([P['b_igate'], P['b_fgate']], axis=-1),
                             ((0, 0), (0, LANES - 2 * N_HEAD)))[:, None, :]
    w['w_b'] = P['w_in_b'].astype(BF16)
    w['w_kv'] = P['w_kv_shared'].astype(BF16)
    w['w_mem'] = jnp.transpose(P['w_mem_kv'], (1, 0, 2)).reshape(D_MODEL, DEPTH * 2 * MEM_W).astype(BF16)
    w['w_out_t'] = P['w_out'][:, :TOK_W, :].astype(BF16)
    w['w_out_m'] = P['w_out'][:, TOK_W:, :].astype(BF16)
    wr = jnp.concatenate([P['w_router'], P['w_group']], axis=-1)
    w['w_r'] = jnp.pad(wr, ((0, 0), (0, 0), (0, LANES - wr.shape[-1]))).astype(BF16)
    br = jnp.concatenate([P['b_router'], P['b_group']], axis=-1)
    w['b_r'] = jnp.pad(br, ((0, 0), (0, LANES - br.shape[-1])))[:, None, :]
    return w


def _layer_tail(x, tok, mem, li, P, W, tm_mix, tm_moe):
    x1, comb = _mix(x, tok, mem, W['w_out_t'][li], W['w_out_m'][li],
                    P['ln_mix_g'][li][None, :], P['ln_mix_b'][li][None, :],
                    W['w_r'][li], W['b_r'][li], tm_mix)
    return _moe(x1, comb, P['w_e1'], P['w_e3'], P['w_e2'],
                P['ln_ffn_g'][li][None, :], P['ln_ffn_b'][li][None, :], li, tm_moe)


def _prompt_trunk(x_prompt, mem_prompt, P, W):
    batch, seq, _ = x_prompt.shape
    m_tok = batch * seq
    x = x_prompt.reshape(m_tok, D_MODEL)
    mem = mem_prompt.reshape(batch * N_MEM, D_MODEL)
    kv_mem = _proj(mem, W['w_mem'], 512)
    kv_mem = jnp.transpose(kv_mem.reshape(batch, N_MEM, DEPTH, 2, MEM_W), (3, 2, 0, 1, 4))
    mem_k, mem_v = kv_mem[0], kv_mem[1]

    new_c, new_n, new_m = [], [], []
    kv32 = kv16 = None
    for li in range(DEPTH):
        if li < N_A:
            proj = _proj(x, W['w_a'][li], 256)
            tok, c, n, m = _mlstm_prompt(proj, W['gate_bias'][li], batch, seq)
            new_c.append(c)
            new_n.append(n)
            new_m.append(m[:, 0, :N_HEAD])
            mq_block = (4 * TOK_W) // MEM_W
        else:
            if li == N_A:
                kv32, kv16 = _proj_with_bf16_copy(x, W['w_kv'], 256)
            j = li - N_A
            proj = _proj(x, W['w_b'][j], 512)
            tok = _diff_attn_prompt(proj, kv16, P['lambda_qk'], P['subln_g'][j:j + 1], j,
                                    _diff_lambda_init(li), batch, seq, 256)
            mq_block = TOK_W // MEM_W
        mem_o = _mem_attn_prompt(proj, mq_block, mem_k[li], mem_v[li], seq, 512)
        x = _layer_tail(x, tok, mem_o, li, P, W, 512, 1024)
    y = x.reshape(batch, seq, D_MODEL)
    k_p = kv32[:, :2 * N_HEAD * DQK].reshape(batch, seq, 2, N_HEAD, DQK)
    v_p = kv32[:, 2 * N_HEAD * DQK:].reshape(batch, seq, N_HEAD, DH)
    shp = (DEPTH, batch, N_MEM, H_M, DH_M)
    return (y, mem_k.reshape(shp), mem_v.reshape(shp),
            jnp.stack(new_c), jnp.stack(new_n), jnp.stack(new_m), k_p, v_p)


def _sample_trunk(x_sample, cache_mem_k, cache_mem_v, state, cache_k, cache_v, page_table, P, W):
    batch = x_sample.shape[0]
    x = x_sample.reshape(batch, D_MODEL)
    c_all, n_all, m_all = state
    mem_k = cache_mem_k.reshape(DEPTH, batch, N_MEM, MEM_W)
    mem_v = cache_mem_v.reshape(DEPTH, batch, N_MEM, MEM_W)
    ck = cache_k.reshape(cache_k.shape[0], PAGE, TOK_W)
    cv = cache_v.reshape(cache_v.shape[0], PAGE, TOK_W)

    new_c, new_n, new_m = [], [], []
    kv32 = None
    for li in range(DEPTH):
        if li < N_A:
            proj = _proj(x, W['w_a'][li], batch)
            m_pad = jnp.pad(m_all[li], ((0, 0), (0, LANES - N_HEAD)))[:, None, :]
            tok3, c, n, m = _mlstm_sample(proj[:, None, :], W['gate_bias'][li], c_all, n_all, m_pad, li)
            new_c.append(c)
            new_n.append(n)
            new_m.append(m[:, 0, :N_HEAD])
            mq = proj[:, 4 * TOK_W:4 * TOK_W + MEM_W]
        else:
            if li == N_A:
                kv32 = _proj(x, W['w_kv'], batch)
            j = li - N_A
            proj = _proj(x, W['w_b'][j], batch)
            tok3 = _diff_attn_sample(page_table, proj[:, None, :TOK_W], kv32[:, None, :TOK_W],
                                     kv32[:, None, TOK_W:], ck, cv, P['lambda_qk'],
                                     P['subln_g'][j:j + 1], j, _diff_lambda_init(li))
            mq = proj[:, TOK_W:]
        mem_o = _mem_attn_sample(mq[:, None, :], mem_k, mem_v, li)
        x = _layer_tail(x, tok3[:, 0, :], mem_o[:, 0, :], li, P, W, batch, batch)
    y = x.reshape(batch, 1, D_MODEL)
    k_s = kv32[:, :TOK_W].reshape(batch, 1, 2, N_HEAD, DQK)
    v_s = kv32[:, TOK_W:].reshape(batch, 1, N_HEAD, DH)
    return y, jnp.stack(new_c), jnp.stack(new_n), jnp.stack(new_m), k_s, v_s


def kernel(x_prompt, x_sample, mem_prompt, cache_mem_k, cache_mem_v, state_mlstm_C, state_mlstm_n,
           state_mlstm_m, cache_k, cache_v, page_table, w_in_a, b_igate, b_fgate, w_in_b, lambda_qk,
           subln_g, w_kv_shared, w_mem_kv, w_out, ln_mix_g, ln_mix_b, ln_ffn_g, ln_ffn_b, w_group,
           b_group, w_router, b_router, w_e1, w_e3, w_e2):
    P = {'w_in_a': w_in_a, 'b_igate': b_igate, 'b_fgate': b_fgate, 'w_in_b': w_in_b,
         'lambda_qk': lambda_qk, 'subln_g': subln_g, 'w_kv_shared': w_kv_shared, 'w_mem_kv': w_mem_kv,
         'w_out': w_out, 'ln_mix_g': ln_mix_g, 'ln_mix_b': ln_mix_b, 'ln_ffn_g': ln_ffn_g,
         'ln_ffn_b': ln_ffn_b, 'w_group': w_group, 'b_group': b_group, 'w_router': w_router,
         'b_router': b_router, 'w_e1': w_e1, 'w_e3': w_e3, 'w_e2': w_e2}
    W = _prep_weights(P)
    y_p, mem_k_p, mem_v_p, c_p, n_p, m_p, k_p, v_p = _prompt_trunk(x_prompt, mem_prompt, P, W)
    y_s, c_s, n_s, m_s, k_s, v_s = _sample_trunk(
        x_sample, cache_mem_k, cache_mem_v, (state_mlstm_C, state_mlstm_n, state_mlstm_m),
        cache_k, cache_v, page_table, P, W)
    return (y_p, y_s, mem_k_p, mem_v_p, c_p, n_p, m_p, k_p, v_p, c_s, n_s, m_s, k_s, v_s)
```

```python
import functools
import math

import jax
import jax.numpy as jnp
from jax import lax
from jax.experimental import pallas as pl
from jax.experimental.pallas import tpu as pltpu

F32 = jnp.float32
BF16 = jnp.bfloat16

D_MODEL = 1024
DEPTH = 4
N_A = 2
TOK_W = 768
MEM_W = 256
N_HEAD = 6
DH = 128
DQK = 64
H_M = 4
DH_M = 64
N_MEM = 256
CHUNK = 128
N_EXPERTS = 16
D_EXPERT = 256
ALPHA = (2.0 * DEPTH) ** 0.25
LN_EPS = 1e-5
PAGE = 128
LANES = 128
GATE_COL = 3328
A_COLS = 3456
VMEM_LIMIT = 48 * 1024 * 1024


def _cparams(sem):
    return pltpu.CompilerParams(dimension_semantics=sem, vmem_limit_bytes=VMEM_LIMIT)


def _bdot(a, b):
    return jnp.dot(a.astype(BF16), b.astype(BF16), preferred_element_type=F32)


def _bdot_nt(a, b):
    return lax.dot_general(a.astype(BF16), b.astype(BF16), (((1,), (1,)), ((), ())),
                           preferred_element_type=F32)


def _layer_norm(y, g, b):
    mu = jnp.mean(y, axis=-1, keepdims=True)
    d = y - mu
    var = jnp.mean(d * d, axis=-1, keepdims=True)
    return d * lax.rsqrt(var + LN_EPS) * g + b


def _sigmoid(x):
    return 1.0 / (1.0 + jnp.exp(-x))


def _log_sigmoid(x):
    return jnp.minimum(x, 0.0) - jnp.log1p(jnp.exp(-jnp.abs(x)))


def _proj_kernel(x_ref, w_ref, o_ref):
    o_ref[...] = jnp.dot(x_ref[...].astype(BF16), w_ref[...], preferred_element_type=F32)


def _proj(x, w, tm):
    m, k = x.shape
    n = w.shape[1]
    return pl.pallas_call(
        _proj_kernel,
        grid=(m // tm,),
        in_specs=[pl.BlockSpec((tm, k), lambda i: (i, 0)),
                  pl.BlockSpec((k, n), lambda i: (0, 0))],
        out_specs=pl.BlockSpec((tm, n), lambda i: (i, 0)),
        out_shape=jax.ShapeDtypeStruct((m, n), F32),
        compiler_params=_cparams(("arbitrary",)),
        name="proj",
    )(x, w)


def _proj_kv_kernel(x_ref, w_ref, o_ref, kb_ref, vt_ref):
    y = jnp.dot(x_ref[...].astype(BF16), w_ref[...], preferred_element_type=F32)
    o_ref[...] = y
    kb_ref[...] = y[:, :TOK_W].astype(BF16)
    vt_ref[0] = y[:, TOK_W:].T.astype(BF16)


def _proj_shared_kv(x, w, tm):
    m, k = x.shape
    n = w.shape[1]
    return pl.pallas_call(
        _proj_kv_kernel,
        grid=(m // tm,),
        in_specs=[pl.BlockSpec((tm, k), lambda i: (i, 0)),
                  pl.BlockSpec((k, n), lambda i: (0, 0))],
        out_specs=[pl.BlockSpec((tm, n), lambda i: (i, 0)),
                   pl.BlockSpec((tm, TOK_W), lambda i: (i, 0)),
                   pl.BlockSpec((1, TOK_W, tm), lambda i: (i, 0, 0))],
        out_shape=[jax.ShapeDtypeStruct((m, n), F32),
                   jax.ShapeDtypeStruct((m, TOK_W), BF16),
                   jax.ShapeDtypeStruct((m // tm, TOK_W, tm), BF16)],
        compiler_params=_cparams(("arbitrary",)),
        name="proj_kv",
    )(x, w)


def _mlstm_chunk_kernel(q_ref, k_ref, v_ref, o_ref, g_ref, bias_ref,
                        tok_ref, c_ref, n_ref, m_ref):
    @pl.when(pl.program_id(1) == 0)
    def _():
        c_ref[...] = jnp.zeros_like(c_ref)
        n_ref[...] = jnp.zeros_like(n_ref)
        m_ref[...] = jnp.zeros_like(m_ref)

    L = CHUNK
    lane = lax.broadcasted_iota(jnp.int32, (L, LANES), 1)
    row = lax.broadcasted_iota(jnp.int32, (L, L), 0)
    col = lax.broadcasted_iota(jnp.int32, (L, L), 1)
    causal = col <= row

    gb = g_ref[...] + bias_ref[...]
    lf = _log_sigmoid(gb)
    tril = causal.astype(BF16)
    hi = lf.astype(BF16)
    r1 = lf - hi.astype(F32)
    mid = r1.astype(BF16)
    lo = (r1 - mid.astype(F32)).astype(BF16)
    fcum = (jnp.dot(tril, hi, preferred_element_type=F32)
            + jnp.dot(tril, mid, preferred_element_type=F32)
            + jnp.dot(tril, lo, preferred_element_type=F32))
    z = jnp.where(lane < N_HEAD, gb, fcum)
    zt = z.T

    m_row = m_ref[0]
    lane1 = lax.broadcasted_iota(jnp.int32, (1, LANES), 1)
    m_new_row = m_row
    scale = DH ** -0.5
    for h in range(N_HEAD):
        f_col = fcum[:, N_HEAD + h:N_HEAD + h + 1]
        f_row = zt[N_HEAD + h:N_HEAD + h + 1, :]
        ig_row = zt[h:h + 1, :]
        ig_col = gb[:, h:h + 1]
        m_prev = m_row[:, h:h + 1]
        log_d = jnp.where(causal, f_col - f_row + ig_row, -jnp.inf)
        log_prev = f_col + m_prev
        m_t = jnp.maximum(log_prev, jnp.max(log_d, axis=1, keepdims=True))
        dm = jnp.exp(log_d - m_t)
        prev_scale = jnp.exp(log_prev - m_t)

        sl = slice(h * DH, (h + 1) * DH)
        qh = q_ref[:, sl]
        kh = k_ref[:, sl] * scale
        vh = v_ref[:, sl]
        qb = qh.astype(BF16)
        vb = vh.astype(BF16)
        c_h = c_ref[0, h]
        n_h = n_ref[0, h:h + 1, :]
        qk = _bdot_nt(qb, kh) * dm
        num = _bdot(qk, vb) + prev_scale * _bdot(qb, c_h)
        den = (jnp.sum(qk, axis=1, keepdims=True)
               + prev_scale * jnp.sum(qh * n_h, axis=1, keepdims=True))
        hh = num / jnp.maximum(jnp.abs(den), jnp.exp(-m_t))
        tok_ref[:, sl] = _sigmoid(o_ref[:, sl]) * hh

        m_last = m_t[L - 1:L, :]
        f_last = f_col[L - 1:L, :]
        w_src = jnp.exp(f_last + ig_col - f_col - m_last)
        c_scale = jnp.exp(f_last + m_prev - m_last)
        kw = kh * w_src
        c_ref[0, h] = c_scale * c_h + _bdot(kw.T, vb)
        n_ref[0, h:h + 1, :] = c_scale * n_h + jnp.sum(kw, axis=0, keepdims=True)
        m_new_row = jnp.where(lane1 == h, m_last, m_new_row)
    m_ref[0] = m_new_row


def _mlstm_prompt(proj, bias_row, batch, seq):
    nc = seq // CHUNK
    m_tok = batch * seq

    def col_block(j):
        return pl.BlockSpec((CHUNK, TOK_W), lambda b, c: (b * nc + c, j))

    return pl.pallas_call(
        _mlstm_chunk_kernel,
        grid=(batch, nc),
        in_specs=[col_block(0), col_block(1), col_block(2), col_block(3),
                  pl.BlockSpec((CHUNK, LANES), lambda b, c: (b * nc + c, GATE_COL // LANES)),
                  pl.BlockSpec((1, LANES), lambda b, c: (0, 0))],
        out_specs=[pl.BlockSpec((CHUNK, TOK_W), lambda b, c: (b * nc + c, 0)),
                   pl.BlockSpec((1, N_HEAD, DH, DH), lambda b, c: (b, 0, 0, 0)),
                   pl.BlockSpec((1, N_HEAD, DH), lambda b, c: (b, 0, 0)),
                   pl.BlockSpec((1, 1, LANES), lambda b, c: (b, 0, 0))],
        out_shape=[jax.ShapeDtypeStruct((m_tok, TOK_W), F32),
                   jax.ShapeDtypeStruct((batch, N_HEAD, DH, DH), F32),
                   jax.ShapeDtypeStruct((batch, N_HEAD, DH), F32),
                   jax.ShapeDtypeStruct((batch, 1, LANES), F32)],
        compiler_params=_cparams(("arbitrary", "arbitrary")),
        name="mlstm_chunk",
    )(proj, proj, proj, proj, proj, bias_row)


def _mlstm_step_kernel(q_ref, k_ref, v_ref, o_ref, g_ref, bias_ref, c_ref, n_ref, m_ref,
                       tok_ref, c_out, n_out, m_out):
    gb = g_ref[0] + bias_ref[...]
    m_row = m_ref[0]
    lane1 = lax.broadcasted_iota(jnp.int32, (1, LANES), 1)
    row = lax.broadcasted_iota(jnp.int32, (DH, DH), 0)
    col = lax.broadcasted_iota(jnp.int32, (DH, DH), 1)
    eye = row == col
    m_new_row = m_row
    scale = DH ** -0.5
    for h in range(N_HEAD):
        ig = gb[:, h:h + 1]
        lf = _log_sigmoid(gb[:, N_HEAD + h:N_HEAD + h + 1])
        m_prev = m_row[:, h:h + 1]
        log_prev = lf + m_prev
        m_t = jnp.maximum(log_prev, ig)
        dm = jnp.exp(ig - m_t)
        prev_scale = jnp.exp(log_prev - m_t)

        sl = slice(h * DH, (h + 1) * DH)
        q_row = q_ref[0][:, sl]
        k_row = k_ref[0][:, sl] * scale
        v_row = v_ref[0][:, sl]
        c_h = c_ref[0, 0, h]
        n_h = n_ref[0, 0, h:h + 1, :]
        qk = jnp.sum(q_row * k_row, axis=1, keepdims=True) * dm
        q8 = jnp.broadcast_to(q_row, (8, DH))
        q_c = _bdot(q8, c_h)[0:1, :]
        num = qk * v_row + prev_scale * q_c
        den = qk + prev_scale * jnp.sum(q_row * n_h, axis=1, keepdims=True)
        hh = num / jnp.maximum(jnp.abs(den), jnp.exp(-m_t))
        tok_ref[0, :, sl] = _sigmoid(o_ref[0][:, sl]) * hh

        k_col = jnp.sum(jnp.where(eye, jnp.broadcast_to(k_row, (DH, DH)), 0.0), axis=1, keepdims=True)
        c_out[0, h] = prev_scale * c_h + (dm * k_col) * v_row
        n_out[0, h:h + 1, :] = prev_scale * n_h + dm * k_row
        m_new_row = jnp.where(lane1 == h, m_t, m_new_row)
    m_out[0] = m_new_row


def _mlstm_sample(proj3, bias_row, c_all, n_all, m_pad, li):
    batch = proj3.shape[0]

    def col_block(j):
        return pl.BlockSpec((1, 1, TOK_W), lambda b: (b, 0, j))

    return pl.pallas_call(
        _mlstm_step_kernel,
        grid=(batch,),
        in_specs=[col_block(0), col_block(1), col_block(2), col_block(3),
                  pl.BlockSpec((1, 1, LANES), lambda b: (b, 0, GATE_COL // LANES)),
                  pl.BlockSpec((1, LANES), lambda b: (0, 0)),
                  pl.BlockSpec((1, 1, N_HEAD, DH, DH), lambda b: (li, b, 0, 0, 0)),
                  pl.BlockSpec((1, 1, N_HEAD, DH), lambda b: (li, b, 0, 0)),
                  pl.BlockSpec((1, 1, LANES), lambda b: (b, 0, 0))],
        out_specs=[pl.BlockSpec((1, 1, TOK_W), lambda b: (b, 0, 0)),
                   pl.BlockSpec((1, N_HEAD, DH, DH), lambda b: (b, 0, 0, 0)),
                   pl.BlockSpec((1, N_HEAD, DH), lambda b: (b, 0, 0)),
                   pl.BlockSpec((1, 1, LANES), lambda b: (b, 0, 0))],
        out_shape=[jax.ShapeDtypeStruct((batch, 1, TOK_W), F32),
                   jax.ShapeDtypeStruct((batch, N_HEAD, DH, DH), F32),
                   jax.ShapeDtypeStruct((batch, N_HEAD, DH), F32),
                   jax.ShapeDtypeStruct((batch, 1, LANES), F32)],
        compiler_params=_cparams(("arbitrary",)),
        name="mlstm_step",
    )(proj3, proj3, proj3, proj3, proj3, bias_row, c_all, n_all, m_pad)


def _mem_attn_kernel(q_ref, k_ref, v_ref, o_ref):
    outs = []
    for h in range(H_M):
        sl = slice(h * DH_M, (h + 1) * DH_M)
        s = _bdot_nt(q_ref[:, sl], k_ref[0][:, sl]) * (DH_M ** -0.5)
        e = jnp.exp(s - jnp.max(s, axis=1, keepdims=True))
        p = e / jnp.sum(e, axis=1, keepdims=True)
        outs.append(_bdot(p, v_ref[0][:, sl]))
    o_ref[...] = jnp.concatenate(outs, axis=-1)


def _mem_attn_prompt(proj, q_col_block, mem_k, mem_v, seq, tq):
    m_tok = proj.shape[0]
    per_b = seq // tq
    return pl.pallas_call(
        _mem_attn_kernel,
        grid=(m_tok // tq,),
        in_specs=[pl.BlockSpec((tq, MEM_W), lambda i: (i, q_col_block)),
                  pl.BlockSpec((1, N_MEM, MEM_W), lambda i: (i // per_b, 0, 0)),
                  pl.BlockSpec((1, N_MEM, MEM_W), lambda i: (i // per_b, 0, 0))],
        out_specs=pl.BlockSpec((tq, MEM_W), lambda i: (i, 0)),
        out_shape=jax.ShapeDtypeStruct((m_tok, MEM_W), F32),
        compiler_params=_cparams(("arbitrary",)),
        name="mem_attn",
    )(proj, mem_k, mem_v)


def _mem_attn_step_kernel(q_ref, k_ref, v_ref, o_ref):
    rows = 8
    rid = lax.broadcasted_iota(jnp.int32, (rows, MEM_W), 0)
    lane = lax.broadcasted_iota(jnp.int32, (rows, MEM_W), 1)
    head_mask = (lane // DH_M) == rid
    q = jnp.broadcast_to(q_ref[0], (rows, MEM_W))
    qm = jnp.where(head_mask, q, 0.0)
    s = _bdot_nt(qm, k_ref[0, 0]) * (DH_M ** -0.5)
    e = jnp.exp(s - jnp.max(s, axis=1, keepdims=True))
    p = e / jnp.sum(e, axis=1, keepdims=True)
    o = _bdot(p, v_ref[0, 0])
    o_ref[0] = jnp.sum(jnp.where(head_mask, o, 0.0), axis=0, keepdims=True)


def _mem_attn_sample(mq3, cache_k, cache_v, li):
    batch = mq3.shape[0]
    return pl.pallas_call(
        _mem_attn_step_kernel,
        grid=(batch,),
        in_specs=[pl.BlockSpec((1, 1, MEM_W), lambda b: (b, 0, 0)),
                  pl.BlockSpec((1, 1, N_MEM, MEM_W), lambda b: (li, b, 0, 0)),
                  pl.BlockSpec((1, 1, N_MEM, MEM_W), lambda b: (li, b, 0, 0))],
        out_specs=pl.BlockSpec((1, 1, MEM_W), lambda b: (b, 0, 0)),
        out_shape=jax.ShapeDtypeStruct((batch, 1, MEM_W), F32),
        compiler_params=_cparams(("arbitrary",)),
        name="mem_attn_step",
    )(mq3, cache_k, cache_v)


def _routing(logits):
    lane = lax.broadcasted_iota(jnp.int32, logits.shape, 1)
    lane_f = lane.astype(F32)
    big = 1000.0
    is_g = (lane >= N_EXPERTS) & (lane < N_EXPERTS + 4)
    lg = jnp.where(is_g, logits, -jnp.inf)
    gmax = jnp.max(lg, axis=1, keepdims=True)
    gidx = jnp.min(jnp.where(lg == gmax, lane_f, big), axis=1, keepdims=True) - float(N_EXPERTS)
    p_g = 1.0 / jnp.sum(jnp.exp(lg - gmax), axis=1, keepdims=True)
    in_grp = (lane < N_EXPERTS) & ((lane >> 2).astype(F32) == gidx)
    le = jnp.where(in_grp, logits, -jnp.inf)
    v1 = jnp.max(le, axis=1, keepdims=True)
    i1 = jnp.min(jnp.where(le == v1, lane_f, big), axis=1, keepdims=True)
    le2 = jnp.where(lane_f == i1, -jnp.inf, le)
    v2 = jnp.max(le2, axis=1, keepdims=True)
    i2 = jnp.min(jnp.where(le2 == v2, lane_f, big), axis=1, keepdims=True)
    e2 = jnp.exp(v2 - v1)
    inv = 1.0 / (1.0 + e2)
    return jnp.where(lane_f == i1, inv * p_g, jnp.where(lane_f == i2, e2 * inv * p_g, 0.0))


def _mix_kernel(x_ref, tok_ref, mem_ref, wt_ref, wm_ref, g_ref, b_ref, wr_ref, br_ref,
                x1_ref, comb_ref):
    mix = (jnp.dot(tok_ref[...].astype(BF16), wt_ref[...], preferred_element_type=F32)
           + jnp.dot(mem_ref[...].astype(BF16), wm_ref[...], preferred_element_type=F32))
    x1 = _layer_norm(ALPHA * x_ref[...] + mix, g_ref[...], b_ref[...])
    x1_ref[...] = x1
    logits = jnp.dot(x1.astype(BF16), wr_ref[...], preferred_element_type=F32) + br_ref[...]
    comb_ref[...] = _routing(logits)


def _mix(x, tok, mem, wt, wm, g, b, wr, br, tm):
    m_tok = x.shape[0]
    const = lambda i: (0, 0)
    rows = lambda i: (i, 0)
    return pl.pallas_call(
        _mix_kernel,
        grid=(m_tok // tm,),
        in_specs=[pl.BlockSpec((tm, D_MODEL), rows),
                  pl.BlockSpec((tm, TOK_W), rows),
                  pl.BlockSpec((tm, MEM_W), rows),
                  pl.BlockSpec((TOK_W, D_MODEL), const),
                  pl.BlockSpec((MEM_W, D_MODEL), const),
                  pl.BlockSpec((1, D_MODEL), const),
                  pl.BlockSpec((1, D_MODEL), const),
                  pl.BlockSpec((D_MODEL, LANES), const),
                  pl.BlockSpec((1, LANES), const)],
        out_specs=[pl.BlockSpec((tm, D_MODEL), rows),
                   pl.BlockSpec((tm, LANES), rows)],
        out_shape=[jax.ShapeDtypeStruct((m_tok, D_MODEL), F32),
                   jax.ShapeDtypeStruct((m_tok, LANES), F32)],
        compiler_params=_cparams(("arbitrary",)),
        name="mix",
    )(x, tok, mem, wt, wm, g, b, wr, br)


def _moe_kernel(x_ref, comb_ref, w1_ref, w3_ref, w2_ref, g_ref, b_ref, o_ref, xb_ref, acc_ref):
    e = pl.program_id(1)

    @pl.when(e == 0)
    def _():
        xb_ref[...] = x_ref[...].astype(BF16)
        acc_ref[...] = jnp.zeros_like(acc_ref)

    xb = xb_ref[...]
    a = jnp.dot(xb, w1_ref[0, 0].astype(BF16), preferred_element_type=F32)
    b = jnp.dot(xb, w3_ref[0, 0].astype(BF16), preferred_element_type=F32)
    comb = comb_ref[...]
    lane = lax.broadcasted_iota(jnp.int32, comb.shape, 1)
    ce = jnp.sum(jnp.where(lane == e, comb, 0.0), axis=1, keepdims=True)
    hid = (a * _sigmoid(a)) * b * ce
    acc_ref[...] += jnp.dot(hid.astype(BF16), w2_ref[0, 0].astype(BF16), preferred_element_type=F32)

    @pl.when(e == N_EXPERTS - 1)
    def _():
        o_ref[...] = _layer_norm(ALPHA * x_ref[...] + acc_ref[...], g_ref[...], b_ref[...])


def _moe(x1, comb, w1, w3, w2, g, b, li, tm):
    m_tok = x1.shape[0]
    rows = lambda i, e: (i, 0)
    const = lambda i, e: (0, 0)
    return pl.pallas_call(
        _moe_kernel,
        grid=(m_tok // tm, N_EXPERTS),
        in_specs=[pl.BlockSpec((tm, D_MODEL), rows),
                  pl.BlockSpec((tm, LANES), rows),
                  pl.BlockSpec((1, 1, D_MODEL, D_EXPERT), lambda i, e: (li, e, 0, 0)),
                  pl.BlockSpec((1, 1, D_MODEL, D_EXPERT), lambda i, e: (li, e, 0, 0)),
                  pl.BlockSpec((1, 1, D_EXPERT, D_MODEL), lambda i, e: (li, e, 0, 0)),
                  pl.BlockSpec((1, D_MODEL), const),
                  pl.BlockSpec((1, D_MODEL), const)],
        out_specs=pl.BlockSpec((tm, D_MODEL), rows),
        out_shape=jax.ShapeDtypeStruct((m_tok, D_MODEL), F32),
        scratch_shapes=[pltpu.VMEM((tm, D_MODEL), BF16), pltpu.VMEM((tm, D_MODEL), F32)],
        compiler_params=_cparams(("arbitrary", "arbitrary")),
        name="moe",
    )(x1, comb, w1, w3, w2, g, b)


def _diff_lambda(lq_ref, lam_init):
    lq = lq_ref[0]
    a = jnp.sum(lq[0:1, :] * lq[1:2, :], axis=1, keepdims=True)
    b = jnp.sum(lq[2:3, :] * lq[3:4, :], axis=1, keepdims=True)
    return jnp.exp(a) - jnp.exp(b) + lam_init


def _sub_norm(o, g_row, lam_init):
    o = o * lax.rsqrt(jnp.mean(o * o, axis=-1, keepdims=True) + LN_EPS) * g_row
    return o * (1.0 - lam_init)


def _diff_attn_kernel(q1_ref, q2_ref, k1_ref, k2_ref, vt_ref, lq_ref, gs_ref, o_ref,
                      m_scr, l_scr, acc_scr, *, tq, lam_init):
    i = pl.program_id(2)
    lam = _diff_lambda(lq_ref, lam_init)
    qs = [(q1_ref[...] * (DQK ** -0.5)).astype(BF16), (q2_ref[...] * (DQK ** -0.5)).astype(BF16)]
    k_refs = [k1_ref, k2_ref]
    m_scr[...] = jnp.full_like(m_scr, -jnp.inf)
    l_scr[...] = jnp.zeros_like(l_scr)
    acc_scr[...] = jnp.zeros_like(acc_scr)
    key_i = lax.broadcasted_iota(jnp.int32, (tq, tq), 0)
    qry_i = lax.broadcasted_iota(jnp.int32, (tq, tq), 1)
    causal = key_i <= qry_i

    def block(j, masked):
        start = pl.multiple_of(j * tq, tq)
        sts = []
        for hh in range(2):
            for mm in range(2):
                kb = k_refs[mm][pl.ds(start, tq), hh * DQK:(hh + 1) * DQK]
                st = lax.dot_general(kb, qs[mm][:, hh * DQK:(hh + 1) * DQK],
                                     (((1,), (1,)), ((), ())), preferred_element_type=F32)
                if masked:
                    st = jnp.where(causal, st, -jnp.inf)
                sts.append(st)
        m_all = m_scr[...]
        l_all = l_scr[...]
        ps, alphas, m_rows, l_rows = [], [], [], []
        for idx in range(4):
            m_old = m_all[idx:idx + 1, :]
            m_new = jnp.maximum(m_old, jnp.max(sts[idx], axis=0, keepdims=True))
            alpha = jnp.exp(m_old - m_new)
            p = jnp.exp(sts[idx] - m_new)
            l_rows.append(alpha * l_all[idx:idx + 1, :] + jnp.sum(p, axis=0, keepdims=True))
            m_rows.append(m_new)
            alphas.append(alpha)
            ps.append(p.astype(BF16))
        m_scr[...] = jnp.concatenate(m_rows + [m_all[4:, :]], axis=0)
        l_scr[...] = jnp.concatenate(l_rows + [l_all[4:, :]], axis=0)
        for idx in range(4):
            vt = vt_ref[j, (idx // 2) * DH:(idx // 2 + 1) * DH, :]
            acc_scr[idx] = alphas[idx] * acc_scr[idx] + jnp.dot(vt, ps[idx],
                                                                preferred_element_type=F32)

    def body(j, carry):
        block(j, False)
        return carry

    lax.fori_loop(0, i, body, 0)
    block(i, True)

    for hh in range(2):
        o1 = acc_scr[hh * 2] / l_scr[hh * 2:hh * 2 + 1, :]
        o2 = acc_scr[hh * 2 + 1] / l_scr[hh * 2 + 1:hh * 2 + 2, :]
        o = o1 - lam * o2
        o = o * lax.rsqrt(jnp.mean(o * o, axis=0, keepdims=True) + LN_EPS)
        o_ref[:, hh * DH:(hh + 1) * DH] = o.T * gs_ref[...] * (1.0 - lam_init)


def _diff_attn_prompt(proj_b, k_b, vt_b, lam_qk, g_sub, j, lam_init, batch, seq, tq):
    m_tok = batch * seq
    nq = seq // tq
    kern = functools.partial(_diff_attn_kernel, tq=tq, lam_init=lam_init)
    return pl.pallas_call(
        kern,
        grid=(batch, 3, nq),
        in_specs=[pl.BlockSpec((tq, 2 * DQK), lambda b, hp, i: (b * nq + i, hp)),
                  pl.BlockSpec((tq, 2 * DQK), lambda b, hp, i: (b * nq + i, 3 + hp)),
                  pl.BlockSpec((seq, 2 * DQK), lambda b, hp, i: (b, hp)),
                  pl.BlockSpec((seq, 2 * DQK), lambda b, hp, i: (b, 3 + hp)),
                  pl.BlockSpec((nq, 2 * DH, tq), lambda b, hp, i: (b, hp, 0)),
                  pl.BlockSpec((1, 4, DQK), lambda b, hp, i: (j, 0, 0)),
                  pl.BlockSpec((1, DH), lambda b, hp, i: (0, 0))],
        out_specs=pl.BlockSpec((tq, 2 * DH), lambda b, hp, i: (b * nq + i, hp)),
        out_shape=jax.ShapeDtypeStruct((m_tok, TOK_W), F32),
        scratch_shapes=[pltpu.VMEM((8, tq), F32), pltpu.VMEM((8, tq), F32),
                        pltpu.VMEM((4, DH, tq), F32)],
        compiler_params=_cparams(("arbitrary", "arbitrary", "arbitrary")),
        name="diff_attn",
    )(proj_b, proj_b, k_b, k_b, vt_b, lam_qk, g_sub)


PAGES_PER_STEP = 8


def _diff_attn_step_kernel(pt_ref, q_ref, kn_ref, vn_ref, *rest, n_steps, lam_init):
    k_refs = rest[:PAGES_PER_STEP]
    v_refs = rest[PAGES_PER_STEP:2 * PAGES_PER_STEP]
    lq_ref, gs_ref, o_ref, qm_scr, m_scr, l_scr, acc_scr = rest[2 * PAGES_PER_STEP:]
    p_idx = pl.program_id(1)
    rows = 16
    rid = lax.broadcasted_iota(jnp.int32, (rows, TOK_W), 0)
    lane = lax.broadcasted_iota(jnp.int32, (rows, TOK_W), 1)

    @pl.when(p_idx == 0)
    def _():
        q = jnp.broadcast_to(q_ref[0] * (DQK ** -0.5), (rows, TOK_W))
        qm_scr[...] = jnp.where((lane // DQK) == rid, q, 0.0)
        m_scr[...] = jnp.full_like(m_scr, -jnp.inf)
        l_scr[...] = jnp.zeros_like(l_scr)
        acc_scr[...] = jnp.zeros_like(acc_scr)

    qm = qm_scr[...]
    qb = qm.astype(BF16)
    s = jnp.concatenate([_bdot_nt(qb, k_ref[0]) for k_ref in k_refs], axis=1)
    m_old = m_scr[...]
    m_new = jnp.maximum(m_old, jnp.max(s, axis=1, keepdims=True))
    alpha = jnp.exp(m_old - m_new)
    p = jnp.exp(s - m_new)
    pb = p.astype(BF16)
    pv = _bdot(pb[:, :PAGE], v_refs[0][0])
    for g in range(1, PAGES_PER_STEP):
        pv = pv + _bdot(pb[:, g * PAGE:(g + 1) * PAGE], v_refs[g][0])
    l_scr[...] = alpha * l_scr[...] + jnp.sum(p, axis=1, keepdims=True)
    acc_scr[...] = alpha * acc_scr[...] + pv
    m_scr[...] = m_new

    @pl.when(p_idx == n_steps - 1)
    def _():
        lam = _diff_lambda(lq_ref, lam_init)
        s_new = jnp.sum(qm * kn_ref[0], axis=1, keepdims=True)
        m_old2 = m_scr[...]
        m_fin = jnp.maximum(m_old2, s_new)
        a2 = jnp.exp(m_old2 - m_fin)
        p_new = jnp.exp(s_new - m_fin)
        l_fin = a2 * l_scr[...] + p_new
        acc = a2 * acc_scr[...] + p_new * vn_ref[0]
        r = acc / l_fin
        head_of_lane = lane // DH
        o1 = jnp.sum(jnp.where(head_of_lane == rid, r, 0.0), axis=0, keepdims=True)
        o2 = jnp.sum(jnp.where(head_of_lane == rid - N_HEAD, r, 0.0), axis=0, keepdims=True)
        o = o1 - lam * o2
        for h in range(N_HEAD):
            sl = slice(h * DH, (h + 1) * DH)
            o_ref[0, :, sl] = _sub_norm(o[:, sl], gs_ref[...], lam_init)


def _diff_attn_sample(page_table, q3, kn3, vn3, cache_k, cache_v, lam_qk, g_sub, j, lam_init):
    batch, n_pages = page_table.shape
    n_steps = n_pages // PAGES_PER_STEP
    kern = functools.partial(_diff_attn_step_kernel, n_steps=n_steps, lam_init=lam_init)
    row3 = lambda b, p, pt: (b, 0, 0)

    def page_spec(g):
        return pl.BlockSpec((1, PAGE, TOK_W), lambda b, p, pt: (pt[b, p * PAGES_PER_STEP + g], 0, 0))

    page_specs = [page_spec(g) for g in range(PAGES_PER_STEP)]
    grid_spec = pltpu.PrefetchScalarGridSpec(
        num_scalar_prefetch=1,
        grid=(batch, n_steps),
        in_specs=[pl.BlockSpec((1, 1, TOK_W), row3),
                  pl.BlockSpec((1, 1, TOK_W), row3),
                  pl.BlockSpec((1, 1, TOK_W), row3)] + page_specs + page_specs + [
                  pl.BlockSpec((1, 4, DQK), lambda b, p, pt: (j, 0, 0)),
                  pl.BlockSpec((1, DH), lambda b, p, pt: (0, 0))],
        out_specs=pl.BlockSpec((1, 1, TOK_W), row3),
        scratch_shapes=[pltpu.VMEM((16, TOK_W), F32), pltpu.VMEM((16, 1), F32),
                        pltpu.VMEM((16, 1), F32), pltpu.VMEM((16, TOK_W), F32)],
    )
    return pl.pallas_call(
        kern,
        grid_spec=grid_spec,
        out_shape=jax.ShapeDtypeStruct((batch, 1, TOK_W), F32),
        compiler_params=_cparams(("arbitrary", "arbitrary")),
        name="diff_attn_step",
    )(page_table, q3, kn3, vn3, *([cache_k] * PAGES_PER_STEP), *([cache_v] * PAGES_PER_STEP),
      lam_qk, g_sub)


def _diff_lambda_init(li):
    return 0.8 - 0.6 * math.exp(-0.3 * li)


def _prep_weights(P):
    w = {}
    wa = P['w_in_a']
    gates = jnp.pad(wa[:, :, 4 * TOK_W:4 * TOK_W + 2 * N_HEAD], ((0, 0), (0, 0), (0, LANES - 2 * N_HEAD)))
    w['w_a'] = jnp.concatenate([wa[:, :, :4 * TOK_W], wa[:, :, 4 * TOK_W + 2 * N_HEAD:], gates],
                               axis=-1).astype(BF16)
    w['gate_bias'] = jnp.pad(jnp.concatenate([P['b_igate'], P['b_fgate']], axis=-1),
                             ((0, 0), (0, LANES - 2 * N_HEAD)))[:, None, :]
    w['w_b'] = P['w_in_b'].astype(BF16)
    w['w_kv'] = P['w_kv_shared'].astype(BF16)
    w['w_mem'] = jnp.transpose(P['w_mem_kv'], (1, 0, 2)).reshape(D_MODEL, DEPTH * 2 * MEM_W).astype(BF16)
    w['w_out_t'] = P['w_out'][:, :TOK_W, :].astype(BF16)
    w['w_out_m'] = P['w_out'][:, TOK_W:, :].astype(BF16)
    wr = jnp.concatenate([P['w_router'], P['w_group']], axis=-1)
    w['w_r'] = jnp.pad(wr, ((0, 0), (0, 0), (0, LANES - wr.shape[-1]))).astype(BF16)
    br = jnp.concatenate([P['b_router'], P['b_group']], axis=-1)
    w['b_r'] = jnp.pad(br, ((0, 0), (0, LANES - br.shape[-1])))[:, None, :]
    return w


def _layer_tail(x, tok, mem, li, P, W, tm_mix, tm_moe):
    x1, comb = _mix(x, tok, mem, W['w_out_t'][li], W['w_out_m'][li],
                    P['ln_mix_g'][li][None, :], P['ln_mix_b'][li][None, :],
                    W['w_r'][li], W['b_r'][li], tm_mix)
    return _moe(x1, comb, P['w_e1'], P['w_e3'], P['w_e2'],
                P['ln_ffn_g'][li][None, :], P['ln_ffn_b'][li][None, :], li, tm_moe)


def _prompt_trunk(x_prompt, mem_prompt, P, W):
    batch, seq, _ = x_prompt.shape
    m_tok = batch * seq
    x = x_prompt.reshape(m_tok, D_MODEL)
    mem = mem_prompt.reshape(batch * N_MEM, D_MODEL)
    kv_mem = _proj(mem, W['w_mem'], 512)
    kv_mem = jnp.transpose(kv_mem.reshape(batch, N_MEM, DEPTH, 2, MEM_W), (3, 2, 0, 1, 4))
    mem_k, mem_v = kv_mem[0], kv_mem[1]

    new_c, new_n, new_m = [], [], []
    kv32 = kv16 = None
    for li in range(DEPTH):
        if li < N_A:
            proj = _proj(x, W['w_a'][li], 256)
            tok, c, n, m = _mlstm_prompt(proj, W['gate_bias'][li], batch, seq)
            new_c.append(c)
            new_n.append(n)
            new_m.append(m[:, 0, :N_HEAD])
            mq_block = (4 * TOK_W) // MEM_W
        else:
            if li == N_A:
                kv32, k16, vt16 = _proj_shared_kv(x, W['w_kv'], 256)
            j = li - N_A
            proj = _proj(x, W['w_b'][j], 512)
            tok = _diff_attn_prompt(proj, k16, vt16, P['lambda_qk'], P['subln_g'][j:j + 1], j,
                                    _diff_lambda_init(li), batch, seq, 256)
            mq_block = TOK_W // MEM_W
        mem_o = _mem_attn_prompt(proj, mq_block, mem_k[li], mem_v[li], seq, 512)
        x = _layer_tail(x, tok, mem_o, li, P, W, 512, 1024)
    y = x.reshape(batch, seq, D_MODEL)
    k_p = kv32[:, :2 * N_HEAD * DQK].reshape(batch, seq, 2, N_HEAD, DQK)
    v_p = kv32[:, 2 * N_HEAD * DQK:].reshape(batch, seq, N_HEAD, DH)
    shp = (DEPTH, batch, N_MEM, H_M, DH_M)
    return (y, mem_k.reshape(shp), mem_v.reshape(shp),
            jnp.stack(new_c), jnp.stack(new_n), jnp.stack(new_m), k_p, v_p)


def _sample_trunk(x_sample, cache_mem_k, cache_mem_v, state, cache_k, cache_v, page_table, P, W):
    batch = x_sample.shape[0]
    x = x_sample.reshape(batch, D_MODEL)
    c_all, n_all, m_all = state
    mem_k = cache_mem_k.reshape(DEPTH, batch, N_MEM, MEM_W)
    mem_v = cache_mem_v.reshape(DEPTH, batch, N_MEM, MEM_W)
    ck = cache_k.astype(BF16).reshape(cache_k.shape[0], PAGE, TOK_W)
    cv = cache_v.astype(BF16).reshape(cache_v.shape[0], PAGE, TOK_W)

    new_c, new_n, new_m = [], [], []
    kv32 = None
    for li in range(DEPTH):
        if li < N_A:
            proj = _proj(x, W['w_a'][li], batch)
            m_pad = jnp.pad(m_all[li], ((0, 0), (0, LANES - N_HEAD)))[:, None, :]
            tok3, c, n, m = _mlstm_sample(proj[:, None, :], W['gate_bias'][li], c_all, n_all, m_pad, li)
            new_c.append(c)
            new_n.append(n)
            new_m.append(m[:, 0, :N_HEAD])
            mq = proj[:, 4 * TOK_W:4 * TOK_W + MEM_W]
        else:
            if li == N_A:
                kv32 = _proj(x, W['w_kv'], batch)
            j = li - N_A
            proj = _proj(x, W['w_b'][j], batch)
            tok3 = _diff_attn_sample(page_table, proj[:, None, :TOK_W], kv32[:, None, :TOK_W],
                                     kv32[:, None, TOK_W:], ck, cv, P['lambda_qk'],
                                     P['subln_g'][j:j + 1], j, _diff_lambda_init(li))
            mq = proj[:, TOK_W:]
        mem_o = _mem_attn_sample(mq[:, None, :], mem_k, mem_v, li)
        x = _layer_tail(x, tok3[:, 0, :], mem_o[:, 0, :], li, P, W, batch, batch)
    y = x.reshape(batch, 1, D_MODEL)
    k_s = kv32[:, :TOK_W].reshape(batch, 1, 2, N_HEAD, DQK)
    v_s = kv32[:, TOK_W:].reshape(batch, 1, N_HEAD, DH)
    return y, jnp.stack(new_c), jnp.stack(new_n), jnp.stack(new_m), k_s, v_s


def kernel(x_prompt, x_sample, mem_prompt, cache_mem_k, cache_mem_v, state_mlstm_C, state_mlstm_n,
           state_mlstm_m, cache_k, cache_v, page_table, w_in_a, b_igate, b_fgate, w_in_b, lambda_qk,
           subln_g, w_kv_shared, w_mem_kv, w_out, ln_mix_g, ln_mix_b, ln_ffn_g, ln_ffn_b, w_group,
           b_group, w_router, b_router, w_e1, w_e3, w_e2):
    P = {'w_in_a': w_in_a, 'b_igate': b_igate, 'b_fgate': b_fgate, 'w_in_b': w_in_b,
         'lambda_qk': lambda_qk, 'subln_g': subln_g, 'w_kv_shared': w_kv_shared, 'w_mem_kv': w_mem_kv,
         'w_out': w_out, 'ln_mix_g': ln_mix_g, 'ln_mix_b': ln_mix_b, 'ln_ffn_g': ln_ffn_g,
         'ln_ffn_b': ln_ffn_b, 'w_group': w_group, 'b_group': b_group, 'w_router': w_router,
         'b_router': b_router, 'w_e1': w_e1, 'w_e3': w_e3, 'w_e2': w_e2}
    W = _prep_weights(P)
    y_p, mem_k_p, mem_v_p, c_p, n_p, m_p, k_p, v_p = _prompt_trunk(x_prompt, mem_prompt, P, W)
    y_s, c_s, n_s, m_s, k_s, v_s = _sample_trunk(
        x_sample, cache_mem_k, cache_mem_v, (state_mlstm_C, state_mlstm_n, state_mlstm_m),
        cache_k, cache_v, page_table, P, W)
    return (y_p, y_s, mem_k_p, mem_v_p, c_p, n_p, m_p, k_p, v_p, c_s, n_s, m_s, k_s, v_s)
```

```python
import functools
import math

import jax
import jax.numpy as jnp
from jax import lax
from jax.experimental import pallas as pl
from jax.experimental.pallas import tpu as pltpu

F32 = jnp.float32
BF16 = jnp.bfloat16

D_MODEL = 1024
DEPTH = 4
N_A = 2
TOK_W = 768
MEM_W = 256
N_HEAD = 6
DH = 128
DQK = 64
H_M = 4
DH_M = 64
N_MEM = 256
CHUNK = 128
N_EXPERTS = 16
D_EXPERT = 256
ALPHA = (2.0 * DEPTH) ** 0.25
LN_EPS = 1e-5
PAGE = 128
LANES = 128
GATE_COL = 3328
A_COLS = 3456
VMEM_LIMIT = 48 * 1024 * 1024


def _cparams(sem):
    return pltpu.CompilerParams(dimension_semantics=sem, vmem_limit_bytes=VMEM_LIMIT)


def _bdot(a, b):
    return jnp.dot(a.astype(BF16), b.astype(BF16), preferred_element_type=F32)


def _bdot_nt(a, b):
    return lax.dot_general(a.astype(BF16), b.astype(BF16), (((1,), (1,)), ((), ())),
                           preferred_element_type=F32)


def _split3(a):
    hi = a.astype(BF16)
    r = a - hi.astype(F32)
    mid = r.astype(BF16)
    lo = (r - mid.astype(F32)).astype(BF16)
    return hi, mid, lo


def _split2(a):
    hi = a.astype(BF16)
    return hi, (a - hi.astype(F32)).astype(BF16)


def _dot_hp(a, b, nt=False):
    m = a.shape[0]
    a_hi, a_mid, a_lo = _split3(a)
    b_hi, b_mid, b_lo = _split3(b)
    dot = _bdot_nt if nt else _bdot
    a3 = jnp.concatenate([a_hi, a_mid, a_lo], axis=0)
    r1 = dot(a3, b_hi)
    r2 = dot(a3[:2 * m], b_mid)
    r3 = dot(a_hi, b_lo)
    return ((r3 + r2[m:]) + (r1[2 * m:] + r2[:m]) + r1[m:2 * m]) + r1[:m]


def _dot_x3(a, b, nt=False):
    m = a.shape[0]
    a_hi, a_lo = _split2(a)
    b_hi, b_lo = _split2(b)
    dot = _bdot_nt if nt else _bdot
    r1 = dot(jnp.concatenate([a_hi, a_lo], axis=0), b_hi)
    return (dot(a_hi, b_lo) + r1[m:]) + r1[:m]


def _layer_norm(y, g, b):
    mu = jnp.mean(y, axis=-1, keepdims=True)
    d = y - mu
    var = jnp.mean(d * d, axis=-1, keepdims=True)
    return d * lax.rsqrt(var + LN_EPS) * g + b


def _sigmoid(x):
    return 1.0 / (1.0 + jnp.exp(-x))


def _log_sigmoid(x):
    return jnp.minimum(x, 0.0) - jnp.log1p(jnp.exp(-jnp.abs(x)))


def _proj_kernel(x_ref, w_ref, o_ref, *, hp):
    if hp:
        o_ref[...] = _dot_hp(x_ref[...], w_ref[...])
    else:
        o_ref[...] = jnp.dot(x_ref[...].astype(BF16), w_ref[...], preferred_element_type=F32)


def _proj(x, w, tm, tn=None, hp=False):
    m, k = x.shape
    n = w.shape[1]
    tn = n if tn is None else tn
    return pl.pallas_call(
        functools.partial(_proj_kernel, hp=hp),
        grid=(m // tm, n // tn),
        in_specs=[pl.BlockSpec((tm, k), lambda i, j: (i, 0)),
                  pl.BlockSpec((k, tn), lambda i, j: (0, j))],
        out_specs=pl.BlockSpec((tm, tn), lambda i, j: (i, j)),
        out_shape=jax.ShapeDtypeStruct((m, n), F32),
        compiler_params=_cparams(("arbitrary", "arbitrary")),
        name="proj_hp" if hp else "proj",
    )(x, w)


def _proj_kv_kernel(x_ref, w_ref, kt_ref, vh_ref, kb_ref, vt_ref):
    y = jnp.dot(x_ref[...].astype(BF16), w_ref[...], preferred_element_type=F32)
    k = y[:, :TOK_W]
    v = y[:, TOK_W:]
    kt_ref[0] = k.T
    for h in range(N_HEAD):
        vh_ref[0, h] = v[:, h * DH:(h + 1) * DH]
    kb_ref[...] = k.astype(BF16)
    vt_ref[0] = v.T.astype(BF16)


def _proj_shared_kv(x, w, batch, seq, tm):
    m, k = x.shape
    n = w.shape[1]
    per_b = seq // tm
    return pl.pallas_call(
        _proj_kv_kernel,
        grid=(m // tm,),
        in_specs=[pl.BlockSpec((tm, k), lambda i: (i, 0)),
                  pl.BlockSpec((k, n), lambda i: (0, 0))],
        out_specs=[pl.BlockSpec((1, TOK_W, tm), lambda i: (i // per_b, 0, i % per_b)),
                   pl.BlockSpec((1, N_HEAD, tm, DH), lambda i: (i // per_b, 0, i % per_b, 0)),
                   pl.BlockSpec((tm, TOK_W), lambda i: (i, 0)),
                   pl.BlockSpec((1, TOK_W, tm), lambda i: (i, 0, 0))],
        out_shape=[jax.ShapeDtypeStruct((batch, TOK_W, seq), F32),
                   jax.ShapeDtypeStruct((batch, N_HEAD, seq, DH), F32),
                   jax.ShapeDtypeStruct((m, TOK_W), BF16),
                   jax.ShapeDtypeStruct((m // tm, TOK_W, tm), BF16)],
        compiler_params=_cparams(("arbitrary",)),
        name="proj_kv",
    )(x, w)


def _mlstm_chunk_kernel(q_ref, k_ref, v_ref, o_ref, g_ref, bias_ref,
                        tok_ref, c_ref, n_ref, m_ref):
    @pl.when(pl.program_id(1) == 0)
    def _():
        c_ref[...] = jnp.zeros_like(c_ref)
        n_ref[...] = jnp.zeros_like(n_ref)
        m_ref[...] = jnp.zeros_like(m_ref)

    L = CHUNK
    lane = lax.broadcasted_iota(jnp.int32, (L, LANES), 1)
    row = lax.broadcasted_iota(jnp.int32, (L, L), 0)
    col = lax.broadcasted_iota(jnp.int32, (L, L), 1)
    causal = col <= row

    gb = g_ref[...] + bias_ref[...]
    lf = _log_sigmoid(gb)
    tril = causal.astype(BF16)
    hi = lf.astype(BF16)
    r1 = lf - hi.astype(F32)
    mid = r1.astype(BF16)
    lo = (r1 - mid.astype(F32)).astype(BF16)
    fcum = (jnp.dot(tril, hi, preferred_element_type=F32)
            + jnp.dot(tril, mid, preferred_element_type=F32)
            + jnp.dot(tril, lo, preferred_element_type=F32))
    z = jnp.where(lane < N_HEAD, gb, fcum)
    zt = z.T

    m_row = m_ref[0]
    lane1 = lax.broadcasted_iota(jnp.int32, (1, LANES), 1)
    m_new_row = m_row
    scale = DH ** -0.5
    for h in range(N_HEAD):
        f_col = fcum[:, N_HEAD + h:N_HEAD + h + 1]
        f_row = zt[N_HEAD + h:N_HEAD + h + 1, :]
        ig_row = zt[h:h + 1, :]
        ig_col = gb[:, h:h + 1]
        m_prev = m_row[:, h:h + 1]
        log_d = jnp.where(causal, f_col - f_row + ig_row, -jnp.inf)
        log_prev = f_col + m_prev
        m_t = jnp.maximum(log_prev, jnp.max(log_d, axis=1, keepdims=True))
        dm = jnp.exp(log_d - m_t)
        prev_scale = jnp.exp(log_prev - m_t)

        sl = slice(h * DH, (h + 1) * DH)
        qh = q_ref[:, sl]
        kh = k_ref[:, sl] * scale
        vh = v_ref[:, sl]
        qb = qh.astype(BF16)
        vb = vh.astype(BF16)
        c_h = c_ref[0, h]
        n_h = n_ref[0, h:h + 1, :]
        qk = _bdot_nt(qb, kh) * dm
        num = _bdot(qk, vb) + prev_scale * _bdot(qb, c_h)
        den = (jnp.sum(qk, axis=1, keepdims=True)
               + prev_scale * jnp.sum(qh * n_h, axis=1, keepdims=True))
        hh = num / jnp.maximum(jnp.abs(den), jnp.exp(-m_t))
        tok_ref[:, sl] = _sigmoid(o_ref[:, sl]) * hh

        m_last = m_t[L - 1:L, :]
        f_last = f_col[L - 1:L, :]
        w_src = jnp.exp(f_last + ig_col - f_col - m_last)
        c_scale = jnp.exp(f_last + m_prev - m_last)
        kw = kh * w_src
        c_ref[0, h] = c_scale * c_h + _bdot(kw.T, vb)
        n_ref[0, h:h + 1, :] = c_scale * n_h + jnp.sum(kw, axis=0, keepdims=True)
        m_new_row = jnp.where(lane1 == h, m_last, m_new_row)
    m_ref[0] = m_new_row


def _mlstm_prompt(proj, bias_row, batch, seq):
    nc = seq // CHUNK
    m_tok = batch * seq

    def col_block(j):
        return pl.BlockSpec((CHUNK, TOK_W), lambda b, c: (b * nc + c, j))

    return pl.pallas_call(
        _mlstm_chunk_kernel,
        grid=(batch, nc),
        in_specs=[col_block(0), col_block(1), col_block(2), col_block(3),
                  pl.BlockSpec((CHUNK, LANES), lambda b, c: (b * nc + c, GATE_COL // LANES)),
                  pl.BlockSpec((1, LANES), lambda b, c: (0, 0))],
        out_specs=[pl.BlockSpec((CHUNK, TOK_W), lambda b, c: (b * nc + c, 0)),
                   pl.BlockSpec((1, N_HEAD, DH, DH), lambda b, c: (b, 0, 0, 0)),
                   pl.BlockSpec((1, N_HEAD, DH), lambda b, c: (b, 0, 0)),
                   pl.BlockSpec((1, 1, LANES), lambda b, c: (b, 0, 0))],
        out_shape=[jax.ShapeDtypeStruct((m_tok, TOK_W), F32),
                   jax.ShapeDtypeStruct((batch, N_HEAD, DH, DH), F32),
                   jax.ShapeDtypeStruct((batch, N_HEAD, DH), F32),
                   jax.ShapeDtypeStruct((batch, 1, LANES), F32)],
        compiler_params=_cparams(("arbitrary", "arbitrary")),
        name="mlstm_chunk",
    )(proj, proj, proj, proj, proj, bias_row)


def _mlstm_step_kernel(q_ref, k_ref, v_ref, o_ref, g_ref, bias_ref, c_ref, n_ref, m_ref,
                       tok_ref, c_out, n_out, m_out):
    gb = g_ref[0] + bias_ref[...]
    m_row = m_ref[0]
    lane1 = lax.broadcasted_iota(jnp.int32, (1, LANES), 1)
    row = lax.broadcasted_iota(jnp.int32, (DH, DH), 0)
    col = lax.broadcasted_iota(jnp.int32, (DH, DH), 1)
    eye = row == col
    m_new_row = m_row
    scale = DH ** -0.5
    for h in range(N_HEAD):
        ig = gb[:, h:h + 1]
        lf = _log_sigmoid(gb[:, N_HEAD + h:N_HEAD + h + 1])
        m_prev = m_row[:, h:h + 1]
        log_prev = lf + m_prev
        m_t = jnp.maximum(log_prev, ig)
        dm = jnp.exp(ig - m_t)
        prev_scale = jnp.exp(log_prev - m_t)

        sl = slice(h * DH, (h + 1) * DH)
        q_row = q_ref[0][:, sl]
        k_row = k_ref[0][:, sl] * scale
        v_row = v_ref[0][:, sl]
        c_h = c_ref[0, 0, h]
        n_h = n_ref[0, 0, h:h + 1, :]
        qk = jnp.sum(q_row * k_row, axis=1, keepdims=True) * dm
        q_c = _dot_hp(jnp.broadcast_to(q_row, (16, DH)), c_h)[0:1, :]
        num = qk * v_row + prev_scale * q_c
        den = qk + prev_scale * jnp.sum(q_row * n_h, axis=1, keepdims=True)
        hh = num / jnp.maximum(jnp.abs(den), jnp.exp(-m_t))
        tok_ref[0, :, sl] = _sigmoid(o_ref[0][:, sl]) * hh

        k_col = jnp.sum(jnp.where(eye, jnp.broadcast_to(k_row, (DH, DH)), 0.0), axis=1, keepdims=True)
        c_out[0, h] = prev_scale * c_h + (dm * k_col) * v_row
        n_out[0, h:h + 1, :] = prev_scale * n_h + dm * k_row
        m_new_row = jnp.where(lane1 == h, m_t, m_new_row)
    m_out[0] = m_new_row


def _mlstm_sample(proj3, bias_row, c_all, n_all, m_pad, li):
    batch = proj3.shape[0]

    def col_block(j):
        return pl.BlockSpec((1, 1, TOK_W), lambda b: (b, 0, j))

    return pl.pallas_call(
        _mlstm_step_kernel,
        grid=(batch,),
        in_specs=[col_block(0), col_block(1), col_block(2), col_block(3),
                  pl.BlockSpec((1, 1, LANES), lambda b: (b, 0, GATE_COL // LANES)),
                  pl.BlockSpec((1, LANES), lambda b: (0, 0)),
                  pl.BlockSpec((1, 1, N_HEAD, DH, DH), lambda b: (li, b, 0, 0, 0)),
                  pl.BlockSpec((1, 1, N_HEAD, DH), lambda b: (li, b, 0, 0)),
                  pl.BlockSpec((1, 1, LANES), lambda b: (b, 0, 0))],
        out_specs=[pl.BlockSpec((1, 1, TOK_W), lambda b: (b, 0, 0)),
                   pl.BlockSpec((1, N_HEAD, DH, DH), lambda b: (b, 0, 0, 0)),
                   pl.BlockSpec((1, N_HEAD, DH), lambda b: (b, 0, 0)),
                   pl.BlockSpec((1, 1, LANES), lambda b: (b, 0, 0))],
        out_shape=[jax.ShapeDtypeStruct((batch, 1, TOK_W), F32),
                   jax.ShapeDtypeStruct((batch, N_HEAD, DH, DH), F32),
                   jax.ShapeDtypeStruct((batch, N_HEAD, DH), F32),
                   jax.ShapeDtypeStruct((batch, 1, LANES), F32)],
        compiler_params=_cparams(("arbitrary",)),
        name="mlstm_step",
    )(proj3, proj3, proj3, proj3, proj3, bias_row, c_all, n_all, m_pad)


def _mem_attn_kernel(q_ref, k_ref, v_ref, o_ref):
    outs = []
    for h in range(H_M):
        sl = slice(h * DH_M, (h + 1) * DH_M)
        s = _bdot_nt(q_ref[:, sl], k_ref[0][:, sl]) * (DH_M ** -0.5)
        e = jnp.exp(s - jnp.max(s, axis=1, keepdims=True))
        p = e / jnp.sum(e, axis=1, keepdims=True)
        outs.append(_bdot(p, v_ref[0][:, sl]))
    o_ref[...] = jnp.concatenate(outs, axis=-1)


def _mem_attn_prompt(proj, q_col_block, mem_k, mem_v, seq, tq):
    m_tok = proj.shape[0]
    per_b = seq // tq
    return pl.pallas_call(
        _mem_attn_kernel,
        grid=(m_tok // tq,),
        in_specs=[pl.BlockSpec((tq, MEM_W), lambda i: (i, q_col_block)),
                  pl.BlockSpec((1, N_MEM, MEM_W), lambda i: (i // per_b, 0, 0)),
                  pl.BlockSpec((1, N_MEM, MEM_W), lambda i: (i // per_b, 0, 0))],
        out_specs=pl.BlockSpec((tq, MEM_W), lambda i: (i, 0)),
        out_shape=jax.ShapeDtypeStruct((m_tok, MEM_W), F32),
        compiler_params=_cparams(("arbitrary",)),
        name="mem_attn",
    )(proj, mem_k, mem_v)


def _mem_attn_step_kernel(q_ref, k_ref, v_ref, o_ref):
    rows = 16
    rid = lax.broadcasted_iota(jnp.int32, (rows, MEM_W), 0)
    lane = lax.broadcasted_iota(jnp.int32, (rows, MEM_W), 1)
    head_mask = (lane // DH_M) == rid
    q = jnp.broadcast_to(q_ref[0], (rows, MEM_W))
    qm = jnp.where(head_mask, q, 0.0)
    s = _dot_hp(qm, k_ref[0, 0]) * (DH_M ** -0.5)
    e = jnp.exp(s - jnp.max(s, axis=1, keepdims=True))
    p = e / jnp.sum(e, axis=1, keepdims=True)
    o = _dot_hp(p, v_ref[0, 0], nt=True)
    o_ref[0] = jnp.sum(jnp.where(head_mask, o, 0.0), axis=0, keepdims=True)


def _mem_attn_sample(mq3, cache_k, cache_v, li):
    batch = mq3.shape[0]
    return pl.pallas_call(
        _mem_attn_step_kernel,
        grid=(batch,),
        in_specs=[pl.BlockSpec((1, 1, MEM_W), lambda b: (b, 0, 0)),
                  pl.BlockSpec((1, 1, N_MEM, MEM_W), lambda b: (li, b, 0, 0)),
                  pl.BlockSpec((1, 1, N_MEM, MEM_W), lambda b: (li, b, 0, 0))],
        out_specs=pl.BlockSpec((1, 1, MEM_W), lambda b: (b, 0, 0)),
        out_shape=jax.ShapeDtypeStruct((batch, 1, MEM_W), F32),
        compiler_params=_cparams(("arbitrary",)),
        name="mem_attn_step",
    )(mq3, cache_k, cache_v)


def _routing(logits):
    lane = lax.broadcasted_iota(jnp.int32, logits.shape, 1)
    lane_f = lane.astype(F32)
    big = 1000.0
    is_g = (lane >= N_EXPERTS) & (lane < N_EXPERTS + 4)
    lg = jnp.where(is_g, logits, -jnp.inf)
    gmax = jnp.max(lg, axis=1, keepdims=True)
    gidx = jnp.min(jnp.where(lg == gmax, lane_f, big), axis=1, keepdims=True) - float(N_EXPERTS)
    p_g = 1.0 / jnp.sum(jnp.exp(lg - gmax), axis=1, keepdims=True)
    in_grp = (lane < N_EXPERTS) & ((lane >> 2).astype(F32) == gidx)
    le = jnp.where(in_grp, logits, -jnp.inf)
    v1 = jnp.max(le, axis=1, keepdims=True)
    i1 = jnp.min(jnp.where(le == v1, lane_f, big), axis=1, keepdims=True)
    le2 = jnp.where(lane_f == i1, -jnp.inf, le)
    v2 = jnp.max(le2, axis=1, keepdims=True)
    i2 = jnp.min(jnp.where(le2 == v2, lane_f, big), axis=1, keepdims=True)
    e2 = jnp.exp(v2 - v1)
    inv = 1.0 / (1.0 + e2)
    return jnp.where(lane_f == i1, inv * p_g, jnp.where(lane_f == i2, e2 * inv * p_g, 0.0))


def _mix_kernel(x_ref, tok_ref, mem_ref, wt_ref, wm_ref, g_ref, b_ref, wr_ref, br_ref,
                x1_ref, comb_ref, *, hp):
    dot = _dot_hp if hp else _bdot
    mix = dot(tok_ref[...], wt_ref[...]) + dot(mem_ref[...], wm_ref[...])
    x1 = _layer_norm(ALPHA * x_ref[...] + mix, g_ref[...], b_ref[...])
    x1_ref[...] = x1
    logits = dot(x1, wr_ref[...]) + br_ref[...]
    comb_ref[...] = _routing(logits)


def _mix(x, tok, mem, wt, wm, g, b, wr, br, tm, hp=False):
    m_tok = x.shape[0]
    const = lambda i: (0, 0)
    rows = lambda i: (i, 0)
    return pl.pallas_call(
        functools.partial(_mix_kernel, hp=hp),
        grid=(m_tok // tm,),
        in_specs=[pl.BlockSpec((tm, D_MODEL), rows),
                  pl.BlockSpec((tm, TOK_W), rows),
                  pl.BlockSpec((tm, MEM_W), rows),
                  pl.BlockSpec((TOK_W, D_MODEL), const),
                  pl.BlockSpec((MEM_W, D_MODEL), const),
                  pl.BlockSpec((1, D_MODEL), const),
                  pl.BlockSpec((1, D_MODEL), const),
                  pl.BlockSpec((D_MODEL, LANES), const),
                  pl.BlockSpec((1, LANES), const)],
        out_specs=[pl.BlockSpec((tm, D_MODEL), rows),
                   pl.BlockSpec((tm, LANES), rows)],
        out_shape=[jax.ShapeDtypeStruct((m_tok, D_MODEL), F32),
                   jax.ShapeDtypeStruct((m_tok, LANES), F32)],
        compiler_params=_cparams(("arbitrary",)),
        name="mix",
    )(x, tok, mem, wt, wm, g, b, wr, br)


def _moe_kernel(x_ref, comb_ref, w1_ref, w3_ref, w2_ref, g_ref, b_ref, o_ref, xb_ref, acc_ref,
                *, hp):
    e = pl.program_id(1)

    @pl.when(e == 0)
    def _():
        xb_ref[...] = x_ref[...].astype(BF16)
        acc_ref[...] = jnp.zeros_like(acc_ref)

    if hp:
        x = x_ref[...]
        a = _dot_hp(x, w1_ref[0, 0])
        b = _dot_hp(x, w3_ref[0, 0])
    else:
        xb = xb_ref[...]
        a = jnp.dot(xb, w1_ref[0, 0].astype(BF16), preferred_element_type=F32)
        b = jnp.dot(xb, w3_ref[0, 0].astype(BF16), preferred_element_type=F32)
    comb = comb_ref[...]
    lane = lax.broadcasted_iota(jnp.int32, comb.shape, 1)
    ce = jnp.sum(jnp.where(lane == e, comb, 0.0), axis=1, keepdims=True)
    hid = (a * _sigmoid(a)) * b * ce
    if hp:
        acc_ref[...] += _dot_hp(hid, w2_ref[0, 0])
    else:
        acc_ref[...] += jnp.dot(hid.astype(BF16), w2_ref[0, 0].astype(BF16),
                                preferred_element_type=F32)

    @pl.when(e == N_EXPERTS - 1)
    def _():
        o_ref[...] = _layer_norm(ALPHA * x_ref[...] + acc_ref[...], g_ref[...], b_ref[...])


def _moe(x1, comb, w1, w3, w2, g, b, li, tm, hp=False):
    m_tok = x1.shape[0]
    rows = lambda i, e: (i, 0)
    const = lambda i, e: (0, 0)
    return pl.pallas_call(
        functools.partial(_moe_kernel, hp=hp),
        grid=(m_tok // tm, N_EXPERTS),
        in_specs=[pl.BlockSpec((tm, D_MODEL), rows),
                  pl.BlockSpec((tm, LANES), rows),
                  pl.BlockSpec((1, 1, D_MODEL, D_EXPERT), lambda i, e: (li, e, 0, 0)),
                  pl.BlockSpec((1, 1, D_MODEL, D_EXPERT), lambda i, e: (li, e, 0, 0)),
                  pl.BlockSpec((1, 1, D_EXPERT, D_MODEL), lambda i, e: (li, e, 0, 0)),
                  pl.BlockSpec((1, D_MODEL), const),
                  pl.BlockSpec((1, D_MODEL), const)],
        out_specs=pl.BlockSpec((tm, D_MODEL), rows),
        out_shape=jax.ShapeDtypeStruct((m_tok, D_MODEL), F32),
        scratch_shapes=[pltpu.VMEM((tm, D_MODEL), BF16), pltpu.VMEM((tm, D_MODEL), F32)],
        compiler_params=_cparams(("arbitrary", "arbitrary")),
        name="moe",
    )(x1, comb, w1, w3, w2, g, b)


def _diff_lambda(lq_ref, lam_init):
    lq = lq_ref[0]
    a = jnp.sum(lq[0:1, :] * lq[1:2, :], axis=1, keepdims=True)
    b = jnp.sum(lq[2:3, :] * lq[3:4, :], axis=1, keepdims=True)
    return jnp.exp(a) - jnp.exp(b) + lam_init


def _sub_norm(o, g_row, lam_init):
    o = o * lax.rsqrt(jnp.mean(o * o, axis=-1, keepdims=True) + LN_EPS) * g_row
    return o * (1.0 - lam_init)


def _diff_attn_kernel(q1_ref, q2_ref, k1_ref, k2_ref, vt_ref, lq_ref, gs_ref, o_ref,
                      m_scr, l_scr, acc_scr, *, tq, lam_init):
    i = pl.program_id(2)
    lam = _diff_lambda(lq_ref, lam_init)
    qs = [(q1_ref[...] * (DQK ** -0.5)).astype(BF16), (q2_ref[...] * (DQK ** -0.5)).astype(BF16)]
    k_refs = [k1_ref, k2_ref]
    m_scr[...] = jnp.full_like(m_scr, -jnp.inf)
    l_scr[...] = jnp.zeros_like(l_scr)
    acc_scr[...] = jnp.zeros_like(acc_scr)
    key_i = lax.broadcasted_iota(jnp.int32, (tq, tq), 0)
    qry_i = lax.broadcasted_iota(jnp.int32, (tq, tq), 1)
    causal = key_i <= qry_i

    def block(j, masked):
        start = pl.multiple_of(j * tq, tq)
        sts = []
        for hh in range(2):
            for mm in range(2):
                kb = k_refs[mm][pl.ds(start, tq), hh * DQK:(hh + 1) * DQK]
                st = lax.dot_general(kb, qs[mm][:, hh * DQK:(hh + 1) * DQK],
                                     (((1,), (1,)), ((), ())), preferred_element_type=F32)
                if masked:
                    st = jnp.where(causal, st, -jnp.inf)
                sts.append(st)
        m_all = m_scr[...]
        l_all = l_scr[...]
        ps, alphas, m_rows, l_rows = [], [], [], []
        for idx in range(4):
            m_old = m_all[idx:idx + 1, :]
            m_new = jnp.maximum(m_old, jnp.max(sts[idx], axis=0, keepdims=True))
            alpha = jnp.exp(m_old - m_new)
            p = jnp.exp(sts[idx] - m_new)
            l_rows.append(alpha * l_all[idx:idx + 1, :] + jnp.sum(p, axis=0, keepdims=True))
            m_rows.append(m_new)
            alphas.append(alpha)
            ps.append(p.astype(BF16))
        m_scr[...] = jnp.concatenate(m_rows + [m_all[4:, :]], axis=0)
        l_scr[...] = jnp.concatenate(l_rows + [l_all[4:, :]], axis=0)
        for idx in range(4):
            vt = vt_ref[j, (idx // 2) * DH:(idx // 2 + 1) * DH, :]
            acc_scr[idx] = alphas[idx] * acc_scr[idx] + jnp.dot(vt, ps[idx],
                                                                preferred_element_type=F32)

    def body(j, carry):
        block(j, False)
        return carry

    lax.fori_loop(0, i, body, 0)
    block(i, True)

    for hh in range(2):
        o1 = acc_scr[hh * 2] / l_scr[hh * 2:hh * 2 + 1, :]
        o2 = acc_scr[hh * 2 + 1] / l_scr[hh * 2 + 1:hh * 2 + 2, :]
        o = o1 - lam * o2
        o = o * lax.rsqrt(jnp.mean(o * o, axis=0, keepdims=True) + LN_EPS)
        o_ref[:, hh * DH:(hh + 1) * DH] = o.T * gs_ref[...] * (1.0 - lam_init)


def _diff_attn_prompt(proj_b, k_b, vt_b, lam_qk, g_sub, j, lam_init, batch, seq, tq):
    m_tok = batch * seq
    nq = seq // tq
    kern = functools.partial(_diff_attn_kernel, tq=tq, lam_init=lam_init)
    return pl.pallas_call(
        kern,
        grid=(batch, 3, nq),
        in_specs=[pl.BlockSpec((tq, 2 * DQK), lambda b, hp, i: (b * nq + i, hp)),
                  pl.BlockSpec((tq, 2 * DQK), lambda b, hp, i: (b * nq + i, 3 + hp)),
                  pl.BlockSpec((seq, 2 * DQK), lambda b, hp, i: (b, hp)),
                  pl.BlockSpec((seq, 2 * DQK), lambda b, hp, i: (b, 3 + hp)),
                  pl.BlockSpec((nq, 2 * DH, tq), lambda b, hp, i: (b, hp, 0)),
                  pl.BlockSpec((1, 4, DQK), lambda b, hp, i: (j, 0, 0)),
                  pl.BlockSpec((1, DH), lambda b, hp, i: (0, 0))],
        out_specs=pl.BlockSpec((tq, 2 * DH), lambda b, hp, i: (b * nq + i, hp)),
        out_shape=jax.ShapeDtypeStruct((m_tok, TOK_W), F32),
        scratch_shapes=[pltpu.VMEM((8, tq), F32), pltpu.VMEM((8, tq), F32),
                        pltpu.VMEM((4, DH, tq), F32)],
        compiler_params=_cparams(("arbitrary", "arbitrary", "arbitrary")),
        name="diff_attn",
    )(proj_b, proj_b, k_b, k_b, vt_b, lam_qk, g_sub)


PAGES_PER_STEP = 8


def _diff_attn_step_kernel(pt_ref, q_ref, kn_ref, vn_ref, *rest, n_steps, lam_init):
    k_refs = rest[:PAGES_PER_STEP]
    v_refs = rest[PAGES_PER_STEP:2 * PAGES_PER_STEP]
    lq_ref, gs_ref, o_ref, qm_scr, m_scr, l_scr, acc_scr = rest[2 * PAGES_PER_STEP:]
    p_idx = pl.program_id(1)
    rows = 16
    rid = lax.broadcasted_iota(jnp.int32, (rows, TOK_W), 0)
    lane = lax.broadcasted_iota(jnp.int32, (rows, TOK_W), 1)

    @pl.when(p_idx == 0)
    def _():
        q = jnp.broadcast_to(q_ref[0] * (DQK ** -0.5), (rows, TOK_W))
        qm_scr[...] = jnp.where((lane // DQK) == rid, q, 0.0)
        m_scr[...] = jnp.full_like(m_scr, -jnp.inf)
        l_scr[...] = jnp.zeros_like(l_scr)
        acc_scr[...] = jnp.zeros_like(acc_scr)

    qm = qm_scr[...]
    q_hi, q_lo = _split2(qm)
    q2 = jnp.concatenate([q_hi, q_lo], axis=0)
    s_pages = []
    for k_ref in k_refs:
        k_hi, k_lo = _split2(k_ref[0])
        r1 = _bdot(q2, k_hi)
        s_pages.append((_bdot(q_hi, k_lo) + r1[rows:]) + r1[:rows])
    s = jnp.concatenate(s_pages, axis=1)
    m_old = m_scr[...]
    m_new = jnp.maximum(m_old, jnp.max(s, axis=1, keepdims=True))
    alpha = jnp.exp(m_old - m_new)
    p = jnp.exp(s - m_new)
    p_hi, p_lo = _split2(p)
    p2 = jnp.concatenate([p_hi, p_lo], axis=0)
    pv = []
    for h in range(N_HEAD):
        pv_h = None
        for g in range(PAGES_PER_STEP):
            cols = slice(g * PAGE, (g + 1) * PAGE)
            v_hi, v_lo = _split2(v_refs[g][0, h])
            r1 = _bdot(p2[:, cols], v_hi)
            term = (_bdot(p_hi[:, cols], v_lo) + r1[rows:]) + r1[:rows]
            pv_h = term if pv_h is None else pv_h + term
        pv.append(pv_h)
    l_scr[...] = alpha * l_scr[...] + jnp.sum(p, axis=1, keepdims=True)
    acc_scr[...] = alpha * acc_scr[...] + jnp.concatenate(pv, axis=1)
    m_scr[...] = m_new

    @pl.when(p_idx == n_steps - 1)
    def _():
        lam = _diff_lambda(lq_ref, lam_init)
        s_new = jnp.sum(qm * kn_ref[0], axis=1, keepdims=True)
        m_old2 = m_scr[...]
        m_fin = jnp.maximum(m_old2, s_new)
        a2 = jnp.exp(m_old2 - m_fin)
        p_new = jnp.exp(s_new - m_fin)
        l_fin = a2 * l_scr[...] + p_new
        acc = a2 * acc_scr[...] + p_new * vn_ref[0]
        r = acc / l_fin
        head_of_lane = lane // DH
        o1 = jnp.sum(jnp.where(head_of_lane == rid, r, 0.0), axis=0, keepdims=True)
        o2 = jnp.sum(jnp.where(head_of_lane == rid - N_HEAD, r, 0.0), axis=0, keepdims=True)
        o = o1 - lam * o2
        for h in range(N_HEAD):
            sl = slice(h * DH, (h + 1) * DH)
            o_ref[0, :, sl] = _sub_norm(o[:, sl], gs_ref[...], lam_init)


def _diff_attn_sample(page_table, q3, kn3, vn3, cache_k, cache_v, lam_qk, g_sub, j, lam_init):
    batch, n_pages = page_table.shape
    n_steps = n_pages // PAGES_PER_STEP
    kern = functools.partial(_diff_attn_step_kernel, n_steps=n_steps, lam_init=lam_init)
    row3 = lambda b, p, pt: (b, 0, 0)

    def k_spec(g):
        return pl.BlockSpec((1, TOK_W, PAGE), lambda b, p, pt: (pt[b, p * PAGES_PER_STEP + g], 0, 0))

    def v_spec(g):
        return pl.BlockSpec((1, N_HEAD, PAGE, DH),
                            lambda b, p, pt: (pt[b, p * PAGES_PER_STEP + g], 0, 0, 0))

    k_specs = [k_spec(g) for g in range(PAGES_PER_STEP)]
    v_specs = [v_spec(g) for g in range(PAGES_PER_STEP)]
    grid_spec = pltpu.PrefetchScalarGridSpec(
        num_scalar_prefetch=1,
        grid=(batch, n_steps),
        in_specs=[pl.BlockSpec((1, 1, TOK_W), row3),
                  pl.BlockSpec((1, 1, TOK_W), row3),
                  pl.BlockSpec((1, 1, TOK_W), row3)] + k_specs + v_specs + [
                  pl.BlockSpec((1, 4, DQK), lambda b, p, pt: (j, 0, 0)),
                  pl.BlockSpec((1, DH), lambda b, p, pt: (0, 0))],
        out_specs=pl.BlockSpec((1, 1, TOK_W), row3),
        scratch_shapes=[pltpu.VMEM((16, TOK_W), F32), pltpu.VMEM((16, 1), F32),
                        pltpu.VMEM((16, 1), F32), pltpu.VMEM((16, TOK_W), F32)],
    )
    return pl.pallas_call(
        kern,
        grid_spec=grid_spec,
        out_shape=jax.ShapeDtypeStruct((batch, 1, TOK_W), F32),
        compiler_params=_cparams(("arbitrary", "arbitrary")),
        name="diff_attn_step",
    )(page_table, q3, kn3, vn3, *([cache_k] * PAGES_PER_STEP), *([cache_v] * PAGES_PER_STEP),
      lam_qk, g_sub)


def _diff_lambda_init(li):
    return 0.8 - 0.6 * math.exp(-0.3 * li)


def _prep_weights(P):
    w = {}
    wa = P['w_in_a']
    gates = jnp.pad(wa[:, :, 4 * TOK_W:4 * TOK_W + 2 * N_HEAD], ((0, 0), (0, 0), (0, LANES - 2 * N_HEAD)))
    w['w_a32'] = jnp.concatenate([wa[:, :, :4 * TOK_W], wa[:, :, 4 * TOK_W + 2 * N_HEAD:], gates],
                                 axis=-1)
    w['w_a'] = w['w_a32'].astype(BF16)
    w['gate_bias'] = jnp.pad(jnp.concatenate([P['b_igate'], P['b_fgate']], axis=-1),
                             ((0, 0), (0, LANES - 2 * N_HEAD)))[:, None, :]
    w['w_b'] = P['w_in_b'].astype(BF16)
    w['w_kv'] = P['w_kv_shared'].astype(BF16)
    w['w_mem'] = jnp.transpose(P['w_mem_kv'], (1, 0, 2)).reshape(D_MODEL, DEPTH * 2 * MEM_W).astype(BF16)
    w['w_out_t32'] = P['w_out'][:, :TOK_W, :]
    w['w_out_m32'] = P['w_out'][:, TOK_W:, :]
    w['w_out_t'] = w['w_out_t32'].astype(BF16)
    w['w_out_m'] = w['w_out_m32'].astype(BF16)
    wr = jnp.concatenate([P['w_router'], P['w_group']], axis=-1)
    w['w_r32'] = jnp.pad(wr, ((0, 0), (0, 0), (0, LANES - wr.shape[-1])))
    w['w_r'] = w['w_r32'].astype(BF16)
    br = jnp.concatenate([P['b_router'], P['b_group']], axis=-1)
    w['b_r'] = jnp.pad(br, ((0, 0), (0, LANES - br.shape[-1])))[:, None, :]
    return w


def _layer_tail(x, tok, mem, li, P, W, tm_mix, tm_moe, hp=False):
    sfx = '32' if hp else ''
    x1, comb = _mix(x, tok, mem, W['w_out_t' + sfx][li], W['w_out_m' + sfx][li],
                    P['ln_mix_g'][li][None, :], P['ln_mix_b'][li][None, :],
                    W['w_r' + sfx][li], W['b_r'][li], tm_mix, hp)
    return _moe(x1, comb, P['w_e1'], P['w_e3'], P['w_e2'],
                P['ln_ffn_g'][li][None, :], P['ln_ffn_b'][li][None, :], li, tm_moe, hp)


def _prompt_trunk(x_prompt, mem_prompt, P, W):
    batch, seq, _ = x_prompt.shape
    m_tok = batch * seq
    x = x_prompt.reshape(m_tok, D_MODEL)
    mem = mem_prompt.reshape(batch * N_MEM, D_MODEL)
    kv_mem = _proj(mem, W['w_mem'], 512)
    kv_mem = jnp.transpose(kv_mem.reshape(batch, N_MEM, DEPTH, 2, MEM_W), (3, 2, 0, 1, 4))
    mem_k, mem_v = kv_mem[0], kv_mem[1]

    new_c, new_n, new_m = [], [], []
    k_t = v_h = k16 = vt16 = None
    for li in range(DEPTH):
        if li < N_A:
            proj = _proj(x, W['w_a'][li], 256)
            tok, c, n, m = _mlstm_prompt(proj, W['gate_bias'][li], batch, seq)
            new_c.append(c)
            new_n.append(n)
            new_m.append(m[:, 0, :N_HEAD])
            mq_block = (4 * TOK_W) // MEM_W
        else:
            if li == N_A:
                k_t, v_h, k16, vt16 = _proj_shared_kv(x, W['w_kv'], batch, seq, 256)
            j = li - N_A
            proj = _proj(x, W['w_b'][j], 512)
            tok = _diff_attn_prompt(proj, k16, vt16, P['lambda_qk'], P['subln_g'][j:j + 1], j,
                                    _diff_lambda_init(li), batch, seq, 256)
            mq_block = TOK_W // MEM_W
        mem_o = _mem_attn_prompt(proj, mq_block, mem_k[li], mem_v[li], seq, 512)
        x = _layer_tail(x, tok, mem_o, li, P, W, 512, 1024)
    y = x.reshape(batch, seq, D_MODEL)
    k_p = jnp.transpose(k_t.reshape(batch, 2, N_HEAD, DQK, seq), (0, 4, 1, 2, 3))
    v_p = jnp.transpose(v_h, (0, 2, 1, 3))
    shp = (DEPTH, batch, N_MEM, H_M, DH_M)
    return (y, mem_k.reshape(shp), mem_v.reshape(shp),
            jnp.stack(new_c), jnp.stack(new_n), jnp.stack(new_m), k_p, v_p)


def _sample_trunk(x_sample, cache_mem_k, cache_mem_v, state, cache_k, cache_v, page_table, P, W):
    batch = x_sample.shape[0]
    x = x_sample.reshape(batch, D_MODEL)
    c_all, n_all, m_all = state
    mem_k = jnp.transpose(cache_mem_k, (0, 1, 3, 4, 2)).reshape(DEPTH, batch, MEM_W, N_MEM)
    mem_v = jnp.transpose(cache_mem_v, (0, 1, 3, 4, 2)).reshape(DEPTH, batch, MEM_W, N_MEM)
    ck = jnp.transpose(cache_k, (0, 2, 3, 4, 1)).reshape(cache_k.shape[0], TOK_W, PAGE)
    cv = jnp.transpose(cache_v, (0, 2, 1, 3))

    new_c, new_n, new_m = [], [], []
    kv32 = None
    for li in range(DEPTH):
        if li < N_A:
            proj = _proj(x, W['w_a32'][li], batch, 384, hp=True)
            m_pad = jnp.pad(m_all[li], ((0, 0), (0, LANES - N_HEAD)))[:, None, :]
            tok3, c, n, m = _mlstm_sample(proj[:, None, :], W['gate_bias'][li], c_all, n_all, m_pad, li)
            new_c.append(c)
            new_n.append(n)
            new_m.append(m[:, 0, :N_HEAD])
            mq = proj[:, 4 * TOK_W:4 * TOK_W + MEM_W]
        else:
            if li == N_A:
                kv32 = _proj(x, P['w_kv_shared'], batch, 384, hp=True)
            j = li - N_A
            proj = _proj(x, P['w_in_b'][j], batch, 256, hp=True)
            tok3 = _diff_attn_sample(page_table, proj[:, None, :TOK_W], kv32[:, None, :TOK_W],
                                     kv32[:, None, TOK_W:], ck, cv, P['lambda_qk'],
                                     P['subln_g'][j:j + 1], j, _diff_lambda_init(li))
            mq = proj[:, TOK_W:]
        mem_o = _mem_attn_sample(mq[:, None, :], mem_k, mem_v, li)
        x = _layer_tail(x, tok3[:, 0, :], mem_o[:, 0, :], li, P, W, batch, batch, hp=True)
    y = x.reshape(batch, 1, D_MODEL)
    k_s = kv32[:, :TOK_W].reshape(batch, 1, 2, N_HEAD, DQK)
    v_s = kv32[:, TOK_W:].reshape(batch, 1, N_HEAD, DH)
    return y, jnp.stack(new_c), jnp.stack(new_n), jnp.stack(new_m), k_s, v_s


def kernel(x_prompt, x_sample, mem_prompt, cache_mem_k, cache_mem_v, state_mlstm_C, state_mlstm_n,
           state_mlstm_m, cache_k, cache_v, page_table, w_in_a, b_igate, b_fgate, w_in_b, lambda_qk,
           subln_g, w_kv_shared, w_mem_kv, w_out, ln_mix_g, ln_mix_b, ln_ffn_g, ln_ffn_b, w_group,
           b_group, w_router, b_router, w_e1, w_e3, w_e2):
    P = {'w_in_a': w_in_a, 'b_igate': b_igate, 'b_fgate': b_fgate, 'w_in_b': w_in_b,
         'lambda_qk': lambda_qk, 'subln_g': subln_g, 'w_kv_shared': w_kv_shared, 'w_mem_kv': w_mem_kv,
         'w_out': w_out, 'ln_mix_g': ln_mix_g, 'ln_mix_b': ln_mix_b, 'ln_ffn_g': ln_ffn_g,
         'ln_ffn_b': ln_ffn_b, 'w_group': w_group, 'b_group': b_group, 'w_router': w_router,
         'b_router': b_router, 'w_e1': w_e1, 'w_e3': w_e3, 'w_e2': w_e2}
    W = _prep_weights(P)
    y_p, mem_k_p, mem_v_p, c_p, n_p, m_p, k_p, v_p = _prompt_trunk(x_prompt, mem_prompt, P, W)
    y_s, c_s, n_s, m_s, k_s, v_s = _sample_trunk(
        x_sample, cache_mem_k, cache_mem_v, (state_mlstm_C, state_mlstm_n, state_mlstm_m),
        cache_k, cache_v, page_table, P, W)
    return (y_p, y_s, mem_k_p, mem_v_p, c_p, n_p, m_p, k_p, v_p, c_s, n_s, m_s, k_s, v_s)
```

```python
import functools
import math

import jax
import jax.numpy as jnp
from jax import lax
from jax.experimental import pallas as pl
from jax.experimental.pallas import tpu as pltpu

F32 = jnp.float32
BF16 = jnp.bfloat16

D_MODEL = 1024
DEPTH = 4
N_A = 2
TOK_W = 768
MEM_W = 256
N_HEAD = 6
DH = 128
DQK = 64
H_M = 4
DH_M = 64
N_MEM = 256
CHUNK = 128
N_EXPERTS = 16
D_EXPERT = 256
ALPHA = (2.0 * DEPTH) ** 0.25
LN_EPS = 1e-5
PAGE = 128
LANES = 128
GATE_COL = 3328
A_COLS = 3456
VMEM_LIMIT = 48 * 1024 * 1024


def _cparams(sem):
    return pltpu.CompilerParams(dimension_semantics=sem, vmem_limit_bytes=VMEM_LIMIT)


def _bdot(a, b):
    return jnp.dot(a.astype(BF16), b.astype(BF16), preferred_element_type=F32)


def _bdot_nt(a, b):
    return lax.dot_general(a.astype(BF16), b.astype(BF16), (((1,), (1,)), ((), ())),
                           preferred_element_type=F32)


def _split3(a):
    hi = a.astype(BF16)
    r = a - hi.astype(F32)
    mid = r.astype(BF16)
    lo = (r - mid.astype(F32)).astype(BF16)
    return hi, mid, lo


def _split2(a):
    hi = a.astype(BF16)
    return hi, (a - hi.astype(F32)).astype(BF16)


def _dot_hp(a, b, nt=False):
    m = a.shape[0]
    a_hi, a_mid, a_lo = _split3(a)
    b_hi, b_mid, b_lo = _split3(b)
    dot = _bdot_nt if nt else _bdot
    a3 = jnp.concatenate([a_hi, a_mid, a_lo], axis=0)
    r1 = dot(a3, b_hi)
    r2 = dot(a3[:2 * m], b_mid)
    r3 = dot(a_hi, b_lo)
    return ((r3 + r2[m:]) + (r1[2 * m:] + r2[:m]) + r1[m:2 * m]) + r1[:m]


def _dot_x3(a, b, nt=False):
    m = a.shape[0]
    a_hi, a_lo = _split2(a)
    b_hi, b_lo = _split2(b)
    dot = _bdot_nt if nt else _bdot
    r1 = dot(jnp.concatenate([a_hi, a_lo], axis=0), b_hi)
    return (dot(a_hi, b_lo) + r1[m:]) + r1[:m]


def _layer_norm(y, g, b):
    mu = jnp.mean(y, axis=-1, keepdims=True)
    d = y - mu
    var = jnp.mean(d * d, axis=-1, keepdims=True)
    return d * lax.rsqrt(var + LN_EPS) * g + b


def _sigmoid(x):
    return 1.0 / (1.0 + jnp.exp(-x))


def _log_sigmoid(x):
    return jnp.minimum(x, 0.0) - jnp.log1p(jnp.exp(-jnp.abs(x)))


def _proj_kernel(x_ref, w_ref, o_ref, *, hp):
    if hp:
        o_ref[...] = _dot_hp(x_ref[...], w_ref[...])
    else:
        o_ref[...] = jnp.dot(x_ref[...].astype(BF16), w_ref[...], preferred_element_type=F32)


def _proj(x, w, tm, tn=None, hp=False):
    m, k = x.shape
    n = w.shape[1]
    tn = n if tn is None else tn
    return pl.pallas_call(
        functools.partial(_proj_kernel, hp=hp),
        grid=(m // tm, n // tn),
        in_specs=[pl.BlockSpec((tm, k), lambda i, j: (i, 0)),
                  pl.BlockSpec((k, tn), lambda i, j: (0, j))],
        out_specs=pl.BlockSpec((tm, tn), lambda i, j: (i, j)),
        out_shape=jax.ShapeDtypeStruct((m, n), F32),
        compiler_params=_cparams(("arbitrary", "arbitrary")),
        name="proj_hp" if hp else "proj",
    )(x, w)


def _proj_kv_kernel(x_ref, w_ref, kt_ref, vh_ref, kb_ref, vt_ref):
    y = jnp.dot(x_ref[...].astype(BF16), w_ref[...], preferred_element_type=F32)
    k = y[:, :TOK_W]
    v = y[:, TOK_W:]
    kt_ref[0] = k.T
    for h in range(N_HEAD):
        vh_ref[0, h] = v[:, h * DH:(h + 1) * DH]
    kb_ref[...] = k.astype(BF16)
    vt_ref[0] = v.T.astype(BF16)


def _proj_shared_kv(x, w, batch, seq, tm):
    m, k = x.shape
    n = w.shape[1]
    per_b = seq // tm
    return pl.pallas_call(
        _proj_kv_kernel,
        grid=(m // tm,),
        in_specs=[pl.BlockSpec((tm, k), lambda i: (i, 0)),
                  pl.BlockSpec((k, n), lambda i: (0, 0))],
        out_specs=[pl.BlockSpec((1, TOK_W, tm), lambda i: (i // per_b, 0, i % per_b)),
                   pl.BlockSpec((1, N_HEAD, tm, DH), lambda i: (i // per_b, 0, i % per_b, 0)),
                   pl.BlockSpec((tm, TOK_W), lambda i: (i, 0)),
                   pl.BlockSpec((1, TOK_W, tm), lambda i: (i, 0, 0))],
        out_shape=[jax.ShapeDtypeStruct((batch, TOK_W, seq), F32),
                   jax.ShapeDtypeStruct((batch, N_HEAD, seq, DH), F32),
                   jax.ShapeDtypeStruct((m, TOK_W), BF16),
                   jax.ShapeDtypeStruct((m // tm, TOK_W, tm), BF16)],
        compiler_params=_cparams(("arbitrary",)),
        name="proj_kv",
    )(x, w)


def _mlstm_chunk_kernel(q_ref, k_ref, v_ref, o_ref, g_ref, bias_ref,
                        tok_ref, c_ref, n_ref, m_ref):
    @pl.when(pl.program_id(1) == 0)
    def _():
        c_ref[...] = jnp.zeros_like(c_ref)
        n_ref[...] = jnp.zeros_like(n_ref)
        m_ref[...] = jnp.zeros_like(m_ref)

    L = CHUNK
    lane = lax.broadcasted_iota(jnp.int32, (L, LANES), 1)
    row = lax.broadcasted_iota(jnp.int32, (L, L), 0)
    col = lax.broadcasted_iota(jnp.int32, (L, L), 1)
    causal = col <= row

    gb = g_ref[...] + bias_ref[...]
    lf = _log_sigmoid(gb)
    tril = causal.astype(BF16)
    hi = lf.astype(BF16)
    r1 = lf - hi.astype(F32)
    mid = r1.astype(BF16)
    lo = (r1 - mid.astype(F32)).astype(BF16)
    fcum = (jnp.dot(tril, hi, preferred_element_type=F32)
            + jnp.dot(tril, mid, preferred_element_type=F32)
            + jnp.dot(tril, lo, preferred_element_type=F32))
    z = jnp.where(lane < N_HEAD, gb, fcum)
    zt = z.T

    m_row = m_ref[0]
    lane1 = lax.broadcasted_iota(jnp.int32, (1, LANES), 1)
    m_new_row = m_row
    scale = DH ** -0.5
    for h in range(N_HEAD):
        f_col = fcum[:, N_HEAD + h:N_HEAD + h + 1]
        f_row = zt[N_HEAD + h:N_HEAD + h + 1, :]
        ig_row = zt[h:h + 1, :]
        ig_col = gb[:, h:h + 1]
        m_prev = m_row[:, h:h + 1]
        log_d = jnp.where(causal, f_col - f_row + ig_row, -jnp.inf)
        log_prev = f_col + m_prev
        m_t = jnp.maximum(log_prev, jnp.max(log_d, axis=1, keepdims=True))
        dm = jnp.exp(log_d - m_t)
        prev_scale = jnp.exp(log_prev - m_t)

        sl = slice(h * DH, (h + 1) * DH)
        qh = q_ref[:, sl]
        kh = k_ref[:, sl] * scale
        vh = v_ref[:, sl]
        qb = qh.astype(BF16)
        vb = vh.astype(BF16)
        c_h = c_ref[0, h]
        n_h = n_ref[0, h:h + 1, :]
        qk = _bdot_nt(qb, kh) * dm
        num = _bdot(qk, vb) + prev_scale * _bdot(qb, c_h)
        den = (jnp.sum(qk, axis=1, keepdims=True)
               + prev_scale * jnp.sum(qh * n_h, axis=1, keepdims=True))
        hh = num / jnp.maximum(jnp.abs(den), jnp.exp(-m_t))
        tok_ref[:, sl] = _sigmoid(o_ref[:, sl]) * hh

        m_last = m_t[L - 1:L, :]
        f_last = f_col[L - 1:L, :]
        w_src = jnp.exp(f_last + ig_col - f_col - m_last)
        c_scale = jnp.exp(f_last + m_prev - m_last)
        kw = kh * w_src
        c_ref[0, h] = c_scale * c_h + _bdot(kw.T, vb)
        n_ref[0, h:h + 1, :] = c_scale * n_h + jnp.sum(kw, axis=0, keepdims=True)
        m_new_row = jnp.where(lane1 == h, m_last, m_new_row)
    m_ref[0] = m_new_row


def _mlstm_prompt(proj, bias_row, batch, seq):
    nc = seq // CHUNK
    m_tok = batch * seq

    def col_block(j):
        return pl.BlockSpec((CHUNK, TOK_W), lambda b, c: (b * nc + c, j))

    return pl.pallas_call(
        _mlstm_chunk_kernel,
        grid=(batch, nc),
        in_specs=[col_block(0), col_block(1), col_block(2), col_block(3),
                  pl.BlockSpec((CHUNK, LANES), lambda b, c: (b * nc + c, GATE_COL // LANES)),
                  pl.BlockSpec((1, LANES), lambda b, c: (0, 0))],
        out_specs=[pl.BlockSpec((CHUNK, TOK_W), lambda b, c: (b * nc + c, 0)),
                   pl.BlockSpec((1, N_HEAD, DH, DH), lambda b, c: (b, 0, 0, 0)),
                   pl.BlockSpec((1, N_HEAD, DH), lambda b, c: (b, 0, 0)),
                   pl.BlockSpec((1, 1, LANES), lambda b, c: (b, 0, 0))],
        out_shape=[jax.ShapeDtypeStruct((m_tok, TOK_W), F32),
                   jax.ShapeDtypeStruct((batch, N_HEAD, DH, DH), F32),
                   jax.ShapeDtypeStruct((batch, N_HEAD, DH), F32),
                   jax.ShapeDtypeStruct((batch, 1, LANES), F32)],
        compiler_params=_cparams(("arbitrary", "arbitrary")),
        name="mlstm_chunk",
    )(proj, proj, proj, proj, proj, bias_row)


def _mlstm_step_kernel(q_ref, k_ref, v_ref, o_ref, g_ref, bias_ref, c_ref, n_ref, m_ref,
                       tok_ref, c_out, n_out, m_out):
    gb = g_ref[0] + bias_ref[...]
    m_row = m_ref[0]
    lane1 = lax.broadcasted_iota(jnp.int32, (1, LANES), 1)
    row = lax.broadcasted_iota(jnp.int32, (DH, DH), 0)
    col = lax.broadcasted_iota(jnp.int32, (DH, DH), 1)
    eye = row == col
    m_new_row = m_row
    scale = DH ** -0.5
    for h in range(N_HEAD):
        ig = gb[:, h:h + 1]
        lf = _log_sigmoid(gb[:, N_HEAD + h:N_HEAD + h + 1])
        m_prev = m_row[:, h:h + 1]
        log_prev = lf + m_prev
        m_t = jnp.maximum(log_prev, ig)
        dm = jnp.exp(ig - m_t)
        prev_scale = jnp.exp(log_prev - m_t)

        sl = slice(h * DH, (h + 1) * DH)
        q_row = q_ref[0][:, sl]
        k_row = k_ref[0][:, sl] * scale
        v_row = v_ref[0][:, sl]
        c_h = c_ref[0, 0, h]
        n_h = n_ref[0, 0, h:h + 1, :]
        qk = jnp.sum(q_row * k_row, axis=1, keepdims=True) * dm
        q_c = _dot_hp(jnp.broadcast_to(q_row, (16, DH)), c_h)[0:1, :]
        num = qk * v_row + prev_scale * q_c
        den = qk + prev_scale * jnp.sum(q_row * n_h, axis=1, keepdims=True)
        hh = num / jnp.maximum(jnp.abs(den), jnp.exp(-m_t))
        tok_ref[0, :, sl] = _sigmoid(o_ref[0][:, sl]) * hh

        k_col = jnp.sum(jnp.where(eye, jnp.broadcast_to(k_row, (DH, DH)), 0.0), axis=1, keepdims=True)
        c_out[0, h] = prev_scale * c_h + (dm * k_col) * v_row
        n_out[0, h:h + 1, :] = prev_scale * n_h + dm * k_row
        m_new_row = jnp.where(lane1 == h, m_t, m_new_row)
    m_out[0] = m_new_row


def _mlstm_sample(proj3, bias_row, c_all, n_all, m_pad, li):
    batch = proj3.shape[0]

    def col_block(j):
        return pl.BlockSpec((1, 1, TOK_W), lambda b: (b, 0, j))

    return pl.pallas_call(
        _mlstm_step_kernel,
        grid=(batch,),
        in_specs=[col_block(0), col_block(1), col_block(2), col_block(3),
                  pl.BlockSpec((1, 1, LANES), lambda b: (b, 0, GATE_COL // LANES)),
                  pl.BlockSpec((1, LANES), lambda b: (0, 0)),
                  pl.BlockSpec((1, 1, N_HEAD, DH, DH), lambda b: (li, b, 0, 0, 0)),
                  pl.BlockSpec((1, 1, N_HEAD, DH), lambda b: (li, b, 0, 0)),
                  pl.BlockSpec((1, 1, LANES), lambda b: (b, 0, 0))],
        out_specs=[pl.BlockSpec((1, 1, TOK_W), lambda b: (b, 0, 0)),
                   pl.BlockSpec((1, N_HEAD, DH, DH), lambda b: (b, 0, 0, 0)),
                   pl.BlockSpec((1, N_HEAD, DH), lambda b: (b, 0, 0)),
                   pl.BlockSpec((1, 1, LANES), lambda b: (b, 0, 0))],
        out_shape=[jax.ShapeDtypeStruct((batch, 1, TOK_W), F32),
                   jax.ShapeDtypeStruct((batch, N_HEAD, DH, DH), F32),
                   jax.ShapeDtypeStruct((batch, N_HEAD, DH), F32),
                   jax.ShapeDtypeStruct((batch, 1, LANES), F32)],
        compiler_params=_cparams(("arbitrary",)),
        name="mlstm_step",
    )(proj3, proj3, proj3, proj3, proj3, bias_row, c_all, n_all, m_pad)


def _mem_attn_kernel(q_ref, k_ref, v_ref, o_ref):
    outs = []
    for h in range(H_M):
        sl = slice(h * DH_M, (h + 1) * DH_M)
        s = _bdot_nt(q_ref[:, sl], k_ref[0][:, sl]) * (DH_M ** -0.5)
        e = jnp.exp(s - jnp.max(s, axis=1, keepdims=True))
        p = e / jnp.sum(e, axis=1, keepdims=True)
        outs.append(_bdot(p, v_ref[0][:, sl]))
    o_ref[...] = jnp.concatenate(outs, axis=-1)


def _mem_attn_prompt(proj, q_col_block, mem_k, mem_v, seq, tq):
    m_tok = proj.shape[0]
    per_b = seq // tq
    return pl.pallas_call(
        _mem_attn_kernel,
        grid=(m_tok // tq,),
        in_specs=[pl.BlockSpec((tq, MEM_W), lambda i: (i, q_col_block)),
                  pl.BlockSpec((1, N_MEM, MEM_W), lambda i: (i // per_b, 0, 0)),
                  pl.BlockSpec((1, N_MEM, MEM_W), lambda i: (i // per_b, 0, 0))],
        out_specs=pl.BlockSpec((tq, MEM_W), lambda i: (i, 0)),
        out_shape=jax.ShapeDtypeStruct((m_tok, MEM_W), F32),
        compiler_params=_cparams(("arbitrary",)),
        name="mem_attn",
    )(proj, mem_k, mem_v)


def _mem_attn_step_kernel(q_ref, k_ref, v_ref, o_ref):
    rows = 16
    rid = lax.broadcasted_iota(jnp.int32, (rows, MEM_W), 0)
    lane = lax.broadcasted_iota(jnp.int32, (rows, MEM_W), 1)
    head_mask = (lane // DH_M) == rid
    q = jnp.broadcast_to(q_ref[0], (rows, MEM_W))
    qm = jnp.where(head_mask, q, 0.0)
    s = _dot_hp(qm, k_ref[0, 0]) * (DH_M ** -0.5)
    e = jnp.exp(s - jnp.max(s, axis=1, keepdims=True))
    p = e / jnp.sum(e, axis=1, keepdims=True)
    o = _dot_hp(p, v_ref[0, 0], nt=True)
    o_ref[0] = jnp.sum(jnp.where(head_mask, o, 0.0), axis=0, keepdims=True)


def _mem_attn_sample(mq3, cache_k, cache_v, li):
    batch = mq3.shape[0]
    return pl.pallas_call(
        _mem_attn_step_kernel,
        grid=(batch,),
        in_specs=[pl.BlockSpec((1, 1, MEM_W), lambda b: (b, 0, 0)),
                  pl.BlockSpec((1, 1, N_MEM, MEM_W), lambda b: (li, b, 0, 0)),
                  pl.BlockSpec((1, 1, N_MEM, MEM_W), lambda b: (li, b, 0, 0))],
        out_specs=pl.BlockSpec((1, 1, MEM_W), lambda b: (b, 0, 0)),
        out_shape=jax.ShapeDtypeStruct((batch, 1, MEM_W), F32),
        compiler_params=_cparams(("arbitrary",)),
        name="mem_attn_step",
    )(mq3, cache_k, cache_v)


def _routing(logits):
    lane = lax.broadcasted_iota(jnp.int32, logits.shape, 1)
    lane_f = lane.astype(F32)
    big = 1000.0
    is_g = (lane >= N_EXPERTS) & (lane < N_EXPERTS + 4)
    lg = jnp.where(is_g, logits, -jnp.inf)
    gmax = jnp.max(lg, axis=1, keepdims=True)
    gidx = jnp.min(jnp.where(lg == gmax, lane_f, big), axis=1, keepdims=True) - float(N_EXPERTS)
    p_g = 1.0 / jnp.sum(jnp.exp(lg - gmax), axis=1, keepdims=True)
    in_grp = (lane < N_EXPERTS) & ((lane >> 2).astype(F32) == gidx)
    le = jnp.where(in_grp, logits, -jnp.inf)
    v1 = jnp.max(le, axis=1, keepdims=True)
    i1 = jnp.min(jnp.where(le == v1, lane_f, big), axis=1, keepdims=True)
    le2 = jnp.where(lane_f == i1, -jnp.inf, le)
    v2 = jnp.max(le2, axis=1, keepdims=True)
    i2 = jnp.min(jnp.where(le2 == v2, lane_f, big), axis=1, keepdims=True)
    e2 = jnp.exp(v2 - v1)
    inv = 1.0 / (1.0 + e2)
    comb = jnp.where(lane_f == i1, inv * p_g, jnp.where(lane_f == i2, e2 * inv * p_g, 0.0))
    return comb, gidx


GROUP_LANE = 16
RANK_LANE = 17


def _mix_kernel(x_ref, tok_ref, mem_ref, wt_ref, wm_ref, g_ref, b_ref, wr_ref, br_ref,
                x1_ref, comb_ref, cnt_ref, *, hp):
    @pl.when(pl.program_id(0) == 0)
    def _():
        cnt_ref[...] = jnp.zeros_like(cnt_ref)

    dot = _dot_hp if hp else _bdot
    mix = dot(tok_ref[...], wt_ref[...]) + dot(mem_ref[...], wm_ref[...])
    x1 = _layer_norm(ALPHA * x_ref[...] + mix, g_ref[...], b_ref[...])
    x1_ref[...] = x1
    logits = dot(x1, wr_ref[...]) + br_ref[...]
    comb, gidx = _routing(logits)
    tm = comb.shape[0]
    lane_f = lax.broadcasted_iota(jnp.int32, comb.shape, 1).astype(F32)
    mine = lane_f == gidx
    onehot = jnp.where(mine, 1.0, 0.0)
    row = lax.broadcasted_iota(jnp.int32, (tm, tm), 0)
    col = lax.broadcasted_iota(jnp.int32, (tm, tm), 1)
    before = jnp.where(col < row, 1.0, 0.0).astype(BF16)
    prefix = jnp.dot(before, onehot.astype(BF16), preferred_element_type=F32)
    cnt = cnt_ref[...]
    rank = jnp.sum(jnp.where(mine, prefix + cnt, 0.0), axis=1, keepdims=True)
    cnt_ref[...] = cnt + jnp.sum(onehot, axis=0, keepdims=True)
    comb_ref[...] = jnp.where(lane_f == float(GROUP_LANE), gidx,
                              jnp.where(lane_f == float(RANK_LANE), rank, comb))


def _mix(x, tok, mem, wt, wm, g, b, wr, br, tm, hp=False):
    m_tok = x.shape[0]
    const = lambda i: (0, 0)
    rows = lambda i: (i, 0)
    return pl.pallas_call(
        functools.partial(_mix_kernel, hp=hp),
        grid=(m_tok // tm,),
        in_specs=[pl.BlockSpec((tm, D_MODEL), rows),
                  pl.BlockSpec((tm, TOK_W), rows),
                  pl.BlockSpec((tm, MEM_W), rows),
                  pl.BlockSpec((TOK_W, D_MODEL), const),
                  pl.BlockSpec((MEM_W, D_MODEL), const),
                  pl.BlockSpec((1, D_MODEL), const),
                  pl.BlockSpec((1, D_MODEL), const),
                  pl.BlockSpec((D_MODEL, LANES), const),
                  pl.BlockSpec((1, LANES), const)],
        out_specs=[pl.BlockSpec((tm, D_MODEL), rows),
                   pl.BlockSpec((tm, LANES), rows),
                   pl.BlockSpec((1, LANES), const)],
        out_shape=[jax.ShapeDtypeStruct((m_tok, D_MODEL), F32),
                   jax.ShapeDtypeStruct((m_tok, LANES), F32),
                   jax.ShapeDtypeStruct((1, LANES), F32)],
        compiler_params=_cparams(("arbitrary",)),
        name="mix",
    )(x, tok, mem, wt, wm, g, b, wr, br)


def _moe_kernel(x_ref, comb_ref, w1_ref, w3_ref, w2_ref, g_ref, b_ref, o_ref, xb_ref, acc_ref,
                *, hp):
    e = pl.program_id(1)

    @pl.when(e == 0)
    def _():
        xb_ref[...] = x_ref[...].astype(BF16)
        acc_ref[...] = jnp.zeros_like(acc_ref)

    if hp:
        x = x_ref[...]
        a = _dot_hp(x, w1_ref[0, 0])
        b = _dot_hp(x, w3_ref[0, 0])
    else:
        xb = xb_ref[...]
        a = jnp.dot(xb, w1_ref[0, 0].astype(BF16), preferred_element_type=F32)
        b = jnp.dot(xb, w3_ref[0, 0].astype(BF16), preferred_element_type=F32)
    comb = comb_ref[...]
    lane = lax.broadcasted_iota(jnp.int32, comb.shape, 1)
    ce = jnp.sum(jnp.where(lane == e, comb, 0.0), axis=1, keepdims=True)
    hid = (a * _sigmoid(a)) * b * ce
    if hp:
        acc_ref[...] += _dot_hp(hid, w2_ref[0, 0])
    else:
        acc_ref[...] += jnp.dot(hid.astype(BF16), w2_ref[0, 0].astype(BF16),
                                preferred_element_type=F32)

    @pl.when(e == N_EXPERTS - 1)
    def _():
        o_ref[...] = _layer_norm(ALPHA * x_ref[...] + acc_ref[...], g_ref[...], b_ref[...])


def _moe(x1, comb, w1, w3, w2, g, b, li, tm, hp=False):
    m_tok = x1.shape[0]
    rows = lambda i, e: (i, 0)
    const = lambda i, e: (0, 0)
    return pl.pallas_call(
        functools.partial(_moe_kernel, hp=hp),
        grid=(m_tok // tm, N_EXPERTS),
        in_specs=[pl.BlockSpec((tm, D_MODEL), rows),
                  pl.BlockSpec((tm, LANES), rows),
                  pl.BlockSpec((1, 1, D_MODEL, D_EXPERT), lambda i, e: (li, e, 0, 0)),
                  pl.BlockSpec((1, 1, D_MODEL, D_EXPERT), lambda i, e: (li, e, 0, 0)),
                  pl.BlockSpec((1, 1, D_EXPERT, D_MODEL), lambda i, e: (li, e, 0, 0)),
                  pl.BlockSpec((1, D_MODEL), const),
                  pl.BlockSpec((1, D_MODEL), const)],
        out_specs=pl.BlockSpec((tm, D_MODEL), rows),
        out_shape=jax.ShapeDtypeStruct((m_tok, D_MODEL), F32),
        scratch_shapes=[pltpu.VMEM((tm, D_MODEL), BF16), pltpu.VMEM((tm, D_MODEL), F32)],
        compiler_params=_cparams(("arbitrary", "arbitrary")),
        name="moe",
    )(x1, comb, w1, w3, w2, g, b)


SLOT_TILE = 256
N_GROUPS = 4
EXP_PER_GROUP = 4
MOE_VMEM_LIMIT = 56 * 1024 * 1024


def _dispatch_kernel(dest_ref, x_ref, xs_in_ref, xs_ref, inv_ref, sem, *, tm, n_slots):
    del xs_in_ref
    i = pl.program_id(0)

    @pl.when(i == 0)
    def _():
        def init(s, carry):
            inv_ref[s] = -1
            return carry
        lax.fori_loop(0, n_slots, init, 0, unroll=16)

    base = i * tm

    def body(r, carry):
        d = dest_ref[base + r]
        pltpu.make_async_copy(x_ref.at[pl.ds(r, 1)], xs_ref.at[pl.ds(d, 1)], sem).start()
        inv_ref[d] = base + r
        return carry

    lax.fori_loop(0, tm, body, 0, unroll=8)
    pltpu.make_async_copy(x_ref, xs_ref.at[pl.ds(0, tm)], sem).wait()


def _dispatch(dest, x1, n_slots, tm):
    m_tok = x1.shape[0]
    xs_zero = jnp.zeros((n_slots, D_MODEL), F32)
    grid_spec = pltpu.PrefetchScalarGridSpec(
        num_scalar_prefetch=1,
        grid=(m_tok // tm,),
        in_specs=[pl.BlockSpec((tm, D_MODEL), lambda i, dest: (i, 0)),
                  pl.BlockSpec(memory_space=pl.ANY)],
        out_specs=[pl.BlockSpec(memory_space=pl.ANY),
                   pl.BlockSpec(memory_space=pltpu.SMEM)],
        scratch_shapes=[pltpu.SemaphoreType.DMA(())],
    )
    return pl.pallas_call(
        functools.partial(_dispatch_kernel, tm=tm, n_slots=n_slots),
        grid_spec=grid_spec,
        out_shape=[jax.ShapeDtypeStruct((n_slots, D_MODEL), F32),
                   jax.ShapeDtypeStruct((n_slots,), jnp.int32)],
        input_output_aliases={2: 0},
        compiler_params=_cparams(("arbitrary",)),
        name="moe_dispatch",
    )(dest, x1, xs_zero)


def _moe_sparse_kernel(blk_ref, grp_ref, val_ref, nrow_ref, inv_ref,
                       xs_ref, w1_ref, w3_ref, w2_ref, wr_ref, br_ref, g_ref, b_ref,
                       out_ref, w1b, w3b, w2b, ybuf, junk, state, sem, *, n_tiles):
    i = pl.program_id(0)
    ts = SLOT_TILE

    @pl.when(i == 0)
    def _():
        state[0] = -1
        state[1] = 0

    def wait_pending():
        @pl.when(state[1] == 1)
        def _():
            pltpu.make_async_copy(ybuf, junk, sem).wait()

    @pl.when(val_ref[i] == 1)
    def _():
        grp = grp_ref[i]

        @pl.when(grp != state[0])
        def _():
            w1b[...] = w1_ref[0, 0].astype(BF16)
            w3b[...] = w3_ref[0, 0].astype(BF16)
            w2b[...] = w2_ref[0, 0].astype(BF16)
            state[0] = grp

        x = xs_ref[...]
        xb = x.astype(BF16)
        logits = jnp.dot(xb, wr_ref[...], preferred_element_type=F32) + br_ref[...]
        comb, _ = _routing(logits)
        lane = lax.broadcasted_iota(jnp.int32, comb.shape, 1)
        acc = jnp.zeros((ts, D_MODEL), F32)
        for j in range(EXP_PER_GROUP):
            ce = jnp.sum(jnp.where(lane == grp * EXP_PER_GROUP + j, comb, 0.0), axis=1, keepdims=True)
            a = jnp.dot(xb, w1b[j], preferred_element_type=F32)
            b = jnp.dot(xb, w3b[j], preferred_element_type=F32)
            hid = (a * _sigmoid(a)) * b * ce
            acc = acc + jnp.dot(hid.astype(BF16), w2b[j], preferred_element_type=F32)
        x2 = _layer_norm(ALPHA * x + acc, g_ref[...], b_ref[...])

        wait_pending()
        ybuf[...] = x2
        n = nrow_ref[i]
        base = blk_ref[i] * ts

        def body(r, carry):
            t = inv_ref[base + r]
            pltpu.make_async_copy(ybuf.at[pl.ds(r, 1)], out_ref.at[pl.ds(t, 1)], sem).start()
            return carry

        def pad_body(r, carry):
            pltpu.make_async_copy(ybuf.at[pl.ds(r, 1)], junk.at[pl.ds(r, 1)], sem).start()
            return carry

        full = n // 8

        def body8(gi, carry):
            for u in range(8):
                body(gi * 8 + u, carry)
            return carry

        lax.fori_loop(0, full, body8, 0)
        lax.fori_loop(full * 8, n, body, 0)
        lax.fori_loop(n, ts, pad_body, 0)
        state[1] = 1

    @pl.when(i == n_tiles - 1)
    def _():
        wait_pending()
        state[1] = 0


def _moe_sparse(x1, comb, cnt, P, W, li):
    m_tok = x1.shape[0]
    ts = SLOT_TILE
    n_tiles = m_tok // ts + N_GROUPS
    n_slots = n_tiles * ts
    grp_tok = comb[:, GROUP_LANE].astype(jnp.int32)
    rank_tok = comb[:, RANK_LANE].astype(jnp.int32)
    counts = cnt[0, :N_GROUPS].astype(jnp.int32)
    tiles_g = (counts + ts - 1) // ts
    tile_end = jnp.cumsum(tiles_g)
    slot_base = (tile_end - tiles_g) * ts
    dest = jnp.take(slot_base, grp_tok) + rank_tok
    total = tile_end[-1]
    tile_i = jnp.arange(n_tiles, dtype=jnp.int32)
    valid = (tile_i < total).astype(jnp.int32)
    blk = jnp.minimum(tile_i, total - 1)
    grp_tile = jnp.minimum(jnp.searchsorted(tile_end, blk, side='right'), N_GROUPS - 1).astype(jnp.int32)
    n_rows = jnp.clip(jnp.take(slot_base + counts, grp_tile) - blk * ts, 0, ts) * valid

    xs, inv = _dispatch(dest, x1, n_slots, 512)

    w1 = P['w_e1'].reshape(DEPTH, N_GROUPS, EXP_PER_GROUP, D_MODEL, D_EXPERT)
    w3 = P['w_e3'].reshape(DEPTH, N_GROUPS, EXP_PER_GROUP, D_MODEL, D_EXPERT)
    w2 = P['w_e2'].reshape(DEPTH, N_GROUPS, EXP_PER_GROUP, D_EXPERT, D_MODEL)
    const = lambda i, *_: (0, 0)
    grid_spec = pltpu.PrefetchScalarGridSpec(
        num_scalar_prefetch=5,
        grid=(n_tiles,),
        in_specs=[pl.BlockSpec((ts, D_MODEL), lambda i, blk, grp, val, nrow, inv: (blk[i], 0)),
                  pl.BlockSpec((1, 1, EXP_PER_GROUP, D_MODEL, D_EXPERT),
                               lambda i, blk, grp, val, nrow, inv: (li, grp[i], 0, 0, 0)),
                  pl.BlockSpec((1, 1, EXP_PER_GROUP, D_MODEL, D_EXPERT),
                               lambda i, blk, grp, val, nrow, inv: (li, grp[i], 0, 0, 0)),
                  pl.BlockSpec((1, 1, EXP_PER_GROUP, D_EXPERT, D_MODEL),
                               lambda i, blk, grp, val, nrow, inv: (li, grp[i], 0, 0, 0)),
                  pl.BlockSpec((D_MODEL, LANES), const),
                  pl.BlockSpec((1, LANES), const),
                  pl.BlockSpec((1, D_MODEL), const),
                  pl.BlockSpec((1, D_MODEL), const)],
        out_specs=pl.BlockSpec(memory_space=pl.ANY),
        scratch_shapes=[pltpu.VMEM((EXP_PER_GROUP, D_MODEL, D_EXPERT), BF16),
                        pltpu.VMEM((EXP_PER_GROUP, D_MODEL, D_EXPERT), BF16),
                        pltpu.VMEM((EXP_PER_GROUP, D_EXPERT, D_MODEL), BF16),
                        pltpu.VMEM((ts, D_MODEL), F32),
                        pltpu.VMEM((ts, D_MODEL), F32),
                        pltpu.SMEM((2,), jnp.int32),
                        pltpu.SemaphoreType.DMA(())],
    )
    return pl.pallas_call(
        functools.partial(_moe_sparse_kernel, n_tiles=n_tiles),
        grid_spec=grid_spec,
        out_shape=jax.ShapeDtypeStruct((m_tok, D_MODEL), F32),
        compiler_params=pltpu.CompilerParams(dimension_semantics=("arbitrary",),
                                             vmem_limit_bytes=MOE_VMEM_LIMIT),
        name="moe_sparse",
    )(blk, grp_tile, valid, n_rows, inv, xs, w1, w3, w2, W['w_r'][li], W['b_r'][li],
      P['ln_ffn_g'][li][None, :], P['ln_ffn_b'][li][None, :])


def _diff_lambda(lq_ref, lam_init):
    lq = lq_ref[0]
    a = jnp.sum(lq[0:1, :] * lq[1:2, :], axis=1, keepdims=True)
    b = jnp.sum(lq[2:3, :] * lq[3:4, :], axis=1, keepdims=True)
    return jnp.exp(a) - jnp.exp(b) + lam_init


def _sub_norm(o, g_row, lam_init):
    o = o * lax.rsqrt(jnp.mean(o * o, axis=-1, keepdims=True) + LN_EPS) * g_row
    return o * (1.0 - lam_init)


def _diff_attn_kernel(q1_ref, q2_ref, k1_ref, k2_ref, vt_ref, lq_ref, gs_ref, o_ref,
                      m_scr, l_scr, acc_scr, *, tq, lam_init):
    i = pl.program_id(2)
    lam = _diff_lambda(lq_ref, lam_init)
    qs = [(q1_ref[...] * (DQK ** -0.5)).astype(BF16), (q2_ref[...] * (DQK ** -0.5)).astype(BF16)]
    k_refs = [k1_ref, k2_ref]
    m_scr[...] = jnp.full_like(m_scr, -jnp.inf)
    l_scr[...] = jnp.zeros_like(l_scr)
    acc_scr[...] = jnp.zeros_like(acc_scr)
    key_i = lax.broadcasted_iota(jnp.int32, (tq, tq), 0)
    qry_i = lax.broadcasted_iota(jnp.int32, (tq, tq), 1)
    causal = key_i <= qry_i

    def block(j, masked):
        start = pl.multiple_of(j * tq, tq)
        sts = []
        for hh in range(2):
            for mm in range(2):
                kb = k_refs[mm][pl.ds(start, tq), hh * DQK:(hh + 1) * DQK]
                st = lax.dot_general(kb, qs[mm][:, hh * DQK:(hh + 1) * DQK],
                                     (((1,), (1,)), ((), ())), preferred_element_type=F32)
                if masked:
                    st = jnp.where(causal, st, -jnp.inf)
                sts.append(st)
        m_all = m_scr[...]
        l_all = l_scr[...]
        ps, alphas, m_rows, l_rows = [], [], [], []
        for idx in range(4):
            m_old = m_all[idx:idx + 1, :]
            m_new = jnp.maximum(m_old, jnp.max(sts[idx], axis=0, keepdims=True))
            alpha = jnp.exp(m_old - m_new)
            p = jnp.exp(sts[idx] - m_new)
            l_rows.append(alpha * l_all[idx:idx + 1, :] + jnp.sum(p, axis=0, keepdims=True))
            m_rows.append(m_new)
            alphas.append(alpha)
            ps.append(p.astype(BF16))
        m_scr[...] = jnp.concatenate(m_rows + [m_all[4:, :]], axis=0)
        l_scr[...] = jnp.concatenate(l_rows + [l_all[4:, :]], axis=0)
        for idx in range(4):
            vt = vt_ref[j, (idx // 2) * DH:(idx // 2 + 1) * DH, :]
            acc_scr[idx] = alphas[idx] * acc_scr[idx] + jnp.dot(vt, ps[idx],
                                                                preferred_element_type=F32)

    def body(j, carry):
        block(j, False)
        return carry

    lax.fori_loop(0, i, body, 0)
    block(i, True)

    for hh in range(2):
        o1 = acc_scr[hh * 2] / l_scr[hh * 2:hh * 2 + 1, :]
        o2 = acc_scr[hh * 2 + 1] / l_scr[hh * 2 + 1:hh * 2 + 2, :]
        o = o1 - lam * o2
        o = o * lax.rsqrt(jnp.mean(o * o, axis=0, keepdims=True) + LN_EPS)
        o_ref[:, hh * DH:(hh + 1) * DH] = o.T * gs_ref[...] * (1.0 - lam_init)


def _diff_attn_prompt(proj_b, k_b, vt_b, lam_qk, g_sub, j, lam_init, batch, seq, tq):
    m_tok = batch * seq
    nq = seq // tq
    kern = functools.partial(_diff_attn_kernel, tq=tq, lam_init=lam_init)
    return pl.pallas_call(
        kern,
        grid=(batch, 3, nq),
        in_specs=[pl.BlockSpec((tq, 2 * DQK), lambda b, hp, i: (b * nq + i, hp)),
                  pl.BlockSpec((tq, 2 * DQK), lambda b, hp, i: (b * nq + i, 3 + hp)),
                  pl.BlockSpec((seq, 2 * DQK), lambda b, hp, i: (b, hp)),
                  pl.BlockSpec((seq, 2 * DQK), lambda b, hp, i: (b, 3 + hp)),
                  pl.BlockSpec((nq, 2 * DH, tq), lambda b, hp, i: (b, hp, 0)),
                  pl.BlockSpec((1, 4, DQK), lambda b, hp, i: (j, 0, 0)),
                  pl.BlockSpec((1, DH), lambda b, hp, i: (0, 0))],
        out_specs=pl.BlockSpec((tq, 2 * DH), lambda b, hp, i: (b * nq + i, hp)),
        out_shape=jax.ShapeDtypeStruct((m_tok, TOK_W), F32),
        scratch_shapes=[pltpu.VMEM((8, tq), F32), pltpu.VMEM((8, tq), F32),
                        pltpu.VMEM((4, DH, tq), F32)],
        compiler_params=_cparams(("arbitrary", "arbitrary", "arbitrary")),
        name="diff_attn",
    )(proj_b, proj_b, k_b, k_b, vt_b, lam_qk, g_sub)


PAGES_PER_STEP = 8


def _diff_attn_step_kernel(pt_ref, q_ref, kn_ref, vn_ref, *rest, n_steps, lam_init):
    k_refs = rest[:PAGES_PER_STEP]
    v_refs = rest[PAGES_PER_STEP:2 * PAGES_PER_STEP]
    lq_ref, gs_ref, o_ref, qm_scr, m_scr, l_scr, acc_scr = rest[2 * PAGES_PER_STEP:]
    p_idx = pl.program_id(1)
    rows = 16
    rid = lax.broadcasted_iota(jnp.int32, (rows, TOK_W), 0)
    lane = lax.broadcasted_iota(jnp.int32, (rows, TOK_W), 1)

    @pl.when(p_idx == 0)
    def _():
        q = jnp.broadcast_to(q_ref[0] * (DQK ** -0.5), (rows, TOK_W))
        qm_scr[...] = jnp.where((lane // DQK) == rid, q, 0.0)
        m_scr[...] = jnp.full_like(m_scr, -jnp.inf)
        l_scr[...] = jnp.zeros_like(l_scr)
        acc_scr[...] = jnp.zeros_like(acc_scr)

    qm = qm_scr[...]
    q_hi, q_lo = _split2(qm)
    q2 = jnp.concatenate([q_hi, q_lo], axis=0)
    s_pages = []
    for k_ref in k_refs:
        k_hi, k_lo = _split2(k_ref[0])
        r1 = _bdot(q2, k_hi)
        s_pages.append((_bdot(q_hi, k_lo) + r1[rows:]) + r1[:rows])
    s = jnp.concatenate(s_pages, axis=1)
    m_old = m_scr[...]
    m_new = jnp.maximum(m_old, jnp.max(s, axis=1, keepdims=True))
    alpha = jnp.exp(m_old - m_new)
    p = jnp.exp(s - m_new)
    p_hi, p_lo = _split2(p)
    p2 = jnp.concatenate([p_hi, p_lo], axis=0)
    pv = []
    for h in range(N_HEAD):
        pv_h = None
        for g in range(PAGES_PER_STEP):
            cols = slice(g * PAGE, (g + 1) * PAGE)
            v_hi, v_lo = _split2(v_refs[g][0, h])
            r1 = _bdot(p2[:, cols], v_hi)
            term = (_bdot(p_hi[:, cols], v_lo) + r1[rows:]) + r1[:rows]
            pv_h = term if pv_h is None else pv_h + term
        pv.append(pv_h)
    l_scr[...] = alpha * l_scr[...] + jnp.sum(p, axis=1, keepdims=True)
    acc_scr[...] = alpha * acc_scr[...] + jnp.concatenate(pv, axis=1)
    m_scr[...] = m_new

    @pl.when(p_idx == n_steps - 1)
    def _():
        lam = _diff_lambda(lq_ref, lam_init)
        s_new = jnp.sum(qm * kn_ref[0], axis=1, keepdims=True)
        m_old2 = m_scr[...]
        m_fin = jnp.maximum(m_old2, s_new)
        a2 = jnp.exp(m_old2 - m_fin)
        p_new = jnp.exp(s_new - m_fin)
        l_fin = a2 * l_scr[...] + p_new
        acc = a2 * acc_scr[...] + p_new * vn_ref[0]
        r = acc / l_fin
        head_of_lane = lane // DH
        o1 = jnp.sum(jnp.where(head_of_lane == rid, r, 0.0), axis=0, keepdims=True)
        o2 = jnp.sum(jnp.where(head_of_lane == rid - N_HEAD, r, 0.0), axis=0, keepdims=True)
        o = o1 - lam * o2
        for h in range(N_HEAD):
            sl = slice(h * DH, (h + 1) * DH)
            o_ref[0, :, sl] = _sub_norm(o[:, sl], gs_ref[...], lam_init)


def _diff_attn_sample(page_table, q3, kn3, vn3, cache_k, cache_v, lam_qk, g_sub, j, lam_init):
    batch, n_pages = page_table.shape
    n_steps = n_pages // PAGES_PER_STEP
    kern = functools.partial(_diff_attn_step_kernel, n_steps=n_steps, lam_init=lam_init)
    row3 = lambda b, p, pt: (b, 0, 0)

    def k_spec(g):
        return pl.BlockSpec((1, TOK_W, PAGE), lambda b, p, pt: (pt[b, p * PAGES_PER_STEP + g], 0, 0))

    def v_spec(g):
        return pl.BlockSpec((1, N_HEAD, PAGE, DH),
                            lambda b, p, pt: (pt[b, p * PAGES_PER_STEP + g], 0, 0, 0))

    k_specs = [k_spec(g) for g in range(PAGES_PER_STEP)]
    v_specs = [v_spec(g) for g in range(PAGES_PER_STEP)]
    grid_spec = pltpu.PrefetchScalarGridSpec(
        num_scalar_prefetch=1,
        grid=(batch, n_steps),
        in_specs=[pl.BlockSpec((1, 1, TOK_W), row3),
                  pl.BlockSpec((1, 1, TOK_W), row3),
                  pl.BlockSpec((1, 1, TOK_W), row3)] + k_specs + v_specs + [
                  pl.BlockSpec((1, 4, DQK), lambda b, p, pt: (j, 0, 0)),
                  pl.BlockSpec((1, DH), lambda b, p, pt: (0, 0))],
        out_specs=pl.BlockSpec((1, 1, TOK_W), row3),
        scratch_shapes=[pltpu.VMEM((16, TOK_W), F32), pltpu.VMEM((16, 1), F32),
                        pltpu.VMEM((16, 1), F32), pltpu.VMEM((16, TOK_W), F32)],
    )
    return pl.pallas_call(
        kern,
        grid_spec=grid_spec,
        out_shape=jax.ShapeDtypeStruct((batch, 1, TOK_W), F32),
        compiler_params=_cparams(("arbitrary", "arbitrary")),
        name="diff_attn_step",
    )(page_table, q3, kn3, vn3, *([cache_k] * PAGES_PER_STEP), *([cache_v] * PAGES_PER_STEP),
      lam_qk, g_sub)


def _diff_lambda_init(li):
    return 0.8 - 0.6 * math.exp(-0.3 * li)


def _prep_weights(P):
    w = {}
    wa = P['w_in_a']
    gates = jnp.pad(wa[:, :, 4 * TOK_W:4 * TOK_W + 2 * N_HEAD], ((0, 0), (0, 0), (0, LANES - 2 * N_HEAD)))
    w['w_a32'] = jnp.concatenate([wa[:, :, :4 * TOK_W], wa[:, :, 4 * TOK_W + 2 * N_HEAD:], gates],
                                 axis=-1)
    w['w_a'] = w['w_a32'].astype(BF16)
    w['gate_bias'] = jnp.pad(jnp.concatenate([P['b_igate'], P['b_fgate']], axis=-1),
                             ((0, 0), (0, LANES - 2 * N_HEAD)))[:, None, :]
    w['w_b'] = P['w_in_b'].astype(BF16)
    w['w_kv'] = P['w_kv_shared'].astype(BF16)
    w['w_mem'] = jnp.transpose(P['w_mem_kv'], (1, 0, 2)).reshape(D_MODEL, DEPTH * 2 * MEM_W).astype(BF16)
    w['w_out_t32'] = P['w_out'][:, :TOK_W, :]
    w['w_out_m32'] = P['w_out'][:, TOK_W:, :]
    w['w_out_t'] = w['w_out_t32'].astype(BF16)
    w['w_out_m'] = w['w_out_m32'].astype(BF16)
    wr = jnp.concatenate([P['w_router'], P['w_group']], axis=-1)
    w['w_r32'] = jnp.pad(wr, ((0, 0), (0, 0), (0, LANES - wr.shape[-1])))
    w['w_r'] = w['w_r32'].astype(BF16)
    br = jnp.concatenate([P['b_router'], P['b_group']], axis=-1)
    w['b_r'] = jnp.pad(br, ((0, 0), (0, LANES - br.shape[-1])))[:, None, :]
    return w


def _layer_tail(x, tok, mem, li, P, W, tm_mix, tm_moe, hp=False):
    sfx = '32' if hp else ''
    x1, comb, cnt = _mix(x, tok, mem, W['w_out_t' + sfx][li], W['w_out_m' + sfx][li],
                         P['ln_mix_g'][li][None, :], P['ln_mix_b'][li][None, :],
                         W['w_r' + sfx][li], W['b_r'][li], tm_mix, hp)
    if hp:
        return _moe(x1, comb, P['w_e1'], P['w_e3'], P['w_e2'],
                    P['ln_ffn_g'][li][None, :], P['ln_ffn_b'][li][None, :], li, tm_moe, hp)
    return _moe_sparse(x1, comb, cnt, P, W, li)


def _prompt_trunk(x_prompt, mem_prompt, P, W):
    batch, seq, _ = x_prompt.shape
    m_tok = batch * seq
    x = x_prompt.reshape(m_tok, D_MODEL)
    mem = mem_prompt.reshape(batch * N_MEM, D_MODEL)
    kv_mem = _proj(mem, W['w_mem'], 512)
    kv_mem = jnp.transpose(kv_mem.reshape(batch, N_MEM, DEPTH, 2, MEM_W), (3, 2, 0, 1, 4))
    mem_k, mem_v = kv_mem[0], kv_mem[1]

    new_c, new_n, new_m = [], [], []
    k_t = v_h = k16 = vt16 = None
    for li in range(DEPTH):
        if li < N_A:
            proj = _proj(x, W['w_a'][li], 256)
            tok, c, n, m = _mlstm_prompt(proj, W['gate_bias'][li], batch, seq)
            new_c.append(c)
            new_n.append(n)
            new_m.append(m[:, 0, :N_HEAD])
            mq_block = (4 * TOK_W) // MEM_W
        else:
            if li == N_A:
                k_t, v_h, k16, vt16 = _proj_shared_kv(x, W['w_kv'], batch, seq, 256)
            j = li - N_A
            proj = _proj(x, W['w_b'][j], 512)
            tok = _diff_attn_prompt(proj, k16, vt16, P['lambda_qk'], P['subln_g'][j:j + 1], j,
                                    _diff_lambda_init(li), batch, seq, 256)
            mq_block = TOK_W // MEM_W
        mem_o = _mem_attn_prompt(proj, mq_block, mem_k[li], mem_v[li], seq, 512)
        x = _layer_tail(x, tok, mem_o, li, P, W, 512, 1024)
    y = x.reshape(batch, seq, D_MODEL)
    k_p = jnp.transpose(k_t.reshape(batch, 2, N_HEAD, DQK, seq), (0, 4, 1, 2, 3))
    v_p = jnp.transpose(v_h, (0, 2, 1, 3))
    shp = (DEPTH, batch, N_MEM, H_M, DH_M)
    return (y, mem_k.reshape(shp), mem_v.reshape(shp),
            jnp.stack(new_c), jnp.stack(new_n), jnp.stack(new_m), k_p, v_p)


def _sample_trunk(x_sample, cache_mem_k, cache_mem_v, state, cache_k, cache_v, page_table, P, W):
    batch = x_sample.shape[0]
    x = x_sample.reshape(batch, D_MODEL)
    c_all, n_all, m_all = state
    mem_k = jnp.transpose(cache_mem_k, (0, 1, 3, 4, 2)).reshape(DEPTH, batch, MEM_W, N_MEM)
    mem_v = jnp.transpose(cache_mem_v, (0, 1, 3, 4, 2)).reshape(DEPTH, batch, MEM_W, N_MEM)
    ck = jnp.transpose(cache_k, (0, 2, 3, 4, 1)).reshape(cache_k.shape[0], TOK_W, PAGE)
    cv = jnp.transpose(cache_v, (0, 2, 1, 3))

    new_c, new_n, new_m = [], [], []
    kv32 = None
    for li in range(DEPTH):
        if li < N_A:
            proj = _proj(x, W['w_a32'][li], batch, 384, hp=True)
            m_pad = jnp.pad(m_all[li], ((0, 0), (0, LANES - N_HEAD)))[:, None, :]
            tok3, c, n, m = _mlstm_sample(proj[:, None, :], W['gate_bias'][li], c_all, n_all, m_pad, li)
            new_c.append(c)
            new_n.append(n)
            new_m.append(m[:, 0, :N_HEAD])
            mq = proj[:, 4 * TOK_W:4 * TOK_W + MEM_W]
        else:
            if li == N_A:
                kv32 = _proj(x, P['w_kv_shared'], batch, 384, hp=True)
            j = li - N_A
            proj = _proj(x, P['w_in_b'][j], batch, 256, hp=True)
            tok3 = _diff_attn_sample(page_table, proj[:, None, :TOK_W], kv32[:, None, :TOK_W],
                                     kv32[:, None, TOK_W:], ck, cv, P['lambda_qk'],
                                     P['subln_g'][j:j + 1], j, _diff_lambda_init(li))
            mq = proj[:, TOK_W:]
        mem_o = _mem_attn_sample(mq[:, None, :], mem_k, mem_v, li)
        x = _layer_tail(x, tok3[:, 0, :], mem_o[:, 0, :], li, P, W, batch, batch, hp=True)
    y = x.reshape(batch, 1, D_MODEL)
    k_s = kv32[:, :TOK_W].reshape(batch, 1, 2, N_HEAD, DQK)
    v_s = kv32[:, TOK_W:].reshape(batch, 1, N_HEAD, DH)
    return y, jnp.stack(new_c), jnp.stack(new_n), jnp.stack(new_m), k_s, v_s


def kernel(x_prompt, x_sample, mem_prompt, cache_mem_k, cache_mem_v, state_mlstm_C, state_mlstm_n,
           state_mlstm_m, cache_k, cache_v, page_table, w_in_a, b_igate, b_fgate, w_in_b, lambda_qk,
           subln_g, w_kv_shared, w_mem_kv, w_out, ln_mix_g, ln_mix_b, ln_ffn_g, ln_ffn_b, w_group,
           b_group, w_router, b_router, w_e1, w_e3, w_e2):
    P = {'w_in_a': w_in_a, 'b_igate': b_igate, 'b_fgate': b_fgate, 'w_in_b': w_in_b,
         'lambda_qk': lambda_qk, 'subln_g': subln_g, 'w_kv_shared': w_kv_shared, 'w_mem_kv': w_mem_kv,
         'w_out': w_out, 'ln_mix_g': ln_mix_g, 'ln_mix_b': ln_mix_b, 'ln_ffn_g': ln_ffn_g,
         'ln_ffn_b': ln_ffn_b, 'w_group': w_group, 'b_group': b_group, 'w_router': w_router,
         'b_router': b_router, 'w_e1': w_e1, 'w_e3': w_e3, 'w_e2': w_e2}
    W = _prep_weights(P)
    y_p, mem_k_p, mem_v_p, c_p, n_p, m_p, k_p, v_p = _prompt_trunk(x_prompt, mem_prompt, P, W)
    y_s, c_s, n_s, m_s, k_s, v_s = _sample_trunk(
        x_sample, cache_mem_k, cache_mem_v, (state_mlstm_C, state_mlstm_n, state_mlstm_m),
        cache_k, cache_v, page_table, P, W)
    return (y_p, y_s, mem_k_p, mem_v_p, c_p, n_p, m_p, k_p, v_p, c_s, n_s, m_s, k_s, v_s)
```

```python
import functools
import math

import jax
import jax.numpy as jnp
from jax import lax
from jax.experimental import pallas as pl
from jax.experimental.pallas import tpu as pltpu

F32 = jnp.float32
BF16 = jnp.bfloat16

D_MODEL = 1024
DEPTH = 4
N_A = 2
TOK_W = 768
MEM_W = 256
N_HEAD = 6
DH = 128
DQK = 64
H_M = 4
DH_M = 64
N_MEM = 256
CHUNK = 128
N_EXPERTS = 16
D_EXPERT = 256
ALPHA = (2.0 * DEPTH) ** 0.25
LN_EPS = 1e-5
PAGE = 128
LANES = 128
GATE_COL = 3328
A_COLS = 3456
VMEM_LIMIT = 48 * 1024 * 1024


def _cparams(sem):
    return pltpu.CompilerParams(dimension_semantics=sem, vmem_limit_bytes=VMEM_LIMIT)


def _bdot(a, b):
    return jnp.dot(a.astype(BF16), b.astype(BF16), preferred_element_type=F32)


def _bdot_nt(a, b):
    return lax.dot_general(a.astype(BF16), b.astype(BF16), (((1,), (1,)), ((), ())),
                           preferred_element_type=F32)


def _split3(a):
    hi = a.astype(BF16)
    r = a - hi.astype(F32)
    mid = r.astype(BF16)
    lo = (r - mid.astype(F32)).astype(BF16)
    return hi, mid, lo


def _split2(a):
    hi = a.astype(BF16)
    return hi, (a - hi.astype(F32)).astype(BF16)


def _dot_hp(a, b, nt=False):
    m = a.shape[0]
    a_hi, a_mid, a_lo = _split3(a)
    b_hi, b_mid, b_lo = _split3(b)
    dot = _bdot_nt if nt else _bdot
    a3 = jnp.concatenate([a_hi, a_mid, a_lo], axis=0)
    r1 = dot(a3, b_hi)
    r2 = dot(a3[:2 * m], b_mid)
    r3 = dot(a_hi, b_lo)
    return ((r3 + r2[m:]) + (r1[2 * m:] + r2[:m]) + r1[m:2 * m]) + r1[:m]


def _dot_x3(a, b, nt=False):
    m = a.shape[0]
    a_hi, a_lo = _split2(a)
    b_hi, b_lo = _split2(b)
    dot = _bdot_nt if nt else _bdot
    r1 = dot(jnp.concatenate([a_hi, a_lo], axis=0), b_hi)
    return (dot(a_hi, b_lo) + r1[m:]) + r1[:m]


def _layer_norm(y, g, b):
    mu = jnp.mean(y, axis=-1, keepdims=True)
    d = y - mu
    var = jnp.mean(d * d, axis=-1, keepdims=True)
    return d * lax.rsqrt(var + LN_EPS) * g + b


def _sigmoid(x):
    return 1.0 / (1.0 + jnp.exp(-x))


def _log_sigmoid(x):
    return jnp.minimum(x, 0.0) - jnp.log1p(jnp.exp(-jnp.abs(x)))


def _proj_kernel(x_ref, w_ref, o_ref, *, hp):
    if hp:
        o_ref[...] = _dot_hp(x_ref[...], w_ref[...])
    else:
        o_ref[...] = jnp.dot(x_ref[...].astype(BF16), w_ref[...], preferred_element_type=F32)


def _proj(x, w, tm, tn=None, hp=False):
    m, k = x.shape
    n = w.shape[1]
    tn = n if tn is None else tn
    return pl.pallas_call(
        functools.partial(_proj_kernel, hp=hp),
        grid=(m // tm, n // tn),
        in_specs=[pl.BlockSpec((tm, k), lambda i, j: (i, 0)),
                  pl.BlockSpec((k, tn), lambda i, j: (0, j))],
        out_specs=pl.BlockSpec((tm, tn), lambda i, j: (i, j)),
        out_shape=jax.ShapeDtypeStruct((m, n), F32),
        compiler_params=_cparams(("arbitrary", "arbitrary")),
        name="proj_hp" if hp else "proj",
    )(x, w)


def _proj_kv_kernel(x_ref, w_ref, kt_ref, vh_ref, kb_ref, vt_ref):
    y = jnp.dot(x_ref[...].astype(BF16), w_ref[...], preferred_element_type=F32)
    k = y[:, :TOK_W]
    v = y[:, TOK_W:]
    kt_ref[0] = k.T
    for h in range(N_HEAD):
        vh_ref[0, h] = v[:, h * DH:(h + 1) * DH]
    kb_ref[...] = k.astype(BF16)
    vt_ref[0] = v.T.astype(BF16)


def _proj_shared_kv(x, w, batch, seq, tm):
    m, k = x.shape
    n = w.shape[1]
    per_b = seq // tm
    return pl.pallas_call(
        _proj_kv_kernel,
        grid=(m // tm,),
        in_specs=[pl.BlockSpec((tm, k), lambda i: (i, 0)),
                  pl.BlockSpec((k, n), lambda i: (0, 0))],
        out_specs=[pl.BlockSpec((1, TOK_W, tm), lambda i: (i // per_b, 0, i % per_b)),
                   pl.BlockSpec((1, N_HEAD, tm, DH), lambda i: (i // per_b, 0, i % per_b, 0)),
                   pl.BlockSpec((tm, TOK_W), lambda i: (i, 0)),
                   pl.BlockSpec((1, TOK_W, tm), lambda i: (i, 0, 0))],
        out_shape=[jax.ShapeDtypeStruct((batch, TOK_W, seq), F32),
                   jax.ShapeDtypeStruct((batch, N_HEAD, seq, DH), F32),
                   jax.ShapeDtypeStruct((m, TOK_W), BF16),
                   jax.ShapeDtypeStruct((m // tm, TOK_W, tm), BF16)],
        compiler_params=_cparams(("arbitrary",)),
        name="proj_kv",
    )(x, w)


def _mlstm_chunk_kernel(q_ref, k_ref, v_ref, o_ref, g_ref, bias_ref,
                        tok_ref, c_ref, n_ref, m_ref):
    @pl.when(pl.program_id(1) == 0)
    def _():
        c_ref[...] = jnp.zeros_like(c_ref)
        n_ref[...] = jnp.zeros_like(n_ref)
        m_ref[...] = jnp.zeros_like(m_ref)

    L = CHUNK
    lane = lax.broadcasted_iota(jnp.int32, (L, LANES), 1)
    row = lax.broadcasted_iota(jnp.int32, (L, L), 0)
    col = lax.broadcasted_iota(jnp.int32, (L, L), 1)
    causal = col <= row

    gb = g_ref[...] + bias_ref[...]
    lf = _log_sigmoid(gb)
    tril = causal.astype(BF16)
    hi = lf.astype(BF16)
    r1 = lf - hi.astype(F32)
    mid = r1.astype(BF16)
    lo = (r1 - mid.astype(F32)).astype(BF16)
    fcum = (jnp.dot(tril, hi, preferred_element_type=F32)
            + jnp.dot(tril, mid, preferred_element_type=F32)
            + jnp.dot(tril, lo, preferred_element_type=F32))
    z = jnp.where(lane < N_HEAD, gb, fcum)
    zt = z.T

    m_row = m_ref[0]
    lane1 = lax.broadcasted_iota(jnp.int32, (1, LANES), 1)
    m_new_row = m_row
    scale = DH ** -0.5
    for h in range(N_HEAD):
        f_col = fcum[:, N_HEAD + h:N_HEAD + h + 1]
        f_row = zt[N_HEAD + h:N_HEAD + h + 1, :]
        ig_row = zt[h:h + 1, :]
        ig_col = gb[:, h:h + 1]
        m_prev = m_row[:, h:h + 1]
        log_d = jnp.where(causal, f_col - f_row + ig_row, -jnp.inf)
        log_prev = f_col + m_prev
        m_t = jnp.maximum(log_prev, jnp.max(log_d, axis=1, keepdims=True))
        dm = jnp.exp(log_d - m_t)
        prev_scale = jnp.exp(log_prev - m_t)

        sl = slice(h * DH, (h + 1) * DH)
        qh = q_ref[:, sl]
        kh = k_ref[:, sl] * scale
        vh = v_ref[:, sl]
        qb = qh.astype(BF16)
        vb = vh.astype(BF16)
        c_h = c_ref[0, h]
        n_h = n_ref[0, h:h + 1, :]
        qk = _bdot_nt(qb, kh) * dm
        num = _bdot(qk, vb) + prev_scale * _bdot(qb, c_h)
        den = (jnp.sum(qk, axis=1, keepdims=True)
               + prev_scale * jnp.sum(qh * n_h, axis=1, keepdims=True))
        hh = num / jnp.maximum(jnp.abs(den), jnp.exp(-m_t))
        tok_ref[:, sl] = _sigmoid(o_ref[:, sl]) * hh

        m_last = m_t[L - 1:L, :]
        f_last = f_col[L - 1:L, :]
        w_src = jnp.exp(f_last + ig_col - f_col - m_last)
        c_scale = jnp.exp(f_last + m_prev - m_last)
        kw = kh * w_src
        c_ref[0, h] = c_scale * c_h + _bdot(kw.T, vb)
        n_ref[0, h:h + 1, :] = c_scale * n_h + jnp.sum(kw, axis=0, keepdims=True)
        m_new_row = jnp.where(lane1 == h, m_last, m_new_row)
    m_ref[0] = m_new_row


def _mlstm_prompt(proj, bias_row, batch, seq):
    nc = seq // CHUNK
    m_tok = batch * seq

    def col_block(j):
        return pl.BlockSpec((CHUNK, TOK_W), lambda b, c: (b * nc + c, j))

    return pl.pallas_call(
        _mlstm_chunk_kernel,
        grid=(batch, nc),
        in_specs=[col_block(0), col_block(1), col_block(2), col_block(3),
                  pl.BlockSpec((CHUNK, LANES), lambda b, c: (b * nc + c, GATE_COL // LANES)),
                  pl.BlockSpec((1, LANES), lambda b, c: (0, 0))],
        out_specs=[pl.BlockSpec((CHUNK, TOK_W), lambda b, c: (b * nc + c, 0)),
                   pl.BlockSpec((1, N_HEAD, DH, DH), lambda b, c: (b, 0, 0, 0)),
                   pl.BlockSpec((1, N_HEAD, DH), lambda b, c: (b, 0, 0)),
                   pl.BlockSpec((1, 1, LANES), lambda b, c: (b, 0, 0))],
        out_shape=[jax.ShapeDtypeStruct((m_tok, TOK_W), F32),
                   jax.ShapeDtypeStruct((batch, N_HEAD, DH, DH), F32),
                   jax.ShapeDtypeStruct((batch, N_HEAD, DH), F32),
                   jax.ShapeDtypeStruct((batch, 1, LANES), F32)],
        compiler_params=_cparams(("arbitrary", "arbitrary")),
        name="mlstm_chunk",
    )(proj, proj, proj, proj, proj, bias_row)


def _mlstm_step_kernel(q_ref, k_ref, v_ref, o_ref, g_ref, bias_ref, c_ref, n_ref, m_ref,
                       tok_ref, c_out, n_out, m_out):
    gb = g_ref[0] + bias_ref[...]
    m_row = m_ref[0]
    lane1 = lax.broadcasted_iota(jnp.int32, (1, LANES), 1)
    row = lax.broadcasted_iota(jnp.int32, (DH, DH), 0)
    col = lax.broadcasted_iota(jnp.int32, (DH, DH), 1)
    eye = row == col
    m_new_row = m_row
    scale = DH ** -0.5
    for h in range(N_HEAD):
        ig = gb[:, h:h + 1]
        lf = _log_sigmoid(gb[:, N_HEAD + h:N_HEAD + h + 1])
        m_prev = m_row[:, h:h + 1]
        log_prev = lf + m_prev
        m_t = jnp.maximum(log_prev, ig)
        dm = jnp.exp(ig - m_t)
        prev_scale = jnp.exp(log_prev - m_t)

        sl = slice(h * DH, (h + 1) * DH)
        q_row = q_ref[0][:, sl]
        k_row = k_ref[0][:, sl] * scale
        v_row = v_ref[0][:, sl]
        c_h = c_ref[0, 0, h]
        n_h = n_ref[0, 0, h:h + 1, :]
        qk = jnp.sum(q_row * k_row, axis=1, keepdims=True) * dm
        q_c = _dot_hp(jnp.broadcast_to(q_row, (16, DH)), c_h)[0:1, :]
        num = qk * v_row + prev_scale * q_c
        den = qk + prev_scale * jnp.sum(q_row * n_h, axis=1, keepdims=True)
        hh = num / jnp.maximum(jnp.abs(den), jnp.exp(-m_t))
        tok_ref[0, :, sl] = _sigmoid(o_ref[0][:, sl]) * hh

        k_col = jnp.sum(jnp.where(eye, jnp.broadcast_to(k_row, (DH, DH)), 0.0), axis=1, keepdims=True)
        c_out[0, h] = prev_scale * c_h + (dm * k_col) * v_row
        n_out[0, h:h + 1, :] = prev_scale * n_h + dm * k_row
        m_new_row = jnp.where(lane1 == h, m_t, m_new_row)
    m_out[0] = m_new_row


def _mlstm_sample(proj3, bias_row, c_all, n_all, m_pad, li):
    batch = proj3.shape[0]

    def col_block(j):
        return pl.BlockSpec((1, 1, TOK_W), lambda b: (b, 0, j))

    return pl.pallas_call(
        _mlstm_step_kernel,
        grid=(batch,),
        in_specs=[col_block(0), col_block(1), col_block(2), col_block(3),
                  pl.BlockSpec((1, 1, LANES), lambda b: (b, 0, GATE_COL // LANES)),
                  pl.BlockSpec((1, LANES), lambda b: (0, 0)),
                  pl.BlockSpec((1, 1, N_HEAD, DH, DH), lambda b: (li, b, 0, 0, 0)),
                  pl.BlockSpec((1, 1, N_HEAD, DH), lambda b: (li, b, 0, 0)),
                  pl.BlockSpec((1, 1, LANES), lambda b: (b, 0, 0))],
        out_specs=[pl.BlockSpec((1, 1, TOK_W), lambda b: (b, 0, 0)),
                   pl.BlockSpec((1, N_HEAD, DH, DH), lambda b: (b, 0, 0, 0)),
                   pl.BlockSpec((1, N_HEAD, DH), lambda b: (b, 0, 0)),
                   pl.BlockSpec((1, 1, LANES), lambda b: (b, 0, 0))],
        out_shape=[jax.ShapeDtypeStruct((batch, 1, TOK_W), F32),
                   jax.ShapeDtypeStruct((batch, N_HEAD, DH, DH), F32),
                   jax.ShapeDtypeStruct((batch, N_HEAD, DH), F32),
                   jax.ShapeDtypeStruct((batch, 1, LANES), F32)],
        compiler_params=_cparams(("arbitrary",)),
        name="mlstm_step",
    )(proj3, proj3, proj3, proj3, proj3, bias_row, c_all, n_all, m_pad)


def _mem_attn_kernel(q_ref, k_ref, v_ref, o_ref):
    outs = []
    for h in range(H_M):
        sl = slice(h * DH_M, (h + 1) * DH_M)
        s = _bdot_nt(q_ref[:, sl], k_ref[0][:, sl]) * (DH_M ** -0.5)
        e = jnp.exp(s - jnp.max(s, axis=1, keepdims=True))
        p = e / jnp.sum(e, axis=1, keepdims=True)
        outs.append(_bdot(p, v_ref[0][:, sl]))
    o_ref[...] = jnp.concatenate(outs, axis=-1)


def _mem_attn_prompt(proj, q_col_block, mem_k, mem_v, seq, tq):
    m_tok = proj.shape[0]
    per_b = seq // tq
    return pl.pallas_call(
        _mem_attn_kernel,
        grid=(m_tok // tq,),
        in_specs=[pl.BlockSpec((tq, MEM_W), lambda i: (i, q_col_block)),
                  pl.BlockSpec((1, N_MEM, MEM_W), lambda i: (i // per_b, 0, 0)),
                  pl.BlockSpec((1, N_MEM, MEM_W), lambda i: (i // per_b, 0, 0))],
        out_specs=pl.BlockSpec((tq, MEM_W), lambda i: (i, 0)),
        out_shape=jax.ShapeDtypeStruct((m_tok, MEM_W), F32),
        compiler_params=_cparams(("arbitrary",)),
        name="mem_attn",
    )(proj, mem_k, mem_v)


def _mem_attn_step_kernel(q_ref, k_ref, v_ref, o_ref):
    rows = 16
    rid = lax.broadcasted_iota(jnp.int32, (rows, MEM_W), 0)
    lane = lax.broadcasted_iota(jnp.int32, (rows, MEM_W), 1)
    head_mask = (lane // DH_M) == rid
    q = jnp.broadcast_to(q_ref[0], (rows, MEM_W))
    qm = jnp.where(head_mask, q, 0.0)
    s = _dot_hp(qm, k_ref[0, 0]) * (DH_M ** -0.5)
    e = jnp.exp(s - jnp.max(s, axis=1, keepdims=True))
    p = e / jnp.sum(e, axis=1, keepdims=True)
    o = _dot_hp(p, v_ref[0, 0], nt=True)
    o_ref[0] = jnp.sum(jnp.where(head_mask, o, 0.0), axis=0, keepdims=True)


def _mem_attn_sample(mq3, cache_k, cache_v, li):
    batch = mq3.shape[0]
    return pl.pallas_call(
        _mem_attn_step_kernel,
        grid=(batch,),
        in_specs=[pl.BlockSpec((1, 1, MEM_W), lambda b: (b, 0, 0)),
                  pl.BlockSpec((1, 1, N_MEM, MEM_W), lambda b: (li, b, 0, 0)),
                  pl.BlockSpec((1, 1, N_MEM, MEM_W), lambda b: (li, b, 0, 0))],
        out_specs=pl.BlockSpec((1, 1, MEM_W), lambda b: (b, 0, 0)),
        out_shape=jax.ShapeDtypeStruct((batch, 1, MEM_W), F32),
        compiler_params=_cparams(("arbitrary",)),
        name="mem_attn_step",
    )(mq3, cache_k, cache_v)


def _routing(logits):
    lane = lax.broadcasted_iota(jnp.int32, logits.shape, 1)
    lane_f = lane.astype(F32)
    big = 1000.0
    is_g = (lane >= N_EXPERTS) & (lane < N_EXPERTS + 4)
    lg = jnp.where(is_g, logits, -jnp.inf)
    gmax = jnp.max(lg, axis=1, keepdims=True)
    gidx = jnp.min(jnp.where(lg == gmax, lane_f, big), axis=1, keepdims=True) - float(N_EXPERTS)
    p_g = 1.0 / jnp.sum(jnp.exp(lg - gmax), axis=1, keepdims=True)
    in_grp = (lane < N_EXPERTS) & ((lane >> 2).astype(F32) == gidx)
    le = jnp.where(in_grp, logits, -jnp.inf)
    v1 = jnp.max(le, axis=1, keepdims=True)
    i1 = jnp.min(jnp.where(le == v1, lane_f, big), axis=1, keepdims=True)
    le2 = jnp.where(lane_f == i1, -jnp.inf, le)
    v2 = jnp.max(le2, axis=1, keepdims=True)
    i2 = jnp.min(jnp.where(le2 == v2, lane_f, big), axis=1, keepdims=True)
    e2 = jnp.exp(v2 - v1)
    inv = 1.0 / (1.0 + e2)
    comb = jnp.where(lane_f == i1, inv * p_g, jnp.where(lane_f == i2, e2 * inv * p_g, 0.0))
    return comb, gidx


GROUP_LANE = 16
RANK_LANE = 17


def _mix_kernel(x_ref, tok_ref, mem_ref, wt_ref, wm_ref, g_ref, b_ref, wr_ref, br_ref,
                x1_ref, comb_ref, cnt_ref, *, hp):
    @pl.when(pl.program_id(0) == 0)
    def _():
        cnt_ref[...] = jnp.zeros_like(cnt_ref)

    dot = _dot_hp if hp else _bdot
    mix = dot(tok_ref[...], wt_ref[...]) + dot(mem_ref[...], wm_ref[...])
    x1 = _layer_norm(ALPHA * x_ref[...] + mix, g_ref[...], b_ref[...])
    x1_ref[...] = x1
    logits = dot(x1, wr_ref[...]) + br_ref[...]
    comb, gidx = _routing(logits)
    tm = comb.shape[0]
    lane_f = lax.broadcasted_iota(jnp.int32, comb.shape, 1).astype(F32)
    mine = lane_f == gidx
    onehot = jnp.where(mine, 1.0, 0.0)
    row = lax.broadcasted_iota(jnp.int32, (tm, tm), 0)
    col = lax.broadcasted_iota(jnp.int32, (tm, tm), 1)
    before = jnp.where(col < row, 1.0, 0.0).astype(BF16)
    prefix = jnp.dot(before, onehot.astype(BF16), preferred_element_type=F32)
    cnt = cnt_ref[...]
    rank = jnp.sum(jnp.where(mine, prefix + cnt, 0.0), axis=1, keepdims=True)
    cnt_ref[...] = cnt + jnp.sum(onehot, axis=0, keepdims=True)
    comb_ref[...] = jnp.where(lane_f == float(GROUP_LANE), gidx,
                              jnp.where(lane_f == float(RANK_LANE), rank, comb))


def _mix(x, tok, mem, wt, wm, g, b, wr, br, tm, hp=False):
    m_tok = x.shape[0]
    const = lambda i: (0, 0)
    rows = lambda i: (i, 0)
    return pl.pallas_call(
        functools.partial(_mix_kernel, hp=hp),
        grid=(m_tok // tm,),
        in_specs=[pl.BlockSpec((tm, D_MODEL), rows),
                  pl.BlockSpec((tm, TOK_W), rows),
                  pl.BlockSpec((tm, MEM_W), rows),
                  pl.BlockSpec((TOK_W, D_MODEL), const),
                  pl.BlockSpec((MEM_W, D_MODEL), const),
                  pl.BlockSpec((1, D_MODEL), const),
                  pl.BlockSpec((1, D_MODEL), const),
                  pl.BlockSpec((D_MODEL, LANES), const),
                  pl.BlockSpec((1, LANES), const)],
        out_specs=[pl.BlockSpec((tm, D_MODEL), rows),
                   pl.BlockSpec((tm, LANES), rows),
                   pl.BlockSpec((1, LANES), const)],
        out_shape=[jax.ShapeDtypeStruct((m_tok, D_MODEL), F32),
                   jax.ShapeDtypeStruct((m_tok, LANES), F32),
                   jax.ShapeDtypeStruct((1, LANES), F32)],
        compiler_params=_cparams(("arbitrary",)),
        name="mix",
    )(x, tok, mem, wt, wm, g, b, wr, br)


def _moe_kernel(x_ref, comb_ref, w1_ref, w3_ref, w2_ref, g_ref, b_ref, o_ref, xb_ref, acc_ref,
                *, hp):
    e = pl.program_id(1)

    @pl.when(e == 0)
    def _():
        xb_ref[...] = x_ref[...].astype(BF16)
        acc_ref[...] = jnp.zeros_like(acc_ref)

    if hp:
        x = x_ref[...]
        a = _dot_hp(x, w1_ref[0, 0])
        b = _dot_hp(x, w3_ref[0, 0])
    else:
        xb = xb_ref[...]
        a = jnp.dot(xb, w1_ref[0, 0].astype(BF16), preferred_element_type=F32)
        b = jnp.dot(xb, w3_ref[0, 0].astype(BF16), preferred_element_type=F32)
    comb = comb_ref[...]
    lane = lax.broadcasted_iota(jnp.int32, comb.shape, 1)
    ce = jnp.sum(jnp.where(lane == e, comb, 0.0), axis=1, keepdims=True)
    hid = (a * _sigmoid(a)) * b * ce
    if hp:
        acc_ref[...] += _dot_hp(hid, w2_ref[0, 0])
    else:
        acc_ref[...] += jnp.dot(hid.astype(BF16), w2_ref[0, 0].astype(BF16),
                                preferred_element_type=F32)

    @pl.when(e == N_EXPERTS - 1)
    def _():
        o_ref[...] = _layer_norm(ALPHA * x_ref[...] + acc_ref[...], g_ref[...], b_ref[...])


def _moe(x1, comb, w1, w3, w2, g, b, li, tm, hp=False):
    m_tok = x1.shape[0]
    rows = lambda i, e: (i, 0)
    const = lambda i, e: (0, 0)
    return pl.pallas_call(
        functools.partial(_moe_kernel, hp=hp),
        grid=(m_tok // tm, N_EXPERTS),
        in_specs=[pl.BlockSpec((tm, D_MODEL), rows),
                  pl.BlockSpec((tm, LANES), rows),
                  pl.BlockSpec((1, 1, D_MODEL, D_EXPERT), lambda i, e: (li, e, 0, 0)),
                  pl.BlockSpec((1, 1, D_MODEL, D_EXPERT), lambda i, e: (li, e, 0, 0)),
                  pl.BlockSpec((1, 1, D_EXPERT, D_MODEL), lambda i, e: (li, e, 0, 0)),
                  pl.BlockSpec((1, D_MODEL), const),
                  pl.BlockSpec((1, D_MODEL), const)],
        out_specs=pl.BlockSpec((tm, D_MODEL), rows),
        out_shape=jax.ShapeDtypeStruct((m_tok, D_MODEL), F32),
        scratch_shapes=[pltpu.VMEM((tm, D_MODEL), BF16), pltpu.VMEM((tm, D_MODEL), F32)],
        compiler_params=_cparams(("arbitrary", "arbitrary")),
        name="moe",
    )(x1, comb, w1, w3, w2, g, b)


SLOT_TILE = 256
N_GROUPS = 4
EXP_PER_GROUP = 4
MOE_VMEM_LIMIT = 56 * 1024 * 1024


def _dispatch_kernel(dest_ref, x_ref, xs_in_ref, xs_ref, inv_ref, sem, *, tm, n_slots):
    del xs_in_ref
    i = pl.program_id(0)

    @pl.when(i == 0)
    def _():
        def init(s, carry):
            inv_ref[s] = -1
            return carry
        lax.fori_loop(0, n_slots, init, 0, unroll=16)

    base = i * tm

    def body(r, carry):
        d = dest_ref[base + r]
        pltpu.make_async_copy(x_ref.at[pl.ds(r, 1)], xs_ref.at[pl.ds(d, 1)], sem).start()
        inv_ref[d] = base + r
        return carry

    lax.fori_loop(0, tm, body, 0, unroll=8)
    pltpu.make_async_copy(x_ref, xs_ref.at[pl.ds(0, tm)], sem).wait()


def _dispatch(dest, x1, n_slots, tm):
    m_tok = x1.shape[0]
    xs_zero = jnp.zeros((n_slots, D_MODEL), F32)
    grid_spec = pltpu.PrefetchScalarGridSpec(
        num_scalar_prefetch=1,
        grid=(m_tok // tm,),
        in_specs=[pl.BlockSpec((tm, D_MODEL), lambda i, dest: (i, 0)),
                  pl.BlockSpec(memory_space=pl.ANY)],
        out_specs=[pl.BlockSpec(memory_space=pl.ANY),
                   pl.BlockSpec(memory_space=pltpu.SMEM)],
        scratch_shapes=[pltpu.SemaphoreType.DMA(())],
    )
    return pl.pallas_call(
        functools.partial(_dispatch_kernel, tm=tm, n_slots=n_slots),
        grid_spec=grid_spec,
        out_shape=[jax.ShapeDtypeStruct((n_slots, D_MODEL), F32),
                   jax.ShapeDtypeStruct((n_slots,), jnp.int32)],
        input_output_aliases={2: 0},
        compiler_params=_cparams(("arbitrary",)),
        name="moe_dispatch",
    )(dest, x1, xs_zero)


def _moe_sparse_kernel(blk_ref, grp_ref, val_ref, nrow_ref, inv_ref,
                       xs_ref, w1_ref, w3_ref, w2_ref, wr_ref, br_ref, g_ref, b_ref,
                       out_ref, w1b, w3b, w2b, ybuf, junk, state, sem, *, n_tiles):
    i = pl.program_id(0)
    ts = SLOT_TILE

    @pl.when(i == 0)
    def _():
        state[0] = -1
        state[1] = 0

    def wait_pending():
        @pl.when(state[1] == 1)
        def _():
            pltpu.make_async_copy(ybuf, junk, sem).wait()

    @pl.when(val_ref[i] == 1)
    def _():
        grp = grp_ref[i]

        @pl.when(grp != state[0])
        def _():
            for j in range(EXP_PER_GROUP):
                cols = slice(j * D_EXPERT, (j + 1) * D_EXPERT)
                w1b[:, cols] = w1_ref[0, 0, j].astype(BF16)
                w3b[:, cols] = w3_ref[0, 0, j].astype(BF16)
                w2b[cols, :] = w2_ref[0, 0, j].astype(BF16)
            state[0] = grp

        x = xs_ref[...]
        xb = x.astype(BF16)
        logits = jnp.dot(xb, wr_ref[...], preferred_element_type=F32) + br_ref[...]
        comb, _ = _routing(logits)
        lane = lax.broadcasted_iota(jnp.int32, comb.shape, 1)
        a = jnp.dot(xb, w1b[...], preferred_element_type=F32)
        b = jnp.dot(xb, w3b[...], preferred_element_type=F32)
        hid = []
        for j in range(EXP_PER_GROUP):
            cols = slice(j * D_EXPERT, (j + 1) * D_EXPERT)
            ce = jnp.sum(jnp.where(lane == grp * EXP_PER_GROUP + j, comb, 0.0), axis=1, keepdims=True)
            a_j = a[:, cols]
            hid.append(((a_j * _sigmoid(a_j)) * b[:, cols] * ce).astype(BF16))
        acc = jnp.dot(jnp.concatenate(hid, axis=1), w2b[...], preferred_element_type=F32)
        x2 = _layer_norm(ALPHA * x + acc, g_ref[...], b_ref[...])

        wait_pending()
        ybuf[...] = x2
        n = nrow_ref[i]
        base = blk_ref[i] * ts

        def body(r, carry):
            t = inv_ref[base + r]
            pltpu.make_async_copy(ybuf.at[pl.ds(r, 1)], out_ref.at[pl.ds(t, 1)], sem).start()
            return carry

        def pad_body(r, carry):
            pltpu.make_async_copy(ybuf.at[pl.ds(r, 1)], junk.at[pl.ds(r, 1)], sem).start()
            return carry

        full = n // 8

        def body8(gi, carry):
            for u in range(8):
                body(gi * 8 + u, carry)
            return carry

        lax.fori_loop(0, full, body8, 0)
        lax.fori_loop(full * 8, n, body, 0)
        lax.fori_loop(n, ts, pad_body, 0)
        state[1] = 1

    @pl.when(i == n_tiles - 1)
    def _():
        wait_pending()
        state[1] = 0


def _moe_sparse(x1, comb, cnt, P, W, li):
    m_tok = x1.shape[0]
    ts = SLOT_TILE
    n_tiles = m_tok // ts + N_GROUPS
    n_slots = n_tiles * ts
    grp_tok = comb[:, GROUP_LANE].astype(jnp.int32)
    rank_tok = comb[:, RANK_LANE].astype(jnp.int32)
    counts = cnt[0, :N_GROUPS].astype(jnp.int32)
    tiles_g = (counts + ts - 1) // ts
    tile_end = jnp.cumsum(tiles_g)
    slot_base = (tile_end - tiles_g) * ts
    dest = jnp.take(slot_base, grp_tok) + rank_tok
    total = tile_end[-1]
    tile_i = jnp.arange(n_tiles, dtype=jnp.int32)
    valid = (tile_i < total).astype(jnp.int32)
    blk = jnp.minimum(tile_i, total - 1)
    grp_tile = jnp.minimum(jnp.sum((tile_end[None, :] <= blk[:, None]).astype(jnp.int32), axis=1),
                           N_GROUPS - 1)
    n_rows = jnp.clip(jnp.take(slot_base + counts, grp_tile) - blk * ts, 0, ts) * valid

    xs, inv = _dispatch(dest, x1, n_slots, 512)

    w1 = P['w_e1'].reshape(DEPTH, N_GROUPS, EXP_PER_GROUP, D_MODEL, D_EXPERT)
    w3 = P['w_e3'].reshape(DEPTH, N_GROUPS, EXP_PER_GROUP, D_MODEL, D_EXPERT)
    w2 = P['w_e2'].reshape(DEPTH, N_GROUPS, EXP_PER_GROUP, D_EXPERT, D_MODEL)
    const = lambda i, *_: (0, 0)
    grid_spec = pltpu.PrefetchScalarGridSpec(
        num_scalar_prefetch=5,
        grid=(n_tiles,),
        in_specs=[pl.BlockSpec((ts, D_MODEL), lambda i, blk, grp, val, nrow, inv: (blk[i], 0)),
                  pl.BlockSpec((1, 1, EXP_PER_GROUP, D_MODEL, D_EXPERT),
                               lambda i, blk, grp, val, nrow, inv: (li, grp[i], 0, 0, 0)),
                  pl.BlockSpec((1, 1, EXP_PER_GROUP, D_MODEL, D_EXPERT),
                               lambda i, blk, grp, val, nrow, inv: (li, grp[i], 0, 0, 0)),
                  pl.BlockSpec((1, 1, EXP_PER_GROUP, D_EXPERT, D_MODEL),
                               lambda i, blk, grp, val, nrow, inv: (li, grp[i], 0, 0, 0)),
                  pl.BlockSpec((D_MODEL, LANES), const),
                  pl.BlockSpec((1, LANES), const),
                  pl.BlockSpec((1, D_MODEL), const),
                  pl.BlockSpec((1, D_MODEL), const)],
        out_specs=pl.BlockSpec(memory_space=pl.ANY),
        scratch_shapes=[pltpu.VMEM((D_MODEL, EXP_PER_GROUP * D_EXPERT), BF16),
                        pltpu.VMEM((D_MODEL, EXP_PER_GROUP * D_EXPERT), BF16),
                        pltpu.VMEM((EXP_PER_GROUP * D_EXPERT, D_MODEL), BF16),
                        pltpu.VMEM((ts, D_MODEL), F32),
                        pltpu.VMEM((ts, D_MODEL), F32),
                        pltpu.SMEM((2,), jnp.int32),
                        pltpu.SemaphoreType.DMA(())],
    )
    return pl.pallas_call(
        functools.partial(_moe_sparse_kernel, n_tiles=n_tiles),
        grid_spec=grid_spec,
        out_shape=jax.ShapeDtypeStruct((m_tok, D_MODEL), F32),
        compiler_params=pltpu.CompilerParams(dimension_semantics=("arbitrary",),
                                             vmem_limit_bytes=MOE_VMEM_LIMIT),
        name="moe_sparse",
    )(blk, grp_tile, valid, n_rows, inv, xs, w1, w3, w2, W['w_r'][li], W['b_r'][li],
      P['ln_ffn_g'][li][None, :], P['ln_ffn_b'][li][None, :])


def _diff_lambda(lq_ref, lam_init):
    lq = lq_ref[0]
    a = jnp.sum(lq[0:1, :] * lq[1:2, :], axis=1, keepdims=True)
    b = jnp.sum(lq[2:3, :] * lq[3:4, :], axis=1, keepdims=True)
    return jnp.exp(a) - jnp.exp(b) + lam_init


def _sub_norm(o, g_row, lam_init):
    o = o * lax.rsqrt(jnp.mean(o * o, axis=-1, keepdims=True) + LN_EPS) * g_row
    return o * (1.0 - lam_init)


def _diff_attn_kernel(q1_ref, q2_ref, k1_ref, k2_ref, vt_ref, lq_ref, gs_ref, o_ref,
                      m_scr, l_scr, acc_scr, *, tq, lam_init):
    i = pl.program_id(2)
    lam = _diff_lambda(lq_ref, lam_init)
    q_scale = (DQK ** -0.5) * math.log2(math.e)
    qs = [(q1_ref[...] * q_scale).astype(BF16), (q2_ref[...] * q_scale).astype(BF16)]
    k_refs = [k1_ref, k2_ref]
    m_scr[...] = jnp.full_like(m_scr, -jnp.inf)
    l_scr[...] = jnp.zeros_like(l_scr)
    acc_scr[...] = jnp.zeros_like(acc_scr)
    key_i = lax.broadcasted_iota(jnp.int32, (tq, tq), 0)
    qry_i = lax.broadcasted_iota(jnp.int32, (tq, tq), 1)
    causal = key_i <= qry_i

    def block(j, masked):
        start = pl.multiple_of(j * tq, tq)
        sts = []
        for hh in range(2):
            for mm in range(2):
                kb = k_refs[mm][pl.ds(start, tq), hh * DQK:(hh + 1) * DQK]
                st = lax.dot_general(kb, qs[mm][:, hh * DQK:(hh + 1) * DQK],
                                     (((1,), (1,)), ((), ())), preferred_element_type=F32)
                if masked:
                    st = jnp.where(causal, st, -jnp.inf)
                sts.append(st)
        m_all = m_scr[...]
        l_all = l_scr[...]
        ps, alphas, m_rows, l_rows = [], [], [], []
        for idx in range(4):
            m_old = m_all[idx:idx + 1, :]
            m_new = jnp.maximum(m_old, jnp.max(sts[idx], axis=0, keepdims=True))
            alpha = jnp.exp2(m_old - m_new)
            p = jnp.exp2(sts[idx] - m_new)
            l_rows.append(alpha * l_all[idx:idx + 1, :] + jnp.sum(p, axis=0, keepdims=True))
            m_rows.append(m_new)
            alphas.append(alpha)
            ps.append(p.astype(BF16))
        m_scr[...] = jnp.concatenate(m_rows + [m_all[4:, :]], axis=0)
        l_scr[...] = jnp.concatenate(l_rows + [l_all[4:, :]], axis=0)
        for idx in range(4):
            vt = vt_ref[j, (idx // 2) * DH:(idx // 2 + 1) * DH, :]
            acc_scr[idx] = alphas[idx] * acc_scr[idx] + jnp.dot(vt, ps[idx],
                                                                preferred_element_type=F32)

    def body(j, carry):
        block(j, False)
        return carry

    lax.fori_loop(0, i, body, 0)
    block(i, True)

    for hh in range(2):
        o1 = acc_scr[hh * 2] / l_scr[hh * 2:hh * 2 + 1, :]
        o2 = acc_scr[hh * 2 + 1] / l_scr[hh * 2 + 1:hh * 2 + 2, :]
        o = o1 - lam * o2
        o = o * lax.rsqrt(jnp.mean(o * o, axis=0, keepdims=True) + LN_EPS)
        o_ref[:, hh * DH:(hh + 1) * DH] = o.T * gs_ref[...] * (1.0 - lam_init)


def _diff_attn_prompt(proj_b, k_b, vt_b, lam_qk, g_sub, j, lam_init, batch, seq, tq):
    m_tok = batch * seq
    nq = seq // tq
    kern = functools.partial(_diff_attn_kernel, tq=tq, lam_init=lam_init)
    return pl.pallas_call(
        kern,
        grid=(batch, 3, nq),
        in_specs=[pl.BlockSpec((tq, 2 * DQK), lambda b, hp, i: (b * nq + i, hp)),
                  pl.BlockSpec((tq, 2 * DQK), lambda b, hp, i: (b * nq + i, 3 + hp)),
                  pl.BlockSpec((seq, 2 * DQK), lambda b, hp, i: (b, hp)),
                  pl.BlockSpec((seq, 2 * DQK), lambda b, hp, i: (b, 3 + hp)),
                  pl.BlockSpec((nq, 2 * DH, tq), lambda b, hp, i: (b, hp, 0)),
                  pl.BlockSpec((1, 4, DQK), lambda b, hp, i: (j, 0, 0)),
                  pl.BlockSpec((1, DH), lambda b, hp, i: (0, 0))],
        out_specs=pl.BlockSpec((tq, 2 * DH), lambda b, hp, i: (b * nq + i, hp)),
        out_shape=jax.ShapeDtypeStruct((m_tok, TOK_W), F32),
        scratch_shapes=[pltpu.VMEM((8, tq), F32), pltpu.VMEM((8, tq), F32),
                        pltpu.VMEM((4, DH, tq), F32)],
        compiler_params=_cparams(("arbitrary", "arbitrary", "arbitrary")),
        name="diff_attn",
    )(proj_b, proj_b, k_b, k_b, vt_b, lam_qk, g_sub)


PAGES_PER_STEP = 8


def _diff_attn_step_kernel(pt_ref, q_ref, kn_ref, vn_ref, *rest, n_steps, lam_init):
    k_refs = rest[:PAGES_PER_STEP]
    v_refs = rest[PAGES_PER_STEP:2 * PAGES_PER_STEP]
    lq_ref, gs_ref, o_ref, qm_scr, m_scr, l_scr, acc_scr = rest[2 * PAGES_PER_STEP:]
    p_idx = pl.program_id(1)
    rows = 16
    rid = lax.broadcasted_iota(jnp.int32, (rows, TOK_W), 0)
    lane = lax.broadcasted_iota(jnp.int32, (rows, TOK_W), 1)

    @pl.when(p_idx == 0)
    def _():
        q = jnp.broadcast_to(q_ref[0] * (DQK ** -0.5), (rows, TOK_W))
        qm_scr[...] = jnp.where((lane // DQK) == rid, q, 0.0)
        m_scr[...] = jnp.full_like(m_scr, -jnp.inf)
        l_scr[...] = jnp.zeros_like(l_scr)
        acc_scr[...] = jnp.zeros_like(acc_scr)

    qm = qm_scr[...]
    q_hi, q_lo = _split2(qm)
    q2 = jnp.concatenate([q_hi, q_lo], axis=0)
    s_pages = []
    for k_ref in k_refs:
        k_hi, k_lo = _split2(k_ref[0])
        r1 = _bdot(q2, k_hi)
        s_pages.append((_bdot(q_hi, k_lo) + r1[rows:]) + r1[:rows])
    s = jnp.concatenate(s_pages, axis=1)
    m_old = m_scr[...]
    m_new = jnp.maximum(m_old, jnp.max(s, axis=1, keepdims=True))
    alpha = jnp.exp(m_old - m_new)
    p = jnp.exp(s - m_new)
    p_hi, p_lo = _split2(p)
    p2 = jnp.concatenate([p_hi, p_lo], axis=0)
    pv = []
    for h in range(N_HEAD):
        pv_h = None
        for g in range(PAGES_PER_STEP):
            cols = slice(g * PAGE, (g + 1) * PAGE)
            v_hi, v_lo = _split2(v_refs[g][0, h])
            r1 = _bdot(p2[:, cols], v_hi)
            term = (_bdot(p_hi[:, cols], v_lo) + r1[rows:]) + r1[:rows]
            pv_h = term if pv_h is None else pv_h + term
        pv.append(pv_h)
    l_scr[...] = alpha * l_scr[...] + jnp.sum(p, axis=1, keepdims=True)
    acc_scr[...] = alpha * acc_scr[...] + jnp.concatenate(pv, axis=1)
    m_scr[...] = m_new

    @pl.when(p_idx == n_steps - 1)
    def _():
        lam = _diff_lambda(lq_ref, lam_init)
        s_new = jnp.sum(qm * kn_ref[0], axis=1, keepdims=True)
        m_old2 = m_scr[...]
        m_fin = jnp.maximum(m_old2, s_new)
        a2 = jnp.exp(m_old2 - m_fin)
        p_new = jnp.exp(s_new - m_fin)
        l_fin = a2 * l_scr[...] + p_new
        acc = a2 * acc_scr[...] + p_new * vn_ref[0]
        r = acc / l_fin
        head_of_lane = lane // DH
        o1 = jnp.sum(jnp.where(head_of_lane == rid, r, 0.0), axis=0, keepdims=True)
        o2 = jnp.sum(jnp.where(head_of_lane == rid - N_HEAD, r, 0.0), axis=0, keepdims=True)
        o = o1 - lam * o2
        for h in range(N_HEAD):
            sl = slice(h * DH, (h + 1) * DH)
            o_ref[0, :, sl] = _sub_norm(o[:, sl], gs_ref[...], lam_init)


def _diff_attn_sample(page_table, q3, kn3, vn3, cache_k, cache_v, lam_qk, g_sub, j, lam_init):
    batch, n_pages = page_table.shape
    n_steps = n_pages // PAGES_PER_STEP
    kern = functools.partial(_diff_attn_step_kernel, n_steps=n_steps, lam_init=lam_init)
    row3 = lambda b, p, pt: (b, 0, 0)

    def k_spec(g):
        return pl.BlockSpec((1, TOK_W, PAGE), lambda b, p, pt: (pt[b, p * PAGES_PER_STEP + g], 0, 0))

    def v_spec(g):
        return pl.BlockSpec((1, N_HEAD, PAGE, DH),
                            lambda b, p, pt: (pt[b, p * PAGES_PER_STEP + g], 0, 0, 0))

    k_specs = [k_spec(g) for g in range(PAGES_PER_STEP)]
    v_specs = [v_spec(g) for g in range(PAGES_PER_STEP)]
    grid_spec = pltpu.PrefetchScalarGridSpec(
        num_scalar_prefetch=1,
        grid=(batch, n_steps),
        in_specs=[pl.BlockSpec((1, 1, TOK_W), row3),
                  pl.BlockSpec((1, 1, TOK_W), row3),
                  pl.BlockSpec((1, 1, TOK_W), row3)] + k_specs + v_specs + [
                  pl.BlockSpec((1, 4, DQK), lambda b, p, pt: (j, 0, 0)),
                  pl.BlockSpec((1, DH), lambda b, p, pt: (0, 0))],
        out_specs=pl.BlockSpec((1, 1, TOK_W), row3),
        scratch_shapes=[pltpu.VMEM((16, TOK_W), F32), pltpu.VMEM((16, 1), F32),
                        pltpu.VMEM((16, 1), F32), pltpu.VMEM((16, TOK_W), F32)],
    )
    return pl.pallas_call(
        kern,
        grid_spec=grid_spec,
        out_shape=jax.ShapeDtypeStruct((batch, 1, TOK_W), F32),
        compiler_params=_cparams(("arbitrary", "arbitrary")),
        name="diff_attn_step",
    )(page_table, q3, kn3, vn3, *([cache_k] * PAGES_PER_STEP), *([cache_v] * PAGES_PER_STEP),
      lam_qk, g_sub)


def _diff_lambda_init(li):
    return 0.8 - 0.6 * math.exp(-0.3 * li)


def _prep_weights(P):
    w = {}
    wa = P['w_in_a']
    gates = jnp.pad(wa[:, :, 4 * TOK_W:4 * TOK_W + 2 * N_HEAD], ((0, 0), (0, 0), (0, LANES - 2 * N_HEAD)))
    w['w_a32'] = jnp.concatenate([wa[:, :, :4 * TOK_W], wa[:, :, 4 * TOK_W + 2 * N_HEAD:], gates],
                                 axis=-1)
    w['w_a'] = w['w_a32'].astype(BF16)
    w['gate_bias'] = jnp.pad(jnp.concatenate([P['b_igate'], P['b_fgate']], axis=-1),
                             ((0, 0), (0, LANES - 2 * N_HEAD)))[:, None, :]
    w['w_b'] = P['w_in_b'].astype(BF16)
    w['w_kv'] = P['w_kv_shared'].astype(BF16)
    w['w_mem'] = jnp.transpose(P['w_mem_kv'], (1, 0, 2)).reshape(D_MODEL, DEPTH * 2 * MEM_W).astype(BF16)
    w['w_out_t32'] = P['w_out'][:, :TOK_W, :]
    w['w_out_m32'] = P['w_out'][:, TOK_W:, :]
    w['w_out_t'] = w['w_out_t32'].astype(BF16)
    w['w_out_m'] = w['w_out_m32'].astype(BF16)
    wr = jnp.concatenate([P['w_router'], P['w_group']], axis=-1)
    w['w_r32'] = jnp.pad(wr, ((0, 0), (0, 0), (0, LANES - wr.shape[-1])))
    w['w_r'] = w['w_r32'].astype(BF16)
    br = jnp.concatenate([P['b_router'], P['b_group']], axis=-1)
    w['b_r'] = jnp.pad(br, ((0, 0), (0, LANES - br.shape[-1])))[:, None, :]
    return w


def _layer_tail(x, tok, mem, li, P, W, tm_mix, tm_moe, hp=False):
    sfx = '32' if hp else ''
    x1, comb, cnt = _mix(x, tok, mem, W['w_out_t' + sfx][li], W['w_out_m' + sfx][li],
                         P['ln_mix_g'][li][None, :], P['ln_mix_b'][li][None, :],
                         W['w_r' + sfx][li], W['b_r'][li], tm_mix, hp)
    if hp:
        return _moe(x1, comb, P['w_e1'], P['w_e3'], P['w_e2'],
                    P['ln_ffn_g'][li][None, :], P['ln_ffn_b'][li][None, :], li, tm_moe, hp)
    return _moe_sparse(x1, comb, cnt, P, W, li)


def _prompt_trunk(x_prompt, mem_prompt, P, W):
    batch, seq, _ = x_prompt.shape
    m_tok = batch * seq
    x = x_prompt.reshape(m_tok, D_MODEL)
    mem = mem_prompt.reshape(batch * N_MEM, D_MODEL)
    kv_mem = _proj(mem, W['w_mem'], 512)
    kv_mem = jnp.transpose(kv_mem.reshape(batch, N_MEM, DEPTH, 2, MEM_W), (3, 2, 0, 1, 4))
    mem_k, mem_v = kv_mem[0], kv_mem[1]

    new_c, new_n, new_m = [], [], []
    k_t = v_h = k16 = vt16 = None
    for li in range(DEPTH):
        if li < N_A:
            proj = _proj(x, W['w_a'][li], 256)
            tok, c, n, m = _mlstm_prompt(proj, W['gate_bias'][li], batch, seq)
            new_c.append(c)
            new_n.append(n)
            new_m.append(m[:, 0, :N_HEAD])
            mq_block = (4 * TOK_W) // MEM_W
        else:
            if li == N_A:
                k_t, v_h, k16, vt16 = _proj_shared_kv(x, W['w_kv'], batch, seq, 256)
            j = li - N_A
            proj = _proj(x, W['w_b'][j], 512)
            tok = _diff_attn_prompt(proj, k16, vt16, P['lambda_qk'], P['subln_g'][j:j + 1], j,
                                    _diff_lambda_init(li), batch, seq, 256)
            mq_block = TOK_W // MEM_W
        mem_o = _mem_attn_prompt(proj, mq_block, mem_k[li], mem_v[li], seq, 512)
        x = _layer_tail(x, tok, mem_o, li, P, W, 512, 1024)
    y = x.reshape(batch, seq, D_MODEL)
    k_p = jnp.transpose(k_t.reshape(batch, 2, N_HEAD, DQK, seq), (0, 4, 1, 2, 3))
    v_p = jnp.transpose(v_h, (0, 2, 1, 3))
    shp = (DEPTH, batch, N_MEM, H_M, DH_M)
    return (y, mem_k.reshape(shp), mem_v.reshape(shp),
            jnp.stack(new_c), jnp.stack(new_n), jnp.stack(new_m), k_p, v_p)


def _sample_trunk(x_sample, cache_mem_k, cache_mem_v, state, cache_k, cache_v, page_table, P, W):
    batch = x_sample.shape[0]
    x = x_sample.reshape(batch, D_MODEL)
    c_all, n_all, m_all = state
    mem_k = jnp.transpose(cache_mem_k, (0, 1, 3, 4, 2)).reshape(DEPTH, batch, MEM_W, N_MEM)
    mem_v = jnp.transpose(cache_mem_v, (0, 1, 3, 4, 2)).reshape(DEPTH, batch, MEM_W, N_MEM)
    ck = jnp.transpose(cache_k, (0, 2, 3, 4, 1)).reshape(cache_k.shape[0], TOK_W, PAGE)
    cv = jnp.transpose(cache_v, (0, 2, 1, 3))

    new_c, new_n, new_m = [], [], []
    kv32 = None
    for li in range(DEPTH):
        if li < N_A:
            proj = _proj(x, W['w_a32'][li], batch, 384, hp=True)
            m_pad = jnp.pad(m_all[li], ((0, 0), (0, LANES - N_HEAD)))[:, None, :]
            tok3, c, n, m = _mlstm_sample(proj[:, None, :], W['gate_bias'][li], c_all, n_all, m_pad, li)
            new_c.append(c)
            new_n.append(n)
            new_m.append(m[:, 0, :N_HEAD])
            mq = proj[:, 4 * TOK_W:4 * TOK_W + MEM_W]
        else:
            if li == N_A:
                kv32 = _proj(x, P['w_kv_shared'], batch, 384, hp=True)
            j = li - N_A
            proj = _proj(x, P['w_in_b'][j], batch, 256, hp=True)
            tok3 = _diff_attn_sample(page_table, proj[:, None, :TOK_W], kv32[:, None, :TOK_W],
                                     kv32[:, None, TOK_W:], ck, cv, P['lambda_qk'],
                                     P['subln_g'][j:j + 1], j, _diff_lambda_init(li))
            mq = proj[:, TOK_W:]
        mem_o = _mem_attn_sample(mq[:, None, :], mem_k, mem_v, li)
        x = _layer_tail(x, tok3[:, 0, :], mem_o[:, 0, :], li, P, W, batch, batch, hp=True)
    y = x.reshape(batch, 1, D_MODEL)
    k_s = kv32[:, :TOK_W].reshape(batch, 1, 2, N_HEAD, DQK)
    v_s = kv32[:, TOK_W:].reshape(batch, 1, N_HEAD, DH)
    return y, jnp.stack(new_c), jnp.stack(new_n), jnp.stack(new_m), k_s, v_s


def kernel(x_prompt, x_sample, mem_prompt, cache_mem_k, cache_mem_v, state_mlstm_C, state_mlstm_n,
           state_mlstm_m, cache_k, cache_v, page_table, w_in_a, b_igate, b_fgate, w_in_b, lambda_qk,
           subln_g, w_kv_shared, w_mem_kv, w_out, ln_mix_g, ln_mix_b, ln_ffn_g, ln_ffn_b, w_group,
           b_group, w_router, b_router, w_e1, w_e3, w_e2):
    P = {'w_in_a': w_in_a, 'b_igate': b_igate, 'b_fgate': b_fgate, 'w_in_b': w_in_b,
         'lambda_qk': lambda_qk, 'subln_g': subln_g, 'w_kv_shared': w_kv_shared, 'w_mem_kv': w_mem_kv,
         'w_out': w_out, 'ln_mix_g': ln_mix_g, 'ln_mix_b': ln_mix_b, 'ln_ffn_g': ln_ffn_g,
         'ln_ffn_b': ln_ffn_b, 'w_group': w_group, 'b_group': b_group, 'w_router': w_router,
         'b_router': b_router, 'w_e1': w_e1, 'w_e3': w_e3, 'w_e2': w_e2}
    W = _prep_weights(P)
    y_p, mem_k_p, mem_v_p, c_p, n_p, m_p, k_p, v_p = _prompt_trunk(x_prompt, mem_prompt, P, W)
    y_s, c_s, n_s, m_s, k_s, v_s = _sample_trunk(
        x_sample, cache_mem_k, cache_mem_v, (state_mlstm_C, state_mlstm_n, state_mlstm_m),
        cache_k, cache_v, page_table, P, W)
    return (y_p, y_s, mem_k_p, mem_v_p, c_p, n_p, m_p, k_p, v_p, c_s, n_s, m_s, k_s, v_s)
```

```python
import functools
import math

import jax
import jax.numpy as jnp
from jax import lax
from jax.experimental import pallas as pl
from jax.experimental.pallas import tpu as pltpu

F32 = jnp.float32
BF16 = jnp.bfloat16

D_MODEL = 1024
DEPTH = 4
N_A = 2
TOK_W = 768
MEM_W = 256
N_HEAD = 6
DH = 128
DQK = 64
H_M = 4
DH_M = 64
N_MEM = 256
CHUNK = 128
N_EXPERTS = 16
D_EXPERT = 256
ALPHA = (2.0 * DEPTH) ** 0.25
LN_EPS = 1e-5
PAGE = 128
LANES = 128
GATE_COL = 3328
A_COLS = 3456
VMEM_LIMIT = 48 * 1024 * 1024


def _cparams(sem):
    return pltpu.CompilerParams(dimension_semantics=sem, vmem_limit_bytes=VMEM_LIMIT)


def _bdot(a, b):
    return jnp.dot(a.astype(BF16), b.astype(BF16), preferred_element_type=F32)


def _bdot_nt(a, b):
    return lax.dot_general(a.astype(BF16), b.astype(BF16), (((1,), (1,)), ((), ())),
                           preferred_element_type=F32)


def _split3(a):
    hi = a.astype(BF16)
    r = a - hi.astype(F32)
    mid = r.astype(BF16)
    lo = (r - mid.astype(F32)).astype(BF16)
    return hi, mid, lo


def _split2(a):
    hi = a.astype(BF16)
    return hi, (a - hi.astype(F32)).astype(BF16)


def _dot_hp(a, b, nt=False):
    m = a.shape[0]
    a_hi, a_mid, a_lo = _split3(a)
    b_hi, b_mid, b_lo = _split3(b)
    dot = _bdot_nt if nt else _bdot
    a3 = jnp.concatenate([a_hi, a_mid, a_lo], axis=0)
    r1 = dot(a3, b_hi)
    r2 = dot(a3[:2 * m], b_mid)
    r3 = dot(a_hi, b_lo)
    return ((r3 + r2[m:]) + (r1[2 * m:] + r2[:m]) + r1[m:2 * m]) + r1[:m]


def _dot_x3(a, b, nt=False):
    m = a.shape[0]
    a_hi, a_lo = _split2(a)
    b_hi, b_lo = _split2(b)
    dot = _bdot_nt if nt else _bdot
    r1 = dot(jnp.concatenate([a_hi, a_lo], axis=0), b_hi)
    return (dot(a_hi, b_lo) + r1[m:]) + r1[:m]


def _layer_norm(y, g, b):
    mu = jnp.mean(y, axis=-1, keepdims=True)
    d = y - mu
    var = jnp.mean(d * d, axis=-1, keepdims=True)
    return d * lax.rsqrt(var + LN_EPS) * g + b


def _sigmoid(x):
    return 1.0 / (1.0 + jnp.exp(-x))


def _log_sigmoid(x):
    return jnp.minimum(x, 0.0) - jnp.log1p(jnp.exp(-jnp.abs(x)))


def _proj_kernel(x_ref, w_ref, o_ref, *, hp):
    if hp:
        o_ref[...] = _dot_hp(x_ref[...], w_ref[...])
    else:
        o_ref[...] = jnp.dot(x_ref[...].astype(BF16), w_ref[...], preferred_element_type=F32)


def _proj(x, w, tm, tn=None, hp=False):
    m, k = x.shape
    n = w.shape[1]
    tn = n if tn is None else tn
    return pl.pallas_call(
        functools.partial(_proj_kernel, hp=hp),
        grid=(m // tm, n // tn),
        in_specs=[pl.BlockSpec((tm, k), lambda i, j: (i, 0)),
                  pl.BlockSpec((k, tn), lambda i, j: (0, j))],
        out_specs=pl.BlockSpec((tm, tn), lambda i, j: (i, j)),
        out_shape=jax.ShapeDtypeStruct((m, n), F32),
        compiler_params=_cparams(("arbitrary", "arbitrary")),
        name="proj_hp" if hp else "proj",
    )(x, w)


def _proj_kv_kernel(x_ref, w_ref, kt_ref, vh_ref, kb_ref, vt_ref):
    y = jnp.dot(x_ref[...].astype(BF16), w_ref[...], preferred_element_type=F32)
    k = y[:, :TOK_W]
    v = y[:, TOK_W:]
    kt_ref[0] = k.T
    for h in range(N_HEAD):
        vh_ref[0, h] = v[:, h * DH:(h + 1) * DH]
    kb_ref[...] = k.astype(BF16)
    vt_ref[0] = v.T.astype(BF16)


def _proj_shared_kv(x, w, batch, seq, tm):
    m, k = x.shape
    n = w.shape[1]
    per_b = seq // tm
    return pl.pallas_call(
        _proj_kv_kernel,
        grid=(m // tm,),
        in_specs=[pl.BlockSpec((tm, k), lambda i: (i, 0)),
                  pl.BlockSpec((k, n), lambda i: (0, 0))],
        out_specs=[pl.BlockSpec((1, TOK_W, tm), lambda i: (i // per_b, 0, i % per_b)),
                   pl.BlockSpec((1, N_HEAD, tm, DH), lambda i: (i // per_b, 0, i % per_b, 0)),
                   pl.BlockSpec((tm, TOK_W), lambda i: (i, 0)),
                   pl.BlockSpec((1, TOK_W, tm), lambda i: (i, 0, 0))],
        out_shape=[jax.ShapeDtypeStruct((batch, TOK_W, seq), F32),
                   jax.ShapeDtypeStruct((batch, N_HEAD, seq, DH), F32),
                   jax.ShapeDtypeStruct((m, TOK_W), BF16),
                   jax.ShapeDtypeStruct((m // tm, TOK_W, tm), BF16)],
        compiler_params=_cparams(("arbitrary",)),
        name="proj_kv",
    )(x, w)


def _mlstm_chunk_kernel(q_ref, k_ref, v_ref, o_ref, g_ref, bias_ref,
                        tok_ref, c_ref, n_ref, m_ref):
    @pl.when(pl.program_id(1) == 0)
    def _():
        c_ref[...] = jnp.zeros_like(c_ref)
        n_ref[...] = jnp.zeros_like(n_ref)
        m_ref[...] = jnp.zeros_like(m_ref)

    L = CHUNK
    lane = lax.broadcasted_iota(jnp.int32, (L, LANES), 1)
    row = lax.broadcasted_iota(jnp.int32, (L, L), 0)
    col = lax.broadcasted_iota(jnp.int32, (L, L), 1)
    causal = col <= row

    gb = g_ref[...] + bias_ref[...]
    lf = _log_sigmoid(gb)
    tril = causal.astype(BF16)
    hi = lf.astype(BF16)
    r1 = lf - hi.astype(F32)
    mid = r1.astype(BF16)
    lo = (r1 - mid.astype(F32)).astype(BF16)
    fcum = (jnp.dot(tril, hi, preferred_element_type=F32)
            + jnp.dot(tril, mid, preferred_element_type=F32)
            + jnp.dot(tril, lo, preferred_element_type=F32))
    z = jnp.where(lane < N_HEAD, gb, fcum)
    zt = z.T

    m_row = m_ref[0]
    lane1 = lax.broadcasted_iota(jnp.int32, (1, LANES), 1)
    m_new_row = m_row
    scale = DH ** -0.5
    for h in range(N_HEAD):
        f_col = fcum[:, N_HEAD + h:N_HEAD + h + 1]
        f_row = zt[N_HEAD + h:N_HEAD + h + 1, :]
        ig_row = zt[h:h + 1, :]
        ig_col = gb[:, h:h + 1]
        m_prev = m_row[:, h:h + 1]
        log_d = jnp.where(causal, f_col - f_row + ig_row, -jnp.inf)
        log_prev = f_col + m_prev
        m_t = jnp.maximum(log_prev, jnp.max(log_d, axis=1, keepdims=True))
        dm = jnp.exp(log_d - m_t)
        prev_scale = jnp.exp(log_prev - m_t)

        sl = slice(h * DH, (h + 1) * DH)
        qh = q_ref[:, sl]
        kh = k_ref[:, sl] * scale
        vh = v_ref[:, sl]
        qb = qh.astype(BF16)
        vb = vh.astype(BF16)
        c_h = c_ref[0, h]
        n_h = n_ref[0, h:h + 1, :]
        qk = _bdot_nt(qb, kh) * dm
        num = _bdot(qk, vb) + prev_scale * _bdot(qb, c_h)
        den = (jnp.sum(qk, axis=1, keepdims=True)
               + prev_scale * jnp.sum(qh * n_h, axis=1, keepdims=True))
        hh = num / jnp.maximum(jnp.abs(den), jnp.exp(-m_t))
        tok_ref[:, sl] = _sigmoid(o_ref[:, sl]) * hh

        m_last = m_t[L - 1:L, :]
        f_last = f_col[L - 1:L, :]
        w_src = jnp.exp(f_last + ig_col - f_col - m_last)
        c_scale = jnp.exp(f_last + m_prev - m_last)
        kw = kh * w_src
        c_ref[0, h] = c_scale * c_h + _bdot(kw.T, vb)
        n_ref[0, h:h + 1, :] = c_scale * n_h + jnp.sum(kw, axis=0, keepdims=True)
        m_new_row = jnp.where(lane1 == h, m_last, m_new_row)
    m_ref[0] = m_new_row


def _mlstm_prompt(proj, bias_row, batch, seq):
    nc = seq // CHUNK
    m_tok = batch * seq

    def col_block(j):
        return pl.BlockSpec((CHUNK, TOK_W), lambda b, c: (b * nc + c, j))

    return pl.pallas_call(
        _mlstm_chunk_kernel,
        grid=(batch, nc),
        in_specs=[col_block(0), col_block(1), col_block(2), col_block(3),
                  pl.BlockSpec((CHUNK, LANES), lambda b, c: (b * nc + c, GATE_COL // LANES)),
                  pl.BlockSpec((1, LANES), lambda b, c: (0, 0))],
        out_specs=[pl.BlockSpec((CHUNK, TOK_W), lambda b, c: (b * nc + c, 0)),
                   pl.BlockSpec((1, N_HEAD, DH, DH), lambda b, c: (b, 0, 0, 0)),
                   pl.BlockSpec((1, N_HEAD, DH), lambda b, c: (b, 0, 0)),
                   pl.BlockSpec((1, 1, LANES), lambda b, c: (b, 0, 0))],
        out_shape=[jax.ShapeDtypeStruct((m_tok, TOK_W), F32),
                   jax.ShapeDtypeStruct((batch, N_HEAD, DH, DH), F32),
                   jax.ShapeDtypeStruct((batch, N_HEAD, DH), F32),
                   jax.ShapeDtypeStruct((batch, 1, LANES), F32)],
        compiler_params=_cparams(("arbitrary", "arbitrary")),
        name="mlstm_chunk",
    )(proj, proj, proj, proj, proj, bias_row)


def _mlstm_step_kernel(q_ref, k_ref, v_ref, o_ref, g_ref, bias_ref, c_ref, n_ref, m_ref,
                       tok_ref, c_out, n_out, m_out):
    gb = g_ref[0] + bias_ref[...]
    m_row = m_ref[0]
    lane1 = lax.broadcasted_iota(jnp.int32, (1, LANES), 1)
    row = lax.broadcasted_iota(jnp.int32, (DH, DH), 0)
    col = lax.broadcasted_iota(jnp.int32, (DH, DH), 1)
    eye = row == col
    m_new_row = m_row
    scale = DH ** -0.5
    for h in range(N_HEAD):
        ig = gb[:, h:h + 1]
        lf = _log_sigmoid(gb[:, N_HEAD + h:N_HEAD + h + 1])
        m_prev = m_row[:, h:h + 1]
        log_prev = lf + m_prev
        m_t = jnp.maximum(log_prev, ig)
        dm = jnp.exp(ig - m_t)
        prev_scale = jnp.exp(log_prev - m_t)

        sl = slice(h * DH, (h + 1) * DH)
        q_row = q_ref[0][:, sl]
        k_row = k_ref[0][:, sl] * scale
        v_row = v_ref[0][:, sl]
        c_h = c_ref[0, 0, h]
        n_h = n_ref[0, 0, h:h + 1, :]
        qk = jnp.sum(q_row * k_row, axis=1, keepdims=True) * dm
        q_c = _dot_hp(jnp.broadcast_to(q_row, (16, DH)), c_h)[0:1, :]
        num = qk * v_row + prev_scale * q_c
        den = qk + prev_scale * jnp.sum(q_row * n_h, axis=1, keepdims=True)
        hh = num / jnp.maximum(jnp.abs(den), jnp.exp(-m_t))
        tok_ref[0, :, sl] = _sigmoid(o_ref[0][:, sl]) * hh

        k_col = jnp.sum(jnp.where(eye, jnp.broadcast_to(k_row, (DH, DH)), 0.0), axis=1, keepdims=True)
        c_out[0, h] = prev_scale * c_h + (dm * k_col) * v_row
        n_out[0, h:h + 1, :] = prev_scale * n_h + dm * k_row
        m_new_row = jnp.where(lane1 == h, m_t, m_new_row)
    m_out[0] = m_new_row


def _mlstm_sample(proj3, bias_row, c_all, n_all, m_pad, li):
    batch = proj3.shape[0]

    def col_block(j):
        return pl.BlockSpec((1, 1, TOK_W), lambda b: (b, 0, j))

    return pl.pallas_call(
        _mlstm_step_kernel,
        grid=(batch,),
        in_specs=[col_block(0), col_block(1), col_block(2), col_block(3),
                  pl.BlockSpec((1, 1, LANES), lambda b: (b, 0, GATE_COL // LANES)),
                  pl.BlockSpec((1, LANES), lambda b: (0, 0)),
                  pl.BlockSpec((1, 1, N_HEAD, DH, DH), lambda b: (li, b, 0, 0, 0)),
                  pl.BlockSpec((1, 1, N_HEAD, DH), lambda b: (li, b, 0, 0)),
                  pl.BlockSpec((1, 1, LANES), lambda b: (b, 0, 0))],
        out_specs=[pl.BlockSpec((1, 1, TOK_W), lambda b: (b, 0, 0)),
                   pl.BlockSpec((1, N_HEAD, DH, DH), lambda b: (b, 0, 0, 0)),
                   pl.BlockSpec((1, N_HEAD, DH), lambda b: (b, 0, 0)),
                   pl.BlockSpec((1, 1, LANES), lambda b: (b, 0, 0))],
        out_shape=[jax.ShapeDtypeStruct((batch, 1, TOK_W), F32),
                   jax.ShapeDtypeStruct((batch, N_HEAD, DH, DH), F32),
                   jax.ShapeDtypeStruct((batch, N_HEAD, DH), F32),
                   jax.ShapeDtypeStruct((batch, 1, LANES), F32)],
        compiler_params=_cparams(("arbitrary",)),
        name="mlstm_step",
    )(proj3, proj3, proj3, proj3, proj3, bias_row, c_all, n_all, m_pad)


def _mem_attn_kernel(q_ref, k_ref, v_ref, o_ref):
    outs = []
    for h in range(H_M):
        sl = slice(h * DH_M, (h + 1) * DH_M)
        s = _bdot_nt(q_ref[:, sl], k_ref[0][:, sl]) * (DH_M ** -0.5)
        e = jnp.exp(s - jnp.max(s, axis=1, keepdims=True))
        p = e / jnp.sum(e, axis=1, keepdims=True)
        outs.append(_bdot(p, v_ref[0][:, sl]))
    o_ref[...] = jnp.concatenate(outs, axis=-1)


def _mem_attn_prompt(proj, q_col_block, mem_k, mem_v, seq, tq):
    m_tok = proj.shape[0]
    per_b = seq // tq
    return pl.pallas_call(
        _mem_attn_kernel,
        grid=(m_tok // tq,),
        in_specs=[pl.BlockSpec((tq, MEM_W), lambda i: (i, q_col_block)),
                  pl.BlockSpec((1, N_MEM, MEM_W), lambda i: (i // per_b, 0, 0)),
                  pl.BlockSpec((1, N_MEM, MEM_W), lambda i: (i // per_b, 0, 0))],
        out_specs=pl.BlockSpec((tq, MEM_W), lambda i: (i, 0)),
        out_shape=jax.ShapeDtypeStruct((m_tok, MEM_W), F32),
        compiler_params=_cparams(("arbitrary",)),
        name="mem_attn",
    )(proj, mem_k, mem_v)


def _mem_attn_step_kernel(q_ref, k_ref, v_ref, o_ref):
    rows = 16
    rid = lax.broadcasted_iota(jnp.int32, (rows, MEM_W), 0)
    lane = lax.broadcasted_iota(jnp.int32, (rows, MEM_W), 1)
    head_mask = (lane // DH_M) == rid
    q = jnp.broadcast_to(q_ref[0], (rows, MEM_W))
    qm = jnp.where(head_mask, q, 0.0)
    s = _dot_hp(qm, k_ref[0, 0]) * (DH_M ** -0.5)
    e = jnp.exp(s - jnp.max(s, axis=1, keepdims=True))
    p = e / jnp.sum(e, axis=1, keepdims=True)
    o = _dot_hp(p, v_ref[0, 0], nt=True)
    o_ref[0] = jnp.sum(jnp.where(head_mask, o, 0.0), axis=0, keepdims=True)


def _mem_attn_sample(mq3, cache_k, cache_v, li):
    batch = mq3.shape[0]
    return pl.pallas_call(
        _mem_attn_step_kernel,
        grid=(batch,),
        in_specs=[pl.BlockSpec((1, 1, MEM_W), lambda b: (b, 0, 0)),
                  pl.BlockSpec((1, 1, N_MEM, MEM_W), lambda b: (li, b, 0, 0)),
                  pl.BlockSpec((1, 1, N_MEM, MEM_W), lambda b: (li, b, 0, 0))],
        out_specs=pl.BlockSpec((1, 1, MEM_W), lambda b: (b, 0, 0)),
        out_shape=jax.ShapeDtypeStruct((batch, 1, MEM_W), F32),
        compiler_params=_cparams(("arbitrary",)),
        name="mem_attn_step",
    )(mq3, cache_k, cache_v)


def _routing(logits):
    lane = lax.broadcasted_iota(jnp.int32, logits.shape, 1)
    lane_f = lane.astype(F32)
    big = 1000.0
    is_g = (lane >= N_EXPERTS) & (lane < N_EXPERTS + 4)
    lg = jnp.where(is_g, logits, -jnp.inf)
    gmax = jnp.max(lg, axis=1, keepdims=True)
    gidx = jnp.min(jnp.where(lg == gmax, lane_f, big), axis=1, keepdims=True) - float(N_EXPERTS)
    p_g = 1.0 / jnp.sum(jnp.exp(lg - gmax), axis=1, keepdims=True)
    in_grp = (lane < N_EXPERTS) & ((lane >> 2).astype(F32) == gidx)
    le = jnp.where(in_grp, logits, -jnp.inf)
    v1 = jnp.max(le, axis=1, keepdims=True)
    i1 = jnp.min(jnp.where(le == v1, lane_f, big), axis=1, keepdims=True)
    le2 = jnp.where(lane_f == i1, -jnp.inf, le)
    v2 = jnp.max(le2, axis=1, keepdims=True)
    i2 = jnp.min(jnp.where(le2 == v2, lane_f, big), axis=1, keepdims=True)
    e2 = jnp.exp(v2 - v1)
    inv = 1.0 / (1.0 + e2)
    comb = jnp.where(lane_f == i1, inv * p_g, jnp.where(lane_f == i2, e2 * inv * p_g, 0.0))
    return comb, gidx


GROUP_LANE = 16
RANK_LANE = 17


def _mix_kernel(x_ref, tok_ref, mem_ref, wt_ref, wm_ref, g_ref, b_ref, wr_ref, br_ref,
                x1_ref, comb_ref, cnt_ref, *, hp):
    @pl.when(pl.program_id(0) == 0)
    def _():
        cnt_ref[...] = jnp.zeros_like(cnt_ref)

    dot = _dot_hp if hp else _bdot
    mix = dot(tok_ref[...], wt_ref[...]) + dot(mem_ref[...], wm_ref[...])
    x1 = _layer_norm(ALPHA * x_ref[...] + mix, g_ref[...], b_ref[...])
    x1_ref[...] = x1
    logits = dot(x1, wr_ref[...]) + br_ref[...]
    comb, gidx = _routing(logits)
    tm = comb.shape[0]
    lane_f = lax.broadcasted_iota(jnp.int32, comb.shape, 1).astype(F32)
    mine = lane_f == gidx
    onehot = jnp.where(mine, 1.0, 0.0)
    row = lax.broadcasted_iota(jnp.int32, (tm, tm), 0)
    col = lax.broadcasted_iota(jnp.int32, (tm, tm), 1)
    before = jnp.where(col < row, 1.0, 0.0).astype(BF16)
    prefix = jnp.dot(before, onehot.astype(BF16), preferred_element_type=F32)
    cnt = cnt_ref[...]
    rank = jnp.sum(jnp.where(mine, prefix + cnt, 0.0), axis=1, keepdims=True)
    cnt_ref[...] = cnt + jnp.sum(onehot, axis=0, keepdims=True)
    comb_ref[...] = jnp.where(lane_f == float(GROUP_LANE), gidx,
                              jnp.where(lane_f == float(RANK_LANE), rank, comb))


def _mix(x, tok, mem, wt, wm, g, b, wr, br, tm, hp=False):
    m_tok = x.shape[0]
    const = lambda i: (0, 0)
    rows = lambda i: (i, 0)
    return pl.pallas_call(
        functools.partial(_mix_kernel, hp=hp),
        grid=(m_tok // tm,),
        in_specs=[pl.BlockSpec((tm, D_MODEL), rows),
                  pl.BlockSpec((tm, TOK_W), rows),
                  pl.BlockSpec((tm, MEM_W), rows),
                  pl.BlockSpec((TOK_W, D_MODEL), const),
                  pl.BlockSpec((MEM_W, D_MODEL), const),
                  pl.BlockSpec((1, D_MODEL), const),
                  pl.BlockSpec((1, D_MODEL), const),
                  pl.BlockSpec((D_MODEL, LANES), const),
                  pl.BlockSpec((1, LANES), const)],
        out_specs=[pl.BlockSpec((tm, D_MODEL), rows),
                   pl.BlockSpec((tm, LANES), rows),
                   pl.BlockSpec((1, LANES), const)],
        out_shape=[jax.ShapeDtypeStruct((m_tok, D_MODEL), F32),
                   jax.ShapeDtypeStruct((m_tok, LANES), F32),
                   jax.ShapeDtypeStruct((1, LANES), F32)],
        compiler_params=_cparams(("arbitrary",)),
        name="mix",
    )(x, tok, mem, wt, wm, g, b, wr, br)


def _moe_kernel(x_ref, comb_ref, w1_ref, w3_ref, w2_ref, g_ref, b_ref, o_ref, xb_ref, acc_ref,
                *, hp):
    e = pl.program_id(1)

    @pl.when(e == 0)
    def _():
        xb_ref[...] = x_ref[...].astype(BF16)
        acc_ref[...] = jnp.zeros_like(acc_ref)

    if hp:
        x = x_ref[...]
        a = _dot_hp(x, w1_ref[0, 0])
        b = _dot_hp(x, w3_ref[0, 0])
    else:
        xb = xb_ref[...]
        a = jnp.dot(xb, w1_ref[0, 0].astype(BF16), preferred_element_type=F32)
        b = jnp.dot(xb, w3_ref[0, 0].astype(BF16), preferred_element_type=F32)
    comb = comb_ref[...]
    lane = lax.broadcasted_iota(jnp.int32, comb.shape, 1)
    ce = jnp.sum(jnp.where(lane == e, comb, 0.0), axis=1, keepdims=True)
    hid = (a * _sigmoid(a)) * b * ce
    if hp:
        acc_ref[...] += _dot_hp(hid, w2_ref[0, 0])
    else:
        acc_ref[...] += jnp.dot(hid.astype(BF16), w2_ref[0, 0].astype(BF16),
                                preferred_element_type=F32)

    @pl.when(e == N_EXPERTS - 1)
    def _():
        o_ref[...] = _layer_norm(ALPHA * x_ref[...] + acc_ref[...], g_ref[...], b_ref[...])


def _moe(x1, comb, w1, w3, w2, g, b, li, tm, hp=False):
    m_tok = x1.shape[0]
    rows = lambda i, e: (i, 0)
    const = lambda i, e: (0, 0)
    return pl.pallas_call(
        functools.partial(_moe_kernel, hp=hp),
        grid=(m_tok // tm, N_EXPERTS),
        in_specs=[pl.BlockSpec((tm, D_MODEL), rows),
                  pl.BlockSpec((tm, LANES), rows),
                  pl.BlockSpec((1, 1, D_MODEL, D_EXPERT), lambda i, e: (li, e, 0, 0)),
                  pl.BlockSpec((1, 1, D_MODEL, D_EXPERT), lambda i, e: (li, e, 0, 0)),
                  pl.BlockSpec((1, 1, D_EXPERT, D_MODEL), lambda i, e: (li, e, 0, 0)),
                  pl.BlockSpec((1, D_MODEL), const),
                  pl.BlockSpec((1, D_MODEL), const)],
        out_specs=pl.BlockSpec((tm, D_MODEL), rows),
        out_shape=jax.ShapeDtypeStruct((m_tok, D_MODEL), F32),
        scratch_shapes=[pltpu.VMEM((tm, D_MODEL), BF16), pltpu.VMEM((tm, D_MODEL), F32)],
        compiler_params=_cparams(("arbitrary", "arbitrary")),
        name="moe",
    )(x1, comb, w1, w3, w2, g, b)


SLOT_TILE = 256
N_GROUPS = 4
EXP_PER_GROUP = 4
MOE_VMEM_LIMIT = 56 * 1024 * 1024


def _dispatch_kernel(dest_ref, fill_ref, x_ref, xs_ref, inv_ref, zrow, sem, sem_fill,
                     *, tm, n_steps, n_fill):
    i = pl.program_id(0)
    base = i * tm

    def body(r, carry):
        d = dest_ref[base + r]
        pltpu.make_async_copy(x_ref.at[pl.ds(base + r, 1)], xs_ref.at[pl.ds(d, 1)], sem).start()
        inv_ref[d] = base + r
        return carry

    lax.fori_loop(0, tm, body, 0, unroll=8)

    def wait_step():
        pltpu.make_async_copy(x_ref.at[pl.ds(0, tm)], xs_ref.at[pl.ds(0, tm)], sem).wait()

    @pl.when(i > 0)
    def _():
        wait_step()

    @pl.when(i == n_steps - 1)
    def _():
        wait_step()
        zrow[...] = jnp.zeros_like(zrow)

        def fill(s, carry):
            pltpu.make_async_copy(zrow.at[pl.ds(0, 1)], xs_ref.at[pl.ds(s, 1)], sem_fill).start()
            inv_ref[s] = -1
            return carry

        for k in range(N_GROUPS + 1):
            lax.fori_loop(fill_ref[2 * k], fill_ref[2 * k + 1], fill, 0)
        pltpu.make_async_copy(xs_ref.at[pl.ds(0, n_fill)], xs_ref.at[pl.ds(0, n_fill)],
                              sem_fill).wait()


def _dispatch(dest, fill, x1, n_slots, tm):
    m_tok = x1.shape[0]
    n_steps = m_tok // tm
    grid_spec = pltpu.PrefetchScalarGridSpec(
        num_scalar_prefetch=2,
        grid=(n_steps,),
        in_specs=[pl.BlockSpec(memory_space=pl.ANY)],
        out_specs=[pl.BlockSpec(memory_space=pl.ANY),
                   pl.BlockSpec(memory_space=pltpu.SMEM)],
        scratch_shapes=[pltpu.VMEM((8, D_MODEL), F32),
                        pltpu.SemaphoreType.DMA(()),
                        pltpu.SemaphoreType.DMA(())],
    )
    return pl.pallas_call(
        functools.partial(_dispatch_kernel, tm=tm, n_steps=n_steps, n_fill=n_slots - m_tok),
        grid_spec=grid_spec,
        out_shape=[jax.ShapeDtypeStruct((n_slots, D_MODEL), F32),
                   jax.ShapeDtypeStruct((n_slots,), jnp.int32)],
        compiler_params=_cparams(("arbitrary",)),
        name="moe_dispatch",
    )(dest, fill, x1)


def _moe_sparse_kernel(blk_ref, grp_ref, val_ref, nrow_ref, inv_ref,
                       xs_ref, w1_ref, w3_ref, w2_ref, wr_ref, br_ref, g_ref, b_ref,
                       out_ref, w1b, w3b, w2b, ybuf, junk, state, sem, *, n_tiles):
    i = pl.program_id(0)
    ts = SLOT_TILE

    @pl.when(i == 0)
    def _():
        state[0] = -1
        state[1] = 0

    def wait_pending():
        @pl.when(state[1] == 1)
        def _():
            pltpu.make_async_copy(ybuf, junk, sem).wait()

    @pl.when(val_ref[i] == 1)
    def _():
        grp = grp_ref[i]

        @pl.when(grp != state[0])
        def _():
            for j in range(EXP_PER_GROUP):
                cols = slice(j * D_EXPERT, (j + 1) * D_EXPERT)
                w1b[:, cols] = w1_ref[0, 0, j].astype(BF16)
                w3b[:, cols] = w3_ref[0, 0, j].astype(BF16)
                w2b[cols, :] = w2_ref[0, 0, j].astype(BF16)
            state[0] = grp

        x = xs_ref[...]
        xb = x.astype(BF16)
        logits = jnp.dot(xb, wr_ref[...], preferred_element_type=F32) + br_ref[...]
        comb, _ = _routing(logits)
        lane = lax.broadcasted_iota(jnp.int32, comb.shape, 1)
        a = jnp.dot(xb, w1b[...], preferred_element_type=F32)
        b = jnp.dot(xb, w3b[...], preferred_element_type=F32)
        hid = []
        for j in range(EXP_PER_GROUP):
            cols = slice(j * D_EXPERT, (j + 1) * D_EXPERT)
            ce = jnp.sum(jnp.where(lane == grp * EXP_PER_GROUP + j, comb, 0.0), axis=1, keepdims=True)
            a_j = a[:, cols]
            hid.append(((a_j * _sigmoid(a_j)) * b[:, cols] * ce).astype(BF16))
        acc = jnp.dot(jnp.concatenate(hid, axis=1), w2b[...], preferred_element_type=F32)
        x2 = _layer_norm(ALPHA * x + acc, g_ref[...], b_ref[...])

        wait_pending()
        ybuf[...] = x2
        n = nrow_ref[i]
        base = blk_ref[i] * ts

        def body(r, carry):
            t = inv_ref[base + r]
            pltpu.make_async_copy(ybuf.at[pl.ds(r, 1)], out_ref.at[pl.ds(t, 1)], sem).start()
            return carry

        def pad_body(r, carry):
            pltpu.make_async_copy(ybuf.at[pl.ds(r, 1)], junk.at[pl.ds(r, 1)], sem).start()
            return carry

        full = n // 8

        def body8(gi, carry):
            for u in range(8):
                body(gi * 8 + u, carry)
            return carry

        lax.fori_loop(0, full, body8, 0)
        lax.fori_loop(full * 8, n, body, 0)
        lax.fori_loop(n, ts, pad_body, 0)
        state[1] = 1

    @pl.when(i == n_tiles - 1)
    def _():
        wait_pending()
        state[1] = 0


def _moe_sparse(x1, comb, cnt, P, W, li):
    m_tok = x1.shape[0]
    ts = SLOT_TILE
    n_tiles = m_tok // ts + N_GROUPS
    n_slots = n_tiles * ts
    grp_tok = comb[:, GROUP_LANE].astype(jnp.int32)
    rank_tok = comb[:, RANK_LANE].astype(jnp.int32)
    counts = cnt[0, :N_GROUPS].astype(jnp.int32)
    tiles_g = (counts + ts - 1) // ts
    tile_end = jnp.cumsum(tiles_g)
    slot_base = (tile_end - tiles_g) * ts
    dest = jnp.take(slot_base, grp_tok) + rank_tok
    total = tile_end[-1]
    tile_i = jnp.arange(n_tiles, dtype=jnp.int32)
    valid = (tile_i < total).astype(jnp.int32)
    blk = jnp.minimum(tile_i, total - 1)
    grp_tile = jnp.minimum(jnp.sum((tile_end[None, :] <= blk[:, None]).astype(jnp.int32), axis=1),
                           N_GROUPS - 1)
    n_rows = jnp.clip(jnp.take(slot_base + counts, grp_tile) - blk * ts, 0, ts) * valid

    fill_lo = jnp.concatenate([slot_base + counts, (total * ts)[None]])
    fill_hi = jnp.concatenate([slot_base + tiles_g * ts, jnp.full((1,), n_slots, jnp.int32)])
    fill = jnp.stack([fill_lo, fill_hi], axis=1).reshape(-1).astype(jnp.int32)

    xs, inv = _dispatch(dest, fill, x1, n_slots, 512)

    w1 = P['w_e1'].reshape(DEPTH, N_GROUPS, EXP_PER_GROUP, D_MODEL, D_EXPERT)
    w3 = P['w_e3'].reshape(DEPTH, N_GROUPS, EXP_PER_GROUP, D_MODEL, D_EXPERT)
    w2 = P['w_e2'].reshape(DEPTH, N_GROUPS, EXP_PER_GROUP, D_EXPERT, D_MODEL)
    const = lambda i, *_: (0, 0)
    grid_spec = pltpu.PrefetchScalarGridSpec(
        num_scalar_prefetch=5,
        grid=(n_tiles,),
        in_specs=[pl.BlockSpec((ts, D_MODEL), lambda i, blk, grp, val, nrow, inv: (blk[i], 0)),
                  pl.BlockSpec((1, 1, EXP_PER_GROUP, D_MODEL, D_EXPERT),
                               lambda i, blk, grp, val, nrow, inv: (li, grp[i], 0, 0, 0)),
                  pl.BlockSpec((1, 1, EXP_PER_GROUP, D_MODEL, D_EXPERT),
                               lambda i, blk, grp, val, nrow, inv: (li, grp[i], 0, 0, 0)),
                  pl.BlockSpec((1, 1, EXP_PER_GROUP, D_EXPERT, D_MODEL),
                               lambda i, blk, grp, val, nrow, inv: (li, grp[i], 0, 0, 0)),
                  pl.BlockSpec((D_MODEL, LANES), const),
                  pl.BlockSpec((1, LANES), const),
                  pl.BlockSpec((1, D_MODEL), const),
                  pl.BlockSpec((1, D_MODEL), const)],
        out_specs=pl.BlockSpec(memory_space=pl.ANY),
        scratch_shapes=[pltpu.VMEM((D_MODEL, EXP_PER_GROUP * D_EXPERT), BF16),
                        pltpu.VMEM((D_MODEL, EXP_PER_GROUP * D_EXPERT), BF16),
                        pltpu.VMEM((EXP_PER_GROUP * D_EXPERT, D_MODEL), BF16),
                        pltpu.VMEM((ts, D_MODEL), F32),
                        pltpu.VMEM((ts, D_MODEL), F32),
                        pltpu.SMEM((2,), jnp.int32),
                        pltpu.SemaphoreType.DMA(())],
    )
    return pl.pallas_call(
        functools.partial(_moe_sparse_kernel, n_tiles=n_tiles),
        grid_spec=grid_spec,
        out_shape=jax.ShapeDtypeStruct((m_tok, D_MODEL), F32),
        compiler_params=pltpu.CompilerParams(dimension_semantics=("arbitrary",),
                                             vmem_limit_bytes=MOE_VMEM_LIMIT),
        name="moe_sparse",
    )(blk, grp_tile, valid, n_rows, inv, xs, w1, w3, w2, W['w_r'][li], W['b_r'][li],
      P['ln_ffn_g'][li][None, :], P['ln_ffn_b'][li][None, :])


def _diff_lambda(lq_ref, lam_init):
    lq = lq_ref[0]
    a = jnp.sum(lq[0:1, :] * lq[1:2, :], axis=1, keepdims=True)
    b = jnp.sum(lq[2:3, :] * lq[3:4, :], axis=1, keepdims=True)
    return jnp.exp(a) - jnp.exp(b) + lam_init


def _sub_norm(o, g_row, lam_init):
    o = o * lax.rsqrt(jnp.mean(o * o, axis=-1, keepdims=True) + LN_EPS) * g_row
    return o * (1.0 - lam_init)


def _diff_attn_kernel(q1_ref, q2_ref, k1_ref, k2_ref, vt_ref, lq_ref, gs_ref, o_ref,
                      m_scr, l_scr, acc_scr, *, tq, lam_init):
    i = pl.program_id(2)
    lam = _diff_lambda(lq_ref, lam_init)
    q_scale = (DQK ** -0.5) * math.log2(math.e)
    qs = [(q1_ref[...] * q_scale).astype(BF16), (q2_ref[...] * q_scale).astype(BF16)]
    k_refs = [k1_ref, k2_ref]
    m_scr[...] = jnp.full_like(m_scr, -jnp.inf)
    l_scr[...] = jnp.zeros_like(l_scr)
    acc_scr[...] = jnp.zeros_like(acc_scr)
    key_i = lax.broadcasted_iota(jnp.int32, (tq, tq), 0)
    qry_i = lax.broadcasted_iota(jnp.int32, (tq, tq), 1)
    causal = key_i <= qry_i

    def block(j, masked):
        start = pl.multiple_of(j * tq, tq)
        sts = []
        for hh in range(2):
            for mm in range(2):
                kb = k_refs[mm][pl.ds(start, tq), hh * DQK:(hh + 1) * DQK]
                st = lax.dot_general(kb, qs[mm][:, hh * DQK:(hh + 1) * DQK],
                                     (((1,), (1,)), ((), ())), preferred_element_type=F32)
                if masked:
                    st = jnp.where(causal, st, -jnp.inf)
                sts.append(st)
        m_all = m_scr[...]
        l_all = l_scr[...]
        ps, alphas, m_rows, l_rows = [], [], [], []
        for idx in range(4):
            m_old = m_all[idx:idx + 1, :]
            m_new = jnp.maximum(m_old, jnp.max(sts[idx], axis=0, keepdims=True))
            alpha = jnp.exp2(m_old - m_new)
            p = jnp.exp2(sts[idx] - m_new)
            l_rows.append(alpha * l_all[idx:idx + 1, :] + jnp.sum(p, axis=0, keepdims=True))
            m_rows.append(m_new)
            alphas.append(alpha)
            ps.append(p.astype(BF16))
        m_scr[...] = jnp.concatenate(m_rows + [m_all[4:, :]], axis=0)
        l_scr[...] = jnp.concatenate(l_rows + [l_all[4:, :]], axis=0)
        for idx in range(4):
            vt = vt_ref[j, (idx // 2) * DH:(idx // 2 + 1) * DH, :]
            acc_scr[idx] = alphas[idx] * acc_scr[idx] + jnp.dot(vt, ps[idx],
                                                                preferred_element_type=F32)

    def body(j, carry):
        block(j, False)
        return carry

    lax.fori_loop(0, i, body, 0)
    block(i, True)

    for hh in range(2):
        o1 = acc_scr[hh * 2] / l_scr[hh * 2:hh * 2 + 1, :]
        o2 = acc_scr[hh * 2 + 1] / l_scr[hh * 2 + 1:hh * 2 + 2, :]
        o = o1 - lam * o2
        o = o * lax.rsqrt(jnp.mean(o * o, axis=0, keepdims=True) + LN_EPS)
        o_ref[:, hh * DH:(hh + 1) * DH] = o.T * gs_ref[...] * (1.0 - lam_init)


def _diff_attn_prompt(proj_b, k_b, vt_b, lam_qk, g_sub, j, lam_init, batch, seq, tq):
    m_tok = batch * seq
    nq = seq // tq
    kern = functools.partial(_diff_attn_kernel, tq=tq, lam_init=lam_init)
    return pl.pallas_call(
        kern,
        grid=(batch, 3, nq),
        in_specs=[pl.BlockSpec((tq, 2 * DQK), lambda b, hp, i: (b * nq + i, hp)),
                  pl.BlockSpec((tq, 2 * DQK), lambda b, hp, i: (b * nq + i, 3 + hp)),
                  pl.BlockSpec((seq, 2 * DQK), lambda b, hp, i: (b, hp)),
                  pl.BlockSpec((seq, 2 * DQK), lambda b, hp, i: (b, 3 + hp)),
                  pl.BlockSpec((nq, 2 * DH, tq), lambda b, hp, i: (b, hp, 0)),
                  pl.BlockSpec((1, 4, DQK), lambda b, hp, i: (j, 0, 0)),
                  pl.BlockSpec((1, DH), lambda b, hp, i: (0, 0))],
        out_specs=pl.BlockSpec((tq, 2 * DH), lambda b, hp, i: (b * nq + i, hp)),
        out_shape=jax.ShapeDtypeStruct((m_tok, TOK_W), F32),
        scratch_shapes=[pltpu.VMEM((8, tq), F32), pltpu.VMEM((8, tq), F32),
                        pltpu.VMEM((4, DH, tq), F32)],
        compiler_params=_cparams(("arbitrary", "arbitrary", "arbitrary")),
        name="diff_attn",
    )(proj_b, proj_b, k_b, k_b, vt_b, lam_qk, g_sub)


PAGES_PER_STEP = 8


def _diff_attn_step_kernel(pt_ref, q_ref, kn_ref, vn_ref, *rest, n_steps, lam_init):
    k_refs = rest[:PAGES_PER_STEP]
    v_refs = rest[PAGES_PER_STEP:2 * PAGES_PER_STEP]
    lq_ref, gs_ref, o_ref, qm_scr, m_scr, l_scr, acc_scr = rest[2 * PAGES_PER_STEP:]
    p_idx = pl.program_id(1)
    rows = 16
    rid = lax.broadcasted_iota(jnp.int32, (rows, TOK_W), 0)
    lane = lax.broadcasted_iota(jnp.int32, (rows, TOK_W), 1)

    @pl.when(p_idx == 0)
    def _():
        q = jnp.broadcast_to(q_ref[0] * (DQK ** -0.5), (rows, TOK_W))
        qm_scr[...] = jnp.where((lane // DQK) == rid, q, 0.0)
        m_scr[...] = jnp.full_like(m_scr, -jnp.inf)
        l_scr[...] = jnp.zeros_like(l_scr)
        acc_scr[...] = jnp.zeros_like(acc_scr)

    qm = qm_scr[...]
    q_hi, q_lo = _split2(qm)
    q2 = jnp.concatenate([q_hi, q_lo], axis=0)
    s_pages = []
    for k_ref in k_refs:
        k_hi, k_lo = _split2(k_ref[0])
        r1 = _bdot(q2, k_hi)
        s_pages.append((_bdot(q_hi, k_lo) + r1[rows:]) + r1[:rows])
    s = jnp.concatenate(s_pages, axis=1)
    m_old = m_scr[...]
    m_new = jnp.maximum(m_old, jnp.max(s, axis=1, keepdims=True))
    alpha = jnp.exp(m_old - m_new)
    p = jnp.exp(s - m_new)
    p_hi, p_lo = _split2(p)
    p2 = jnp.concatenate([p_hi, p_lo], axis=0)
    pv = []
    for h in range(N_HEAD):
        pv_h = None
        for g in range(PAGES_PER_STEP):
            cols = slice(g * PAGE, (g + 1) * PAGE)
            v_hi, v_lo = _split2(v_refs[g][0, h])
            r1 = _bdot(p2[:, cols], v_hi)
            term = (_bdot(p_hi[:, cols], v_lo) + r1[rows:]) + r1[:rows]
            pv_h = term if pv_h is None else pv_h + term
        pv.append(pv_h)
    l_scr[...] = alpha * l_scr[...] + jnp.sum(p, axis=1, keepdims=True)
    acc_scr[...] = alpha * acc_scr[...] + jnp.concatenate(pv, axis=1)
    m_scr[...] = m_new

    @pl.when(p_idx == n_steps - 1)
    def _():
        lam = _diff_lambda(lq_ref, lam_init)
        s_new = jnp.sum(qm * kn_ref[0], axis=1, keepdims=True)
        m_old2 = m_scr[...]
        m_fin = jnp.maximum(m_old2, s_new)
        a2 = jnp.exp(m_old2 - m_fin)
        p_new = jnp.exp(s_new - m_fin)
        l_fin = a2 * l_scr[...] + p_new
        acc = a2 * acc_scr[...] + p_new * vn_ref[0]
        r = acc / l_fin
        head_of_lane = lane // DH
        o1 = jnp.sum(jnp.where(head_of_lane == rid, r, 0.0), axis=0, keepdims=True)
        o2 = jnp.sum(jnp.where(head_of_lane == rid - N_HEAD, r, 0.0), axis=0, keepdims=True)
        o = o1 - lam * o2
        for h in range(N_HEAD):
            sl = slice(h * DH, (h + 1) * DH)
            o_ref[0, :, sl] = _sub_norm(o[:, sl], gs_ref[...], lam_init)


def _diff_attn_sample(page_table, q3, kn3, vn3, cache_k, cache_v, lam_qk, g_sub, j, lam_init):
    batch, n_pages = page_table.shape
    n_steps = n_pages // PAGES_PER_STEP
    kern = functools.partial(_diff_attn_step_kernel, n_steps=n_steps, lam_init=lam_init)
    row3 = lambda b, p, pt: (b, 0, 0)

    def k_spec(g):
        return pl.BlockSpec((1, TOK_W, PAGE), lambda b, p, pt: (pt[b, p * PAGES_PER_STEP + g], 0, 0))

    def v_spec(g):
        return pl.BlockSpec((1, N_HEAD, PAGE, DH),
                            lambda b, p, pt: (pt[b, p * PAGES_PER_STEP + g], 0, 0, 0))

    k_specs = [k_spec(g) for g in range(PAGES_PER_STEP)]
    v_specs = [v_spec(g) for g in range(PAGES_PER_STEP)]
    grid_spec = pltpu.PrefetchScalarGridSpec(
        num_scalar_prefetch=1,
        grid=(batch, n_steps),
        in_specs=[pl.BlockSpec((1, 1, TOK_W), row3),
                  pl.BlockSpec((1, 1, TOK_W), row3),
                  pl.BlockSpec((1, 1, TOK_W), row3)] + k_specs + v_specs + [
                  pl.BlockSpec((1, 4, DQK), lambda b, p, pt: (j, 0, 0)),
                  pl.BlockSpec((1, DH), lambda b, p, pt: (0, 0))],
        out_specs=pl.BlockSpec((1, 1, TOK_W), row3),
        scratch_shapes=[pltpu.VMEM((16, TOK_W), F32), pltpu.VMEM((16, 1), F32),
                        pltpu.VMEM((16, 1), F32), pltpu.VMEM((16, TOK_W), F32)],
    )
    return pl.pallas_call(
        kern,
        grid_spec=grid_spec,
        out_shape=jax.ShapeDtypeStruct((batch, 1, TOK_W), F32),
        compiler_params=_cparams(("arbitrary", "arbitrary")),
        name="diff_attn_step",
    )(page_table, q3, kn3, vn3, *([cache_k] * PAGES_PER_STEP), *([cache_v] * PAGES_PER_STEP),
      lam_qk, g_sub)


def _diff_lambda_init(li):
    return 0.8 - 0.6 * math.exp(-0.3 * li)


def _prep_weights(P):
    w = {}
    wa = P['w_in_a']
    gates = jnp.pad(wa[:, :, 4 * TOK_W:4 * TOK_W + 2 * N_HEAD], ((0, 0), (0, 0), (0, LANES - 2 * N_HEAD)))
    w['w_a32'] = jnp.concatenate([wa[:, :, :4 * TOK_W], wa[:, :, 4 * TOK_W + 2 * N_HEAD:], gates],
                                 axis=-1)
    w['w_a'] = w['w_a32'].astype(BF16)
    w['gate_bias'] = jnp.pad(jnp.concatenate([P['b_igate'], P['b_fgate']], axis=-1),
                             ((0, 0), (0, LANES - 2 * N_HEAD)))[:, None, :]
    w['w_b'] = P['w_in_b'].astype(BF16)
    w['w_kv'] = P['w_kv_shared'].astype(BF16)
    w['w_mem'] = jnp.transpose(P['w_mem_kv'], (1, 0, 2)).reshape(D_MODEL, DEPTH * 2 * MEM_W).astype(BF16)
    w['w_out_t32'] = P['w_out'][:, :TOK_W, :]
    w['w_out_m32'] = P['w_out'][:, TOK_W:, :]
    w['w_out_t'] = w['w_out_t32'].astype(BF16)
    w['w_out_m'] = w['w_out_m32'].astype(BF16)
    wr = jnp.concatenate([P['w_router'], P['w_group']], axis=-1)
    w['w_r32'] = jnp.pad(wr, ((0, 0), (0, 0), (0, LANES - wr.shape[-1])))
    w['w_r'] = w['w_r32'].astype(BF16)
    br = jnp.concatenate([P['b_router'], P['b_group']], axis=-1)
    w['b_r'] = jnp.pad(br, ((0, 0), (0, LANES - br.shape[-1])))[:, None, :]
    return w


def _layer_tail(x, tok, mem, li, P, W, tm_mix, tm_moe, hp=False):
    sfx = '32' if hp else ''
    x1, comb, cnt = _mix(x, tok, mem, W['w_out_t' + sfx][li], W['w_out_m' + sfx][li],
                         P['ln_mix_g'][li][None, :], P['ln_mix_b'][li][None, :],
                         W['w_r' + sfx][li], W['b_r'][li], tm_mix, hp)
    if hp:
        return _moe(x1, comb, P['w_e1'], P['w_e3'], P['w_e2'],
                    P['ln_ffn_g'][li][None, :], P['ln_ffn_b'][li][None, :], li, tm_moe, hp)
    return _moe_sparse(x1, comb, cnt, P, W, li)


def _prompt_trunk(x_prompt, mem_prompt, P, W):
    batch, seq, _ = x_prompt.shape
    m_tok = batch * seq
    x = x_prompt.reshape(m_tok, D_MODEL)
    mem = mem_prompt.reshape(batch * N_MEM, D_MODEL)
    kv_mem = _proj(mem, W['w_mem'], 512)
    kv_mem = jnp.transpose(kv_mem.reshape(batch, N_MEM, DEPTH, 2, MEM_W), (3, 2, 0, 1, 4))
    mem_k, mem_v = kv_mem[0], kv_mem[1]

    new_c, new_n, new_m = [], [], []
    k_t = v_h = k16 = vt16 = None
    for li in range(DEPTH):
        if li < N_A:
            proj = _proj(x, W['w_a'][li], 256)
            tok, c, n, m = _mlstm_prompt(proj, W['gate_bias'][li], batch, seq)
            new_c.append(c)
            new_n.append(n)
            new_m.append(m[:, 0, :N_HEAD])
            mq_block = (4 * TOK_W) // MEM_W
        else:
            if li == N_A:
                k_t, v_h, k16, vt16 = _proj_shared_kv(x, W['w_kv'], batch, seq, 256)
            j = li - N_A
            proj = _proj(x, W['w_b'][j], 512)
            tok = _diff_attn_prompt(proj, k16, vt16, P['lambda_qk'], P['subln_g'][j:j + 1], j,
                                    _diff_lambda_init(li), batch, seq, 256)
            mq_block = TOK_W // MEM_W
        mem_o = _mem_attn_prompt(proj, mq_block, mem_k[li], mem_v[li], seq, 512)
        x = _layer_tail(x, tok, mem_o, li, P, W, 512, 1024)
    y = x.reshape(batch, seq, D_MODEL)
    k_p = jnp.transpose(k_t.reshape(batch, 2, N_HEAD, DQK, seq), (0, 4, 1, 2, 3))
    v_p = jnp.transpose(v_h, (0, 2, 1, 3))
    shp = (DEPTH, batch, N_MEM, H_M, DH_M)
    return (y, mem_k.reshape(shp), mem_v.reshape(shp),
            jnp.stack(new_c), jnp.stack(new_n), jnp.stack(new_m), k_p, v_p)


def _sample_trunk(x_sample, cache_mem_k, cache_mem_v, state, cache_k, cache_v, page_table, P, W):
    batch = x_sample.shape[0]
    x = x_sample.reshape(batch, D_MODEL)
    c_all, n_all, m_all = state
    mem_k = jnp.transpose(cache_mem_k, (0, 1, 3, 4, 2)).reshape(DEPTH, batch, MEM_W, N_MEM)
    mem_v = jnp.transpose(cache_mem_v, (0, 1, 3, 4, 2)).reshape(DEPTH, batch, MEM_W, N_MEM)
    ck = jnp.transpose(cache_k, (0, 2, 3, 4, 1)).reshape(cache_k.shape[0], TOK_W, PAGE)
    cv = jnp.transpose(cache_v, (0, 2, 1, 3))

    new_c, new_n, new_m = [], [], []
    kv32 = None
    for li in range(DEPTH):
        if li < N_A:
            proj = _proj(x, W['w_a32'][li], batch, 384, hp=True)
            m_pad = jnp.pad(m_all[li], ((0, 0), (0, LANES - N_HEAD)))[:, None, :]
            tok3, c, n, m = _mlstm_sample(proj[:, None, :], W['gate_bias'][li], c_all, n_all, m_pad, li)
            new_c.append(c)
            new_n.append(n)
            new_m.append(m[:, 0, :N_HEAD])
            mq = proj[:, 4 * TOK_W:4 * TOK_W + MEM_W]
        else:
            if li == N_A:
                kv32 = _proj(x, P['w_kv_shared'], batch, 384, hp=True)
            j = li - N_A
            proj = _proj(x, P['w_in_b'][j], batch, 256, hp=True)
            tok3 = _diff_attn_sample(page_table, proj[:, None, :TOK_W], kv32[:, None, :TOK_W],
                                     kv32[:, None, TOK_W:], ck, cv, P['lambda_qk'],
                                     P['subln_g'][j:j + 1], j, _diff_lambda_init(li))
            mq = proj[:, TOK_W:]
        mem_o = _mem_attn_sample(mq[:, None, :], mem_k, mem_v, li)
        x = _layer_tail(x, tok3[:, 0, :], mem_o[:, 0, :], li, P, W, batch, batch, hp=True)
    y = x.reshape(batch, 1, D_MODEL)
    k_s = kv32[:, :TOK_W].reshape(batch, 1, 2, N_HEAD, DQK)
    v_s = kv32[:, TOK_W:].reshape(batch, 1, N_HEAD, DH)
    return y, jnp.stack(new_c), jnp.stack(new_n), jnp.stack(new_m), k_s, v_s


def kernel(x_prompt, x_sample, mem_prompt, cache_mem_k, cache_mem_v, state_mlstm_C, state_mlstm_n,
           state_mlstm_m, cache_k, cache_v, page_table, w_in_a, b_igate, b_fgate, w_in_b, lambda_qk,
           subln_g, w_kv_shared, w_mem_kv, w_out, ln_mix_g, ln_mix_b, ln_ffn_g, ln_ffn_b, w_group,
           b_group, w_router, b_router, w_e1, w_e3, w_e2):
    P = {'w_in_a': w_in_a, 'b_igate': b_igate, 'b_fgate': b_fgate, 'w_in_b': w_in_b,
         'lambda_qk': lambda_qk, 'subln_g': subln_g, 'w_kv_shared': w_kv_shared, 'w_mem_kv': w_mem_kv,
         'w_out': w_out, 'ln_mix_g': ln_mix_g, 'ln_mix_b': ln_mix_b, 'ln_ffn_g': ln_ffn_g,
         'ln_ffn_b': ln_ffn_b, 'w_group': w_group, 'b_group': b_group, 'w_router': w_router,
         'b_router': b_router, 'w_e1': w_e1, 'w_e3': w_e3, 'w_e2': w_e2}
    W = _prep_weights(P)
    y_p, mem_k_p, mem_v_p, c_p, n_p, m_p, k_p, v_p = _prompt_trunk(x_prompt, mem_prompt, P, W)
    y_s, c_s, n_s, m_s, k_s, v_s = _sample_trunk(
        x_sample, cache_mem_k, cache_mem_v, (state_mlstm_C, state_mlstm_n, state_mlstm_m),
        cache_k, cache_v, page_table, P, W)
    return (y_p, y_s, mem_k_p, mem_v_p, c_p, n_p, m_p, k_p, v_p, c_s, n_s, m_s, k_s, v_s)
```

```python
import functools
import math

import jax
import jax.numpy as jnp
from jax import lax
from jax.experimental import pallas as pl
from jax.experimental.pallas import tpu as pltpu

F32 = jnp.float32
BF16 = jnp.bfloat16

D_MODEL = 1024
DEPTH = 4
N_A = 2
TOK_W = 768
MEM_W = 256
N_HEAD = 6
DH = 128
DQK = 64
H_M = 4
DH_M = 64
N_MEM = 256
CHUNK = 128
N_EXPERTS = 16
D_EXPERT = 256
ALPHA = (2.0 * DEPTH) ** 0.25
LN_EPS = 1e-5
PAGE = 128
LANES = 128
GATE_COL = 3328
A_COLS = 3456
VMEM_LIMIT = 48 * 1024 * 1024


def _cparams(sem):
    return pltpu.CompilerParams(dimension_semantics=sem, vmem_limit_bytes=VMEM_LIMIT)


def _bdot(a, b):
    return jnp.dot(a.astype(BF16), b.astype(BF16), preferred_element_type=F32)


def _bdot_nt(a, b):
    return lax.dot_general(a.astype(BF16), b.astype(BF16), (((1,), (1,)), ((), ())),
                           preferred_element_type=F32)


def _split3(a):
    hi = a.astype(BF16)
    r = a - hi.astype(F32)
    mid = r.astype(BF16)
    lo = (r - mid.astype(F32)).astype(BF16)
    return hi, mid, lo


def _split2(a):
    hi = a.astype(BF16)
    return hi, (a - hi.astype(F32)).astype(BF16)


def _dot_hp(a, b, nt=False):
    m = a.shape[0]
    a_hi, a_mid, a_lo = _split3(a)
    b_hi, b_mid, b_lo = _split3(b)
    dot = _bdot_nt if nt else _bdot
    a3 = jnp.concatenate([a_hi, a_mid, a_lo], axis=0)
    r1 = dot(a3, b_hi)
    r2 = dot(a3[:2 * m], b_mid)
    r3 = dot(a_hi, b_lo)
    return ((r3 + r2[m:]) + (r1[2 * m:] + r2[:m]) + r1[m:2 * m]) + r1[:m]


def _dot_x3(a, b, nt=False):
    m = a.shape[0]
    a_hi, a_lo = _split2(a)
    b_hi, b_lo = _split2(b)
    dot = _bdot_nt if nt else _bdot
    r1 = dot(jnp.concatenate([a_hi, a_lo], axis=0), b_hi)
    return (dot(a_hi, b_lo) + r1[m:]) + r1[:m]


def _layer_norm(y, g, b):
    mu = jnp.mean(y, axis=-1, keepdims=True)
    d = y - mu
    var = jnp.mean(d * d, axis=-1, keepdims=True)
    return d * lax.rsqrt(var + LN_EPS) * g + b


def _sigmoid(x):
    return 1.0 / (1.0 + jnp.exp(-x))


def _log_sigmoid(x):
    return jnp.minimum(x, 0.0) - jnp.log1p(jnp.exp(-jnp.abs(x)))


def _proj_kernel(x_ref, w_ref, o_ref, *, hp):
    if hp:
        o_ref[...] = _dot_hp(x_ref[...], w_ref[...])
    else:
        o_ref[...] = jnp.dot(x_ref[...].astype(BF16), w_ref[...], preferred_element_type=F32)


def _proj(x, w, tm, tn=None, hp=False):
    m, k = x.shape
    n = w.shape[1]
    tn = n if tn is None else tn
    return pl.pallas_call(
        functools.partial(_proj_kernel, hp=hp),
        grid=(m // tm, n // tn),
        in_specs=[pl.BlockSpec((tm, k), lambda i, j: (i, 0)),
                  pl.BlockSpec((k, tn), lambda i, j: (0, j))],
        out_specs=pl.BlockSpec((tm, tn), lambda i, j: (i, j)),
        out_shape=jax.ShapeDtypeStruct((m, n), F32),
        compiler_params=_cparams(("arbitrary", "arbitrary")),
        name="proj_hp" if hp else "proj",
    )(x, w)


def _proj_kv_kernel(x_ref, w_ref, kt_ref, vh_ref, kb_ref, vt_ref):
    y = jnp.dot(x_ref[...].astype(BF16), w_ref[...], preferred_element_type=F32)
    k = y[:, :TOK_W]
    v = y[:, TOK_W:]
    kt_ref[0] = k.T
    for h in range(N_HEAD):
        vh_ref[0, h] = v[:, h * DH:(h + 1) * DH]
    kb_ref[...] = k.astype(BF16)
    vt_ref[0] = v.T.astype(BF16)


def _proj_shared_kv(x, w, batch, seq, tm):
    m, k = x.shape
    n = w.shape[1]
    per_b = seq // tm
    return pl.pallas_call(
        _proj_kv_kernel,
        grid=(m // tm,),
        in_specs=[pl.BlockSpec((tm, k), lambda i: (i, 0)),
                  pl.BlockSpec((k, n), lambda i: (0, 0))],
        out_specs=[pl.BlockSpec((1, TOK_W, tm), lambda i: (i // per_b, 0, i % per_b)),
                   pl.BlockSpec((1, N_HEAD, tm, DH), lambda i: (i // per_b, 0, i % per_b, 0)),
                   pl.BlockSpec((tm, TOK_W), lambda i: (i, 0)),
                   pl.BlockSpec((1, TOK_W, tm), lambda i: (i, 0, 0))],
        out_shape=[jax.ShapeDtypeStruct((batch, TOK_W, seq), F32),
                   jax.ShapeDtypeStruct((batch, N_HEAD, seq, DH), F32),
                   jax.ShapeDtypeStruct((m, TOK_W), BF16),
                   jax.ShapeDtypeStruct((m // tm, TOK_W, tm), BF16)],
        compiler_params=_cparams(("arbitrary",)),
        name="proj_kv",
    )(x, w)


def _mlstm_chunk_kernel(q_ref, k_ref, v_ref, o_ref, g_ref, bias_ref,
                        tok_ref, c_ref, n_ref, m_ref):
    @pl.when(pl.program_id(1) == 0)
    def _():
        c_ref[...] = jnp.zeros_like(c_ref)
        n_ref[...] = jnp.zeros_like(n_ref)
        m_ref[...] = jnp.zeros_like(m_ref)

    L = CHUNK
    lane = lax.broadcasted_iota(jnp.int32, (L, LANES), 1)
    row = lax.broadcasted_iota(jnp.int32, (L, L), 0)
    col = lax.broadcasted_iota(jnp.int32, (L, L), 1)
    causal = col <= row

    gb = g_ref[...] + bias_ref[...]
    lf = _log_sigmoid(gb)
    tril = causal.astype(BF16)
    hi = lf.astype(BF16)
    r1 = lf - hi.astype(F32)
    mid = r1.astype(BF16)
    lo = (r1 - mid.astype(F32)).astype(BF16)
    fcum = (jnp.dot(tril, hi, preferred_element_type=F32)
            + jnp.dot(tril, mid, preferred_element_type=F32)
            + jnp.dot(tril, lo, preferred_element_type=F32))
    z = jnp.where(lane < N_HEAD, gb, fcum)
    zt = z.T

    m_row = m_ref[0]
    lane1 = lax.broadcasted_iota(jnp.int32, (1, LANES), 1)
    m_new_row = m_row
    scale = DH ** -0.5
    for h in range(N_HEAD):
        f_col = fcum[:, N_HEAD + h:N_HEAD + h + 1]
        f_row = zt[N_HEAD + h:N_HEAD + h + 1, :]
        ig_row = zt[h:h + 1, :]
        ig_col = gb[:, h:h + 1]
        m_prev = m_row[:, h:h + 1]
        log_d = jnp.where(causal, f_col - f_row + ig_row, -jnp.inf)
        log_prev = f_col + m_prev
        m_t = jnp.maximum(log_prev, jnp.max(log_d, axis=1, keepdims=True))
        dm = jnp.exp(log_d - m_t)
        prev_scale = jnp.exp(log_prev - m_t)

        sl = slice(h * DH, (h + 1) * DH)
        qh = q_ref[:, sl]
        kh = k_ref[:, sl] * scale
        vh = v_ref[:, sl]
        qb = qh.astype(BF16)
        vb = vh.astype(BF16)
        c_h = c_ref[0, h]
        n_h = n_ref[0, h:h + 1, :]
        qk = _bdot_nt(qb, kh) * dm
        num = _bdot(qk, vb) + prev_scale * _bdot(qb, c_h)
        den = (jnp.sum(qk, axis=1, keepdims=True)
               + prev_scale * jnp.sum(qh * n_h, axis=1, keepdims=True))
        hh = num / jnp.maximum(jnp.abs(den), jnp.exp(-m_t))
        tok_ref[:, sl] = _sigmoid(o_ref[:, sl]) * hh

        m_last = m_t[L - 1:L, :]
        f_last = f_col[L - 1:L, :]
        w_src = jnp.exp(f_last + ig_col - f_col - m_last)
        c_scale = jnp.exp(f_last + m_prev - m_last)
        kw = kh * w_src
        c_ref[0, h] = c_scale * c_h + _bdot(kw.T, vb)
        n_ref[0, h:h + 1, :] = c_scale * n_h + jnp.sum(kw, axis=0, keepdims=True)
        m_new_row = jnp.where(lane1 == h, m_last, m_new_row)
    m_ref[0] = m_new_row


def _mlstm_prompt(proj, bias_row, batch, seq):
    nc = seq // CHUNK
    m_tok = batch * seq

    def col_block(j):
        return pl.BlockSpec((CHUNK, TOK_W), lambda b, c: (b * nc + c, j))

    return pl.pallas_call(
        _mlstm_chunk_kernel,
        grid=(batch, nc),
        in_specs=[col_block(0), col_block(1), col_block(2), col_block(3),
                  pl.BlockSpec((CHUNK, LANES), lambda b, c: (b * nc + c, GATE_COL // LANES)),
                  pl.BlockSpec((1, LANES), lambda b, c: (0, 0))],
        out_specs=[pl.BlockSpec((CHUNK, TOK_W), lambda b, c: (b * nc + c, 0)),
                   pl.BlockSpec((1, N_HEAD, DH, DH), lambda b, c: (b, 0, 0, 0)),
                   pl.BlockSpec((1, N_HEAD, DH), lambda b, c: (b, 0, 0)),
                   pl.BlockSpec((1, 1, LANES), lambda b, c: (b, 0, 0))],
        out_shape=[jax.ShapeDtypeStruct((m_tok, TOK_W), F32),
                   jax.ShapeDtypeStruct((batch, N_HEAD, DH, DH), F32),
                   jax.ShapeDtypeStruct((batch, N_HEAD, DH), F32),
                   jax.ShapeDtypeStruct((batch, 1, LANES), F32)],
        compiler_params=_cparams(("arbitrary", "arbitrary")),
        name="mlstm_chunk",
    )(proj, proj, proj, proj, proj, bias_row)


def _mlstm_step_kernel(q_ref, k_ref, v_ref, o_ref, g_ref, bias_ref, c_ref, n_ref, m_ref,
                       tok_ref, c_out, n_out, m_out):
    gb = g_ref[0] + bias_ref[...]
    m_row = m_ref[0]
    lane1 = lax.broadcasted_iota(jnp.int32, (1, LANES), 1)
    row = lax.broadcasted_iota(jnp.int32, (DH, DH), 0)
    col = lax.broadcasted_iota(jnp.int32, (DH, DH), 1)
    eye = row == col
    m_new_row = m_row
    scale = DH ** -0.5
    for h in range(N_HEAD):
        ig = gb[:, h:h + 1]
        lf = _log_sigmoid(gb[:, N_HEAD + h:N_HEAD + h + 1])
        m_prev = m_row[:, h:h + 1]
        log_prev = lf + m_prev
        m_t = jnp.maximum(log_prev, ig)
        dm = jnp.exp(ig - m_t)
        prev_scale = jnp.exp(log_prev - m_t)

        sl = slice(h * DH, (h + 1) * DH)
        q_row = q_ref[0][:, sl]
        k_row = k_ref[0][:, sl] * scale
        v_row = v_ref[0][:, sl]
        c_h = c_ref[0, 0, h]
        n_h = n_ref[0, 0, h:h + 1, :]
        qk = jnp.sum(q_row * k_row, axis=1, keepdims=True) * dm
        q_c = _dot_hp(jnp.broadcast_to(q_row, (16, DH)), c_h)[0:1, :]
        num = qk * v_row + prev_scale * q_c
        den = qk + prev_scale * jnp.sum(q_row * n_h, axis=1, keepdims=True)
        hh = num / jnp.maximum(jnp.abs(den), jnp.exp(-m_t))
        tok_ref[0, :, sl] = _sigmoid(o_ref[0][:, sl]) * hh

        k_col = jnp.sum(jnp.where(eye, jnp.broadcast_to(k_row, (DH, DH)), 0.0), axis=1, keepdims=True)
        c_out[0, h] = prev_scale * c_h + (dm * k_col) * v_row
        n_out[0, h:h + 1, :] = prev_scale * n_h + dm * k_row
        m_new_row = jnp.where(lane1 == h, m_t, m_new_row)
    m_out[0] = m_new_row


def _mlstm_sample(proj3, bias_row, c_all, n_all, m_pad, li):
    batch = proj3.shape[0]

    def col_block(j):
        return pl.BlockSpec((1, 1, TOK_W), lambda b: (b, 0, j))

    return pl.pallas_call(
        _mlstm_step_kernel,
        grid=(batch,),
        in_specs=[col_block(0), col_block(1), col_block(2), col_block(3),
                  pl.BlockSpec((1, 1, LANES), lambda b: (b, 0, GATE_COL // LANES)),
                  pl.BlockSpec((1, LANES), lambda b: (0, 0)),
                  pl.BlockSpec((1, 1, N_HEAD, DH, DH), lambda b: (li, b, 0, 0, 0)),
                  pl.BlockSpec((1, 1, N_HEAD, DH), lambda b: (li, b, 0, 0)),
                  pl.BlockSpec((1, 1, LANES), lambda b: (b, 0, 0))],
        out_specs=[pl.BlockSpec((1, 1, TOK_W), lambda b: (b, 0, 0)),
                   pl.BlockSpec((1, N_HEAD, DH, DH), lambda b: (b, 0, 0, 0)),
                   pl.BlockSpec((1, N_HEAD, DH), lambda b: (b, 0, 0)),
                   pl.BlockSpec((1, 1, LANES), lambda b: (b, 0, 0))],
        out_shape=[jax.ShapeDtypeStruct((batch, 1, TOK_W), F32),
                   jax.ShapeDtypeStruct((batch, N_HEAD, DH, DH), F32),
                   jax.ShapeDtypeStruct((batch, N_HEAD, DH), F32),
                   jax.ShapeDtypeStruct((batch, 1, LANES), F32)],
        compiler_params=_cparams(("arbitrary",)),
        name="mlstm_step",
    )(proj3, proj3, proj3, proj3, proj3, bias_row, c_all, n_all, m_pad)


def _mem_attn_kernel(q_ref, k_ref, v_ref, o_ref):
    outs = []
    for h in range(H_M):
        sl = slice(h * DH_M, (h + 1) * DH_M)
        s = _bdot_nt(q_ref[:, sl], k_ref[0][:, sl]) * (DH_M ** -0.5)
        e = jnp.exp(s - jnp.max(s, axis=1, keepdims=True))
        p = e / jnp.sum(e, axis=1, keepdims=True)
        outs.append(_bdot(p, v_ref[0][:, sl]))
    o_ref[...] = jnp.concatenate(outs, axis=-1)


def _mem_attn_prompt(proj, q_col_block, mem_k, mem_v, seq, tq):
    m_tok = proj.shape[0]
    per_b = seq // tq
    return pl.pallas_call(
        _mem_attn_kernel,
        grid=(m_tok // tq,),
        in_specs=[pl.BlockSpec((tq, MEM_W), lambda i: (i, q_col_block)),
                  pl.BlockSpec((1, N_MEM, MEM_W), lambda i: (i // per_b, 0, 0)),
                  pl.BlockSpec((1, N_MEM, MEM_W), lambda i: (i // per_b, 0, 0))],
        out_specs=pl.BlockSpec((tq, MEM_W), lambda i: (i, 0)),
        out_shape=jax.ShapeDtypeStruct((m_tok, MEM_W), F32),
        compiler_params=_cparams(("arbitrary",)),
        name="mem_attn",
    )(proj, mem_k, mem_v)


MEM_STEP_BATCH = 4


def _mem_attn_step_kernel(q_ref, k_ref, v_ref, o_ref):
    rows = 16
    rid = lax.broadcasted_iota(jnp.int32, (rows, MEM_W), 0)
    lane = lax.broadcasted_iota(jnp.int32, (rows, MEM_W), 1)
    head_mask = (lane // DH_M) == rid
    for g in range(MEM_STEP_BATCH):
        q = jnp.broadcast_to(q_ref[g], (rows, MEM_W))
        qm = jnp.where(head_mask, q, 0.0)
        s = _dot_hp(qm, k_ref[0, g]) * (DH_M ** -0.5)
        e = jnp.exp(s - jnp.max(s, axis=1, keepdims=True))
        p = e / jnp.sum(e, axis=1, keepdims=True)
        o = _dot_hp(p, v_ref[0, g], nt=True)
        o_ref[g] = jnp.sum(jnp.where(head_mask, o, 0.0), axis=0, keepdims=True)


def _mem_attn_sample(mq3, cache_k, cache_v, li):
    batch = mq3.shape[0]
    gb = MEM_STEP_BATCH
    return pl.pallas_call(
        _mem_attn_step_kernel,
        grid=(batch // gb,),
        in_specs=[pl.BlockSpec((gb, 1, MEM_W), lambda b: (b, 0, 0)),
                  pl.BlockSpec((1, gb, N_MEM, MEM_W), lambda b: (li, b, 0, 0)),
                  pl.BlockSpec((1, gb, N_MEM, MEM_W), lambda b: (li, b, 0, 0))],
        out_specs=pl.BlockSpec((gb, 1, MEM_W), lambda b: (b, 0, 0)),
        out_shape=jax.ShapeDtypeStruct((batch, 1, MEM_W), F32),
        compiler_params=_cparams(("arbitrary",)),
        name="mem_attn_step",
    )(mq3, cache_k, cache_v)


def _routing(logits):
    lane = lax.broadcasted_iota(jnp.int32, logits.shape, 1)
    lane_f = lane.astype(F32)
    big = 1000.0
    is_g = (lane >= N_EXPERTS) & (lane < N_EXPERTS + 4)
    lg = jnp.where(is_g, logits, -jnp.inf)
    gmax = jnp.max(lg, axis=1, keepdims=True)
    gidx = jnp.min(jnp.where(lg == gmax, lane_f, big), axis=1, keepdims=True) - float(N_EXPERTS)
    p_g = 1.0 / jnp.sum(jnp.exp(lg - gmax), axis=1, keepdims=True)
    in_grp = (lane < N_EXPERTS) & ((lane >> 2).astype(F32) == gidx)
    le = jnp.where(in_grp, logits, -jnp.inf)
    v1 = jnp.max(le, axis=1, keepdims=True)
    i1 = jnp.min(jnp.where(le == v1, lane_f, big), axis=1, keepdims=True)
    le2 = jnp.where(lane_f == i1, -jnp.inf, le)
    v2 = jnp.max(le2, axis=1, keepdims=True)
    i2 = jnp.min(jnp.where(le2 == v2, lane_f, big), axis=1, keepdims=True)
    e2 = jnp.exp(v2 - v1)
    inv = 1.0 / (1.0 + e2)
    comb = jnp.where(lane_f == i1, inv * p_g, jnp.where(lane_f == i2, e2 * inv * p_g, 0.0))
    return comb, gidx


GROUP_LANE = 16
RANK_LANE = 17


def _mix_kernel(x_ref, tok_ref, mem_ref, wt_ref, wm_ref, g_ref, b_ref, wr_ref, br_ref,
                x1_ref, comb_ref, cnt_ref, *, hp):
    @pl.when(pl.program_id(0) == 0)
    def _():
        cnt_ref[...] = jnp.zeros_like(cnt_ref)

    dot = _dot_hp if hp else _bdot
    mix = dot(tok_ref[...], wt_ref[...]) + dot(mem_ref[...], wm_ref[...])
    x1 = _layer_norm(ALPHA * x_ref[...] + mix, g_ref[...], b_ref[...])
    x1_ref[...] = x1
    logits = dot(x1, wr_ref[...]) + br_ref[...]
    comb, gidx = _routing(logits)
    tm = comb.shape[0]
    lane_f = lax.broadcasted_iota(jnp.int32, comb.shape, 1).astype(F32)
    mine = lane_f == gidx
    onehot = jnp.where(mine, 1.0, 0.0)
    row = lax.broadcasted_iota(jnp.int32, (tm, tm), 0)
    col = lax.broadcasted_iota(jnp.int32, (tm, tm), 1)
    before = jnp.where(col < row, 1.0, 0.0).astype(BF16)
    prefix = jnp.dot(before, onehot.astype(BF16), preferred_element_type=F32)
    cnt = cnt_ref[...]
    rank = jnp.sum(jnp.where(mine, prefix + cnt, 0.0), axis=1, keepdims=True)
    cnt_ref[...] = cnt + jnp.sum(onehot, axis=0, keepdims=True)
    comb_ref[...] = jnp.where(lane_f == float(GROUP_LANE), gidx,
                              jnp.where(lane_f == float(RANK_LANE), rank, comb))


def _mix(x, tok, mem, wt, wm, g, b, wr, br, tm, hp=False):
    m_tok = x.shape[0]
    const = lambda i: (0, 0)
    rows = lambda i: (i, 0)
    return pl.pallas_call(
        functools.partial(_mix_kernel, hp=hp),
        grid=(m_tok // tm,),
        in_specs=[pl.BlockSpec((tm, D_MODEL), rows),
                  pl.BlockSpec((tm, TOK_W), rows),
                  pl.BlockSpec((tm, MEM_W), rows),
                  pl.BlockSpec((TOK_W, D_MODEL), const),
                  pl.BlockSpec((MEM_W, D_MODEL), const),
                  pl.BlockSpec((1, D_MODEL), const),
                  pl.BlockSpec((1, D_MODEL), const),
                  pl.BlockSpec((D_MODEL, LANES), const),
                  pl.BlockSpec((1, LANES), const)],
        out_specs=[pl.BlockSpec((tm, D_MODEL), rows),
                   pl.BlockSpec((tm, LANES), rows),
                   pl.BlockSpec((1, LANES), const)],
        out_shape=[jax.ShapeDtypeStruct((m_tok, D_MODEL), F32),
                   jax.ShapeDtypeStruct((m_tok, LANES), F32),
                   jax.ShapeDtypeStruct((1, LANES), F32)],
        compiler_params=_cparams(("arbitrary",)),
        name="mix",
    )(x, tok, mem, wt, wm, g, b, wr, br)


def _moe_kernel(x_ref, comb_ref, w1_ref, w3_ref, w2_ref, g_ref, b_ref, o_ref, xb_ref, acc_ref,
                *, hp):
    e = pl.program_id(1)

    @pl.when(e == 0)
    def _():
        xb_ref[...] = x_ref[...].astype(BF16)
        acc_ref[...] = jnp.zeros_like(acc_ref)

    if hp:
        x = x_ref[...]
        a = _dot_hp(x, w1_ref[0, 0])
        b = _dot_hp(x, w3_ref[0, 0])
    else:
        xb = xb_ref[...]
        a = jnp.dot(xb, w1_ref[0, 0].astype(BF16), preferred_element_type=F32)
        b = jnp.dot(xb, w3_ref[0, 0].astype(BF16), preferred_element_type=F32)
    comb = comb_ref[...]
    lane = lax.broadcasted_iota(jnp.int32, comb.shape, 1)
    ce = jnp.sum(jnp.where(lane == e, comb, 0.0), axis=1, keepdims=True)
    hid = (a * _sigmoid(a)) * b * ce
    if hp:
        acc_ref[...] += _dot_hp(hid, w2_ref[0, 0])
    else:
        acc_ref[...] += jnp.dot(hid.astype(BF16), w2_ref[0, 0].astype(BF16),
                                preferred_element_type=F32)

    @pl.when(e == N_EXPERTS - 1)
    def _():
        o_ref[...] = _layer_norm(ALPHA * x_ref[...] + acc_ref[...], g_ref[...], b_ref[...])


def _moe(x1, comb, w1, w3, w2, g, b, li, tm, hp=False):
    m_tok = x1.shape[0]
    rows = lambda i, e: (i, 0)
    const = lambda i, e: (0, 0)
    return pl.pallas_call(
        functools.partial(_moe_kernel, hp=hp),
        grid=(m_tok // tm, N_EXPERTS),
        in_specs=[pl.BlockSpec((tm, D_MODEL), rows),
                  pl.BlockSpec((tm, LANES), rows),
                  pl.BlockSpec((1, 1, D_MODEL, D_EXPERT), lambda i, e: (li, e, 0, 0)),
                  pl.BlockSpec((1, 1, D_MODEL, D_EXPERT), lambda i, e: (li, e, 0, 0)),
                  pl.BlockSpec((1, 1, D_EXPERT, D_MODEL), lambda i, e: (li, e, 0, 0)),
                  pl.BlockSpec((1, D_MODEL), const),
                  pl.BlockSpec((1, D_MODEL), const)],
        out_specs=pl.BlockSpec((tm, D_MODEL), rows),
        out_shape=jax.ShapeDtypeStruct((m_tok, D_MODEL), F32),
        scratch_shapes=[pltpu.VMEM((tm, D_MODEL), BF16), pltpu.VMEM((tm, D_MODEL), F32)],
        compiler_params=_cparams(("arbitrary", "arbitrary")),
        name="moe",
    )(x1, comb, w1, w3, w2, g, b)


SLOT_TILE = 256
N_GROUPS = 4
EXP_PER_GROUP = 4
MOE_VMEM_LIMIT = 56 * 1024 * 1024


N_XBUF = 3


def _dispatch_kernel(dest_ref, fill_ref, x_ref, xs_ref, inv_ref, xbuf, zrow, load_sem, sem, sem_fill,
                     *, tm, n_steps, n_fill):
    i = pl.program_id(0)
    base = i * tm
    slot = i % N_XBUF

    def load(step, buf):
        return pltpu.make_async_copy(x_ref.at[pl.ds(step * tm, tm)], xbuf.at[buf], load_sem.at[buf])

    @pl.when(i == 0)
    def _():
        load(0, 0).start()

    @pl.when(i + 1 < n_steps)
    def _():
        load(i + 1, (i + 1) % N_XBUF).start()

    load(i, slot).wait()

    par = i % 2

    def body(r, carry):
        d = dest_ref[base + r]
        pltpu.make_async_copy(xbuf.at[slot, pl.ds(r, 1)], xs_ref.at[pl.ds(d, 1)], sem.at[par]).start()
        inv_ref[d] = base + r
        return carry

    lax.fori_loop(0, tm, body, 0, unroll=8)

    def wait_step(p):
        pltpu.make_async_copy(xbuf.at[0], xs_ref.at[pl.ds(0, tm)], sem.at[p]).wait()

    @pl.when(i > 0)
    def _():
        wait_step(1 - par)

    @pl.when(i == n_steps - 1)
    def _():
        wait_step(par)
        zrow[...] = jnp.zeros_like(zrow)

        def fill(s, carry):
            pltpu.make_async_copy(zrow.at[pl.ds(0, 1)], xs_ref.at[pl.ds(s, 1)], sem_fill).start()
            inv_ref[s] = -1
            return carry

        for k in range(N_GROUPS + 1):
            lax.fori_loop(fill_ref[2 * k], fill_ref[2 * k + 1], fill, 0)
        pltpu.make_async_copy(xs_ref.at[pl.ds(0, n_fill)], xs_ref.at[pl.ds(0, n_fill)],
                              sem_fill).wait()


def _dispatch(dest, fill, x1, n_slots, tm):
    m_tok = x1.shape[0]
    n_steps = m_tok // tm
    grid_spec = pltpu.PrefetchScalarGridSpec(
        num_scalar_prefetch=2,
        grid=(n_steps,),
        in_specs=[pl.BlockSpec(memory_space=pl.ANY)],
        out_specs=[pl.BlockSpec(memory_space=pl.ANY),
                   pl.BlockSpec(memory_space=pltpu.SMEM)],
        scratch_shapes=[pltpu.VMEM((N_XBUF, tm, D_MODEL), F32),
                        pltpu.VMEM((8, D_MODEL), F32),
                        pltpu.SemaphoreType.DMA((N_XBUF,)),
                        pltpu.SemaphoreType.DMA((2,)),
                        pltpu.SemaphoreType.DMA(())],
    )
    return pl.pallas_call(
        functools.partial(_dispatch_kernel, tm=tm, n_steps=n_steps, n_fill=n_slots - m_tok),
        grid_spec=grid_spec,
        out_shape=[jax.ShapeDtypeStruct((n_slots, D_MODEL), F32),
                   jax.ShapeDtypeStruct((n_slots,), jnp.int32)],
        compiler_params=_cparams(("arbitrary",)),
        name="moe_dispatch",
    )(dest, fill, x1)


def _moe_sparse_kernel(blk_ref, grp_ref, val_ref, nrow_ref, inv_ref,
                       xs_ref, w1_ref, w3_ref, w2_ref, wr_ref, br_ref, g_ref, b_ref,
                       out_ref, w1b, w3b, w2b, ybuf, junk, state, sem, *, n_tiles):
    i = pl.program_id(0)
    ts = SLOT_TILE

    @pl.when(i == 0)
    def _():
        state[0] = -1
        state[1] = 0

    def wait_pending():
        @pl.when(state[1] == 1)
        def _():
            pltpu.make_async_copy(ybuf, junk, sem).wait()

    @pl.when(val_ref[i] == 1)
    def _():
        grp = grp_ref[i]

        @pl.when(grp != state[0])
        def _():
            for j in range(EXP_PER_GROUP):
                cols = slice(j * D_EXPERT, (j + 1) * D_EXPERT)
                w1b[:, cols] = w1_ref[0, 0, j].astype(BF16)
                w3b[:, cols] = w3_ref[0, 0, j].astype(BF16)
                w2b[cols, :] = w2_ref[0, 0, j].astype(BF16)
            state[0] = grp

        x = xs_ref[...]
        xb = x.astype(BF16)
        logits = jnp.dot(xb, wr_ref[...], preferred_element_type=F32) + br_ref[...]
        comb, _ = _routing(logits)
        lane = lax.broadcasted_iota(jnp.int32, comb.shape, 1)
        a = jnp.dot(xb, w1b[...], preferred_element_type=F32)
        b = jnp.dot(xb, w3b[...], preferred_element_type=F32)
        hid = []
        for j in range(EXP_PER_GROUP):
            cols = slice(j * D_EXPERT, (j + 1) * D_EXPERT)
            ce = jnp.sum(jnp.where(lane == grp * EXP_PER_GROUP + j, comb, 0.0), axis=1, keepdims=True)
            a_j = a[:, cols]
            hid.append(((a_j * _sigmoid(a_j)) * b[:, cols] * ce).astype(BF16))
        acc = jnp.dot(jnp.concatenate(hid, axis=1), w2b[...], preferred_element_type=F32)
        x2 = _layer_norm(ALPHA * x + acc, g_ref[...], b_ref[...])

        wait_pending()
        ybuf[...] = x2
        n = nrow_ref[i]
        base = blk_ref[i] * ts

        def body(r, carry):
            t = inv_ref[base + r]
            pltpu.make_async_copy(ybuf.at[pl.ds(r, 1)], out_ref.at[pl.ds(t, 1)], sem).start()
            return carry

        def pad_body(r, carry):
            pltpu.make_async_copy(ybuf.at[pl.ds(r, 1)], junk.at[pl.ds(r, 1)], sem).start()
            return carry

        full = n // 8

        def body8(gi, carry):
            for u in range(8):
                body(gi * 8 + u, carry)
            return carry

        lax.fori_loop(0, full, body8, 0)
        lax.fori_loop(full * 8, n, body, 0)
        lax.fori_loop(n, ts, pad_body, 0)
        state[1] = 1

    @pl.when(i == n_tiles - 1)
    def _():
        wait_pending()
        state[1] = 0


def _moe_sparse(x1, comb, cnt, P, W, li):
    m_tok = x1.shape[0]
    ts = SLOT_TILE
    n_tiles = m_tok // ts + N_GROUPS
    n_slots = n_tiles * ts
    grp_tok = comb[:, GROUP_LANE].astype(jnp.int32)
    rank_tok = comb[:, RANK_LANE].astype(jnp.int32)
    counts = cnt[0, :N_GROUPS].astype(jnp.int32)
    tiles_g = (counts + ts - 1) // ts
    tile_end = jnp.cumsum(tiles_g)
    slot_base = (tile_end - tiles_g) * ts
    dest = jnp.take(slot_base, grp_tok) + rank_tok
    total = tile_end[-1]
    tile_i = jnp.arange(n_tiles, dtype=jnp.int32)
    valid = (tile_i < total).astype(jnp.int32)
    blk = jnp.minimum(tile_i, total - 1)
    grp_tile = jnp.minimum(jnp.sum((tile_end[None, :] <= blk[:, None]).astype(jnp.int32), axis=1),
                           N_GROUPS - 1)
    n_rows = jnp.clip(jnp.take(slot_base + counts, grp_tile) - blk * ts, 0, ts) * valid

    fill_lo = jnp.concatenate([slot_base + counts, (total * ts)[None]])
    fill_hi = jnp.concatenate([slot_base + tiles_g * ts, jnp.full((1,), n_slots, jnp.int32)])
    fill = jnp.stack([fill_lo, fill_hi], axis=1).reshape(-1).astype(jnp.int32)

    xs, inv = _dispatch(dest, fill, x1, n_slots, 512)

    w1 = P['w_e1'].reshape(DEPTH, N_GROUPS, EXP_PER_GROUP, D_MODEL, D_EXPERT)
    w3 = P['w_e3'].reshape(DEPTH, N_GROUPS, EXP_PER_GROUP, D_MODEL, D_EXPERT)
    w2 = P['w_e2'].reshape(DEPTH, N_GROUPS, EXP_PER_GROUP, D_EXPERT, D_MODEL)
    const = lambda i, *_: (0, 0)
    grid_spec = pltpu.PrefetchScalarGridSpec(
        num_scalar_prefetch=5,
        grid=(n_tiles,),
        in_specs=[pl.BlockSpec((ts, D_MODEL), lambda i, blk, grp, val, nrow, inv: (blk[i], 0)),
                  pl.BlockSpec((1, 1, EXP_PER_GROUP, D_MODEL, D_EXPERT),
                               lambda i, blk, grp, val, nrow, inv: (li, grp[i], 0, 0, 0)),
                  pl.BlockSpec((1, 1, EXP_PER_GROUP, D_MODEL, D_EXPERT),
                               lambda i, blk, grp, val, nrow, inv: (li, grp[i], 0, 0, 0)),
                  pl.BlockSpec((1, 1, EXP_PER_GROUP, D_EXPERT, D_MODEL),
                               lambda i, blk, grp, val, nrow, inv: (li, grp[i], 0, 0, 0)),
                  pl.BlockSpec((D_MODEL, LANES), const),
                  pl.BlockSpec((1, LANES), const),
                  pl.BlockSpec((1, D_MODEL), const),
                  pl.BlockSpec((1, D_MODEL), const)],
        out_specs=pl.BlockSpec(memory_space=pl.ANY),
        scratch_shapes=[pltpu.VMEM((D_MODEL, EXP_PER_GROUP * D_EXPERT), BF16),
                        pltpu.VMEM((D_MODEL, EXP_PER_GROUP * D_EXPERT), BF16),
                        pltpu.VMEM((EXP_PER_GROUP * D_EXPERT, D_MODEL), BF16),
                        pltpu.VMEM((ts, D_MODEL), F32),
                        pltpu.VMEM((ts, D_MODEL), F32),
                        pltpu.SMEM((2,), jnp.int32),
                        pltpu.SemaphoreType.DMA(())],
    )
    return pl.pallas_call(
        functools.partial(_moe_sparse_kernel, n_tiles=n_tiles),
        grid_spec=grid_spec,
        out_shape=jax.ShapeDtypeStruct((m_tok, D_MODEL), F32),
        compiler_params=pltpu.CompilerParams(dimension_semantics=("arbitrary",),
                                             vmem_limit_bytes=MOE_VMEM_LIMIT),
        name="moe_sparse",
    )(blk, grp_tile, valid, n_rows, inv, xs, w1, w3, w2, W['w_r'][li], W['b_r'][li],
      P['ln_ffn_g'][li][None, :], P['ln_ffn_b'][li][None, :])


def _diff_lambda(lq_ref, lam_init):
    lq = lq_ref[0]
    a = jnp.sum(lq[0:1, :] * lq[1:2, :], axis=1, keepdims=True)
    b = jnp.sum(lq[2:3, :] * lq[3:4, :], axis=1, keepdims=True)
    return jnp.exp(a) - jnp.exp(b) + lam_init


def _sub_norm(o, g_row, lam_init):
    o = o * lax.rsqrt(jnp.mean(o * o, axis=-1, keepdims=True) + LN_EPS) * g_row
    return o * (1.0 - lam_init)


def _diff_attn_kernel(q1_ref, q2_ref, k1_ref, k2_ref, vt_ref, lq_ref, gs_ref, o_ref,
                      m_scr, l_scr, acc_scr, *, tq, lam_init):
    i = pl.program_id(2)
    lam = _diff_lambda(lq_ref, lam_init)
    q_scale = (DQK ** -0.5) * math.log2(math.e)
    qs = [(q1_ref[...] * q_scale).astype(BF16), (q2_ref[...] * q_scale).astype(BF16)]
    k_refs = [k1_ref, k2_ref]
    m_scr[...] = jnp.full_like(m_scr, -jnp.inf)
    l_scr[...] = jnp.zeros_like(l_scr)
    acc_scr[...] = jnp.zeros_like(acc_scr)
    key_i = lax.broadcasted_iota(jnp.int32, (tq, tq), 0)
    qry_i = lax.broadcasted_iota(jnp.int32, (tq, tq), 1)
    causal = key_i <= qry_i

    def block(j, masked):
        start = pl.multiple_of(j * tq, tq)
        sts = []
        for hh in range(2):
            for mm in range(2):
                kb = k_refs[mm][pl.ds(start, tq), hh * DQK:(hh + 1) * DQK]
                st = lax.dot_general(kb, qs[mm][:, hh * DQK:(hh + 1) * DQK],
                                     (((1,), (1,)), ((), ())), preferred_element_type=F32)
                if masked:
                    st = jnp.where(causal, st, -jnp.inf)
                sts.append(st)
        m_all = m_scr[...]
        l_all = l_scr[...]
        ps, alphas, m_rows, l_rows = [], [], [], []
        for idx in range(4):
            m_old = m_all[idx:idx + 1, :]
            m_new = jnp.maximum(m_old, jnp.max(sts[idx], axis=0, keepdims=True))
            alpha = jnp.exp2(m_old - m_new)
            p = jnp.exp2(sts[idx] - m_new)
            l_rows.append(alpha * l_all[idx:idx + 1, :] + jnp.sum(p, axis=0, keepdims=True))
            m_rows.append(m_new)
            alphas.append(alpha)
            ps.append(p.astype(BF16))
        m_scr[...] = jnp.concatenate(m_rows + [m_all[4:, :]], axis=0)
        l_scr[...] = jnp.concatenate(l_rows + [l_all[4:, :]], axis=0)
        for idx in range(4):
            vt = vt_ref[j, (idx // 2) * DH:(idx // 2 + 1) * DH, :]
            acc_scr[idx] = alphas[idx] * acc_scr[idx] + jnp.dot(vt, ps[idx],
                                                                preferred_element_type=F32)

    def body(j, carry):
        block(j, False)
        return carry

    lax.fori_loop(0, i, body, 0)
    block(i, True)

    for hh in range(2):
        o1 = acc_scr[hh * 2] / l_scr[hh * 2:hh * 2 + 1, :]
        o2 = acc_scr[hh * 2 + 1] / l_scr[hh * 2 + 1:hh * 2 + 2, :]
        o = o1 - lam * o2
        o = o * lax.rsqrt(jnp.mean(o * o, axis=0, keepdims=True) + LN_EPS)
        o_ref[:, hh * DH:(hh + 1) * DH] = o.T * gs_ref[...] * (1.0 - lam_init)


def _diff_attn_prompt(proj_b, k_b, vt_b, lam_qk, g_sub, j, lam_init, batch, seq, tq):
    m_tok = batch * seq
    nq = seq // tq
    kern = functools.partial(_diff_attn_kernel, tq=tq, lam_init=lam_init)
    return pl.pallas_call(
        kern,
        grid=(batch, 3, nq),
        in_specs=[pl.BlockSpec((tq, 2 * DQK), lambda b, hp, i: (b * nq + i, hp)),
                  pl.BlockSpec((tq, 2 * DQK), lambda b, hp, i: (b * nq + i, 3 + hp)),
                  pl.BlockSpec((seq, 2 * DQK), lambda b, hp, i: (b, hp)),
                  pl.BlockSpec((seq, 2 * DQK), lambda b, hp, i: (b, 3 + hp)),
                  pl.BlockSpec((nq, 2 * DH, tq), lambda b, hp, i: (b, hp, 0)),
                  pl.BlockSpec((1, 4, DQK), lambda b, hp, i: (j, 0, 0)),
                  pl.BlockSpec((1, DH), lambda b, hp, i: (0, 0))],
        out_specs=pl.BlockSpec((tq, 2 * DH), lambda b, hp, i: (b * nq + i, hp)),
        out_shape=jax.ShapeDtypeStruct((m_tok, TOK_W), F32),
        scratch_shapes=[pltpu.VMEM((8, tq), F32), pltpu.VMEM((8, tq), F32),
                        pltpu.VMEM((4, DH, tq), F32)],
        compiler_params=_cparams(("arbitrary", "arbitrary", "arbitrary")),
        name="diff_attn",
    )(proj_b, proj_b, k_b, k_b, vt_b, lam_qk, g_sub)


PAGES_PER_STEP = 8


def _diff_attn_step_kernel(pt_ref, q_ref, kn_ref, vn_ref, *rest, n_steps, lam_init):
    k_refs = rest[:PAGES_PER_STEP]
    v_refs = rest[PAGES_PER_STEP:2 * PAGES_PER_STEP]
    lq_ref, gs_ref, o_ref, qm_scr, m_scr, l_scr, acc_scr = rest[2 * PAGES_PER_STEP:]
    p_idx = pl.program_id(1)
    rows = 16
    rid = lax.broadcasted_iota(jnp.int32, (rows, TOK_W), 0)
    lane = lax.broadcasted_iota(jnp.int32, (rows, TOK_W), 1)

    @pl.when(p_idx == 0)
    def _():
        q = jnp.broadcast_to(q_ref[0] * (DQK ** -0.5), (rows, TOK_W))
        qm_scr[...] = jnp.where((lane // DQK) == rid, q, 0.0)
        m_scr[...] = jnp.full_like(m_scr, -jnp.inf)
        l_scr[...] = jnp.zeros_like(l_scr)
        acc_scr[...] = jnp.zeros_like(acc_scr)

    qm = qm_scr[...]
    q_hi, q_lo = _split2(qm)
    q2 = jnp.concatenate([q_hi, q_lo], axis=0)
    s_pages = []
    for k_ref in k_refs:
        k_hi, k_lo = _split2(k_ref[0])
        r1 = _bdot(q2, k_hi)
        s_pages.append((_bdot(q_hi, k_lo) + r1[rows:]) + r1[:rows])
    s = jnp.concatenate(s_pages, axis=1)
    m_old = m_scr[...]
    m_new = jnp.maximum(m_old, jnp.max(s, axis=1, keepdims=True))
    alpha = jnp.exp(m_old - m_new)
    p = jnp.exp(s - m_new)
    p_hi, p_lo = _split2(p)
    p2 = jnp.concatenate([p_hi, p_lo], axis=0)
    pv = []
    for h in range(N_HEAD):
        pv_h = None
        for g in range(PAGES_PER_STEP):
            cols = slice(g * PAGE, (g + 1) * PAGE)
            v_hi, v_lo = _split2(v_refs[g][0, h])
            r1 = _bdot(p2[:, cols], v_hi)
            term = (_bdot(p_hi[:, cols], v_lo) + r1[rows:]) + r1[:rows]
            pv_h = term if pv_h is None else pv_h + term
        pv.append(pv_h)
    l_scr[...] = alpha * l_scr[...] + jnp.sum(p, axis=1, keepdims=True)
    acc_scr[...] = alpha * acc_scr[...] + jnp.concatenate(pv, axis=1)
    m_scr[...] = m_new

    @pl.when(p_idx == n_steps - 1)
    def _():
        lam = _diff_lambda(lq_ref, lam_init)
        s_new = jnp.sum(qm * kn_ref[0], axis=1, keepdims=True)
        m_old2 = m_scr[...]
        m_fin = jnp.maximum(m_old2, s_new)
        a2 = jnp.exp(m_old2 - m_fin)
        p_new = jnp.exp(s_new - m_fin)
        l_fin = a2 * l_scr[...] + p_new
        acc = a2 * acc_scr[...] + p_new * vn_ref[0]
        r = acc / l_fin
        head_of_lane = lane // DH
        o1 = jnp.sum(jnp.where(head_of_lane == rid, r, 0.0), axis=0, keepdims=True)
        o2 = jnp.sum(jnp.where(head_of_lane == rid - N_HEAD, r, 0.0), axis=0, keepdims=True)
        o = o1 - lam * o2
        for h in range(N_HEAD):
            sl = slice(h * DH, (h + 1) * DH)
            o_ref[0, :, sl] = _sub_norm(o[:, sl], gs_ref[...], lam_init)


def _diff_attn_sample(page_table, q3, kn3, vn3, cache_k, cache_v, lam_qk, g_sub, j, lam_init):
    batch, n_pages = page_table.shape
    n_steps = n_pages // PAGES_PER_STEP
    kern = functools.partial(_diff_attn_step_kernel, n_steps=n_steps, lam_init=lam_init)
    row3 = lambda b, p, pt: (b, 0, 0)

    def k_spec(g):
        return pl.BlockSpec((1, TOK_W, PAGE), lambda b, p, pt: (pt[b, p * PAGES_PER_STEP + g], 0, 0))

    def v_spec(g):
        return pl.BlockSpec((1, N_HEAD, PAGE, DH),
                            lambda b, p, pt: (pt[b, p * PAGES_PER_STEP + g], 0, 0, 0))

    k_specs = [k_spec(g) for g in range(PAGES_PER_STEP)]
    v_specs = [v_spec(g) for g in range(PAGES_PER_STEP)]
    grid_spec = pltpu.PrefetchScalarGridSpec(
        num_scalar_prefetch=1,
        grid=(batch, n_steps),
        in_specs=[pl.BlockSpec((1, 1, TOK_W), row3),
                  pl.BlockSpec((1, 1, TOK_W), row3),
                  pl.BlockSpec((1, 1, TOK_W), row3)] + k_specs + v_specs + [
                  pl.BlockSpec((1, 4, DQK), lambda b, p, pt: (j, 0, 0)),
                  pl.BlockSpec((1, DH), lambda b, p, pt: (0, 0))],
        out_specs=pl.BlockSpec((1, 1, TOK_W), row3),
        scratch_shapes=[pltpu.VMEM((16, TOK_W), F32), pltpu.VMEM((16, 1), F32),
                        pltpu.VMEM((16, 1), F32), pltpu.VMEM((16, TOK_W), F32)],
    )
    return pl.pallas_call(
        kern,
        grid_spec=grid_spec,
        out_shape=jax.ShapeDtypeStruct((batch, 1, TOK_W), F32),
        compiler_params=_cparams(("arbitrary", "arbitrary")),
        name="diff_attn_step",
    )(page_table, q3, kn3, vn3, *([cache_k] * PAGES_PER_STEP), *([cache_v] * PAGES_PER_STEP),
      lam_qk, g_sub)


def _diff_lambda_init(li):
    return 0.8 - 0.6 * math.exp(-0.3 * li)


def _prep_weights(P):
    w = {}
    wa = P['w_in_a']
    gates = jnp.pad(wa[:, :, 4 * TOK_W:4 * TOK_W + 2 * N_HEAD], ((0, 0), (0, 0), (0, LANES - 2 * N_HEAD)))
    w['w_a32'] = jnp.concatenate([wa[:, :, :4 * TOK_W], wa[:, :, 4 * TOK_W + 2 * N_HEAD:], gates],
                                 axis=-1)
    w['w_a'] = w['w_a32'].astype(BF16)
    w['gate_bias'] = jnp.pad(jnp.concatenate([P['b_igate'], P['b_fgate']], axis=-1),
                             ((0, 0), (0, LANES - 2 * N_HEAD)))[:, None, :]
    w['w_b'] = P['w_in_b'].astype(BF16)
    w['w_kv'] = P['w_kv_shared'].astype(BF16)
    w['w_mem'] = jnp.transpose(P['w_mem_kv'], (1, 0, 2)).reshape(D_MODEL, DEPTH * 2 * MEM_W).astype(BF16)
    w['w_out_t32'] = P['w_out'][:, :TOK_W, :]
    w['w_out_m32'] = P['w_out'][:, TOK_W:, :]
    w['w_out_t'] = w['w_out_t32'].astype(BF16)
    w['w_out_m'] = w['w_out_m32'].astype(BF16)
    wr = jnp.concatenate([P['w_router'], P['w_group']], axis=-1)
    w['w_r32'] = jnp.pad(wr, ((0, 0), (0, 0), (0, LANES - wr.shape[-1])))
    w['w_r'] = w['w_r32'].astype(BF16)
    br = jnp.concatenate([P['b_router'], P['b_group']], axis=-1)
    w['b_r'] = jnp.pad(br, ((0, 0), (0, LANES - br.shape[-1])))[:, None, :]
    return w


def _layer_tail(x, tok, mem, li, P, W, tm_mix, tm_moe, hp=False):
    sfx = '32' if hp else ''
    x1, comb, cnt = _mix(x, tok, mem, W['w_out_t' + sfx][li], W['w_out_m' + sfx][li],
                         P['ln_mix_g'][li][None, :], P['ln_mix_b'][li][None, :],
                         W['w_r' + sfx][li], W['b_r'][li], tm_mix, hp)
    if hp:
        return _moe(x1, comb, P['w_e1'], P['w_e3'], P['w_e2'],
                    P['ln_ffn_g'][li][None, :], P['ln_ffn_b'][li][None, :], li, tm_moe, hp)
    return _moe_sparse(x1, comb, cnt, P, W, li)


def _prompt_trunk(x_prompt, mem_prompt, P, W):
    batch, seq, _ = x_prompt.shape
    m_tok = batch * seq
    x = x_prompt.reshape(m_tok, D_MODEL)
    mem = mem_prompt.reshape(batch * N_MEM, D_MODEL)
    kv_mem = _proj(mem, W['w_mem'], 512)
    kv_mem = jnp.transpose(kv_mem.reshape(batch, N_MEM, DEPTH, 2, MEM_W), (3, 2, 0, 1, 4))
    mem_k, mem_v = kv_mem[0], kv_mem[1]

    new_c, new_n, new_m = [], [], []
    k_t = v_h = k16 = vt16 = None
    for li in range(DEPTH):
        if li < N_A:
            proj = _proj(x, W['w_a'][li], 256)
            tok, c, n, m = _mlstm_prompt(proj, W['gate_bias'][li], batch, seq)
            new_c.append(c)
            new_n.append(n)
            new_m.append(m[:, 0, :N_HEAD])
            mq_block = (4 * TOK_W) // MEM_W
        else:
            if li == N_A:
                k_t, v_h, k16, vt16 = _proj_shared_kv(x, W['w_kv'], batch, seq, 256)
            j = li - N_A
            proj = _proj(x, W['w_b'][j], 512)
            tok = _diff_attn_prompt(proj, k16, vt16, P['lambda_qk'], P['subln_g'][j:j + 1], j,
                                    _diff_lambda_init(li), batch, seq, 256)
            mq_block = TOK_W // MEM_W
        mem_o = _mem_attn_prompt(proj, mq_block, mem_k[li], mem_v[li], seq, 512)
        x = _layer_tail(x, tok, mem_o, li, P, W, 512, 1024)
    y = x.reshape(batch, seq, D_MODEL)
    k_p = jnp.transpose(k_t.reshape(batch, 2, N_HEAD, DQK, seq), (0, 4, 1, 2, 3))
    v_p = jnp.transpose(v_h, (0, 2, 1, 3))
    shp = (DEPTH, batch, N_MEM, H_M, DH_M)
    return (y, mem_k.reshape(shp), mem_v.reshape(shp),
            jnp.stack(new_c), jnp.stack(new_n), jnp.stack(new_m), k_p, v_p)


def _sample_trunk(x_sample, cache_mem_k, cache_mem_v, state, cache_k, cache_v, page_table, P, W):
    batch = x_sample.shape[0]
    x = x_sample.reshape(batch, D_MODEL)
    c_all, n_all, m_all = state
    mem_k = jnp.transpose(cache_mem_k, (0, 1, 3, 4, 2)).reshape(DEPTH, batch, MEM_W, N_MEM)
    mem_v = jnp.transpose(cache_mem_v, (0, 1, 3, 4, 2)).reshape(DEPTH, batch, MEM_W, N_MEM)
    ck = jnp.transpose(cache_k, (0, 2, 3, 4, 1)).reshape(cache_k.shape[0], TOK_W, PAGE)
    cv = jnp.transpose(cache_v, (0, 2, 1, 3))

    new_c, new_n, new_m = [], [], []
    kv32 = None
    for li in range(DEPTH):
        if li < N_A:
            proj = _proj(x, W['w_a32'][li], batch, 384, hp=True)
            m_pad = jnp.pad(m_all[li], ((0, 0), (0, LANES - N_HEAD)))[:, None, :]
            tok3, c, n, m = _mlstm_sample(proj[:, None, :], W['gate_bias'][li], c_all, n_all, m_pad, li)
            new_c.append(c)
            new_n.append(n)
            new_m.append(m[:, 0, :N_HEAD])
            mq = proj[:, 4 * TOK_W:4 * TOK_W + MEM_W]
        else:
            if li == N_A:
                kv32 = _proj(x, P['w_kv_shared'], batch, 384, hp=True)
            j = li - N_A
            proj = _proj(x, P['w_in_b'][j], batch, 256, hp=True)
            tok3 = _diff_attn_sample(page_table, proj[:, None, :TOK_W], kv32[:, None, :TOK_W],
                                     kv32[:, None, TOK_W:], ck, cv, P['lambda_qk'],
                                     P['subln_g'][j:j + 1], j, _diff_lambda_init(li))
            mq = proj[:, TOK_W:]
        mem_o = _mem_attn_sample(mq[:, None, :], mem_k, mem_v, li)
        x = _layer_tail(x, tok3[:, 0, :], mem_o[:, 0, :], li, P, W, batch, batch, hp=True)
    y = x.reshape(batch, 1, D_MODEL)
    k_s = kv32[:, :TOK_W].reshape(batch, 1, 2, N_HEAD, DQK)
    v_s = kv32[:, TOK_W:].reshape(batch, 1, N_HEAD, DH)
    return y, jnp.stack(new_c), jnp.stack(new_n), jnp.stack(new_m), k_s, v_s


def kernel(x_prompt, x_sample, mem_prompt, cache_mem_k, cache_mem_v, state_mlstm_C, state_mlstm_n,
           state_mlstm_m, cache_k, cache_v, page_table, w_in_a, b_igate, b_fgate, w_in_b, lambda_qk,
           subln_g, w_kv_shared, w_mem_kv, w_out, ln_mix_g, ln_mix_b, ln_ffn_g, ln_ffn_b, w_group,
           b_group, w_router, b_router, w_e1, w_e3, w_e2):
    P = {'w_in_a': w_in_a, 'b_igate': b_igate, 'b_fgate': b_fgate, 'w_in_b': w_in_b,
         'lambda_qk': lambda_qk, 'subln_g': subln_g, 'w_kv_shared': w_kv_shared, 'w_mem_kv': w_mem_kv,
         'w_out': w_out, 'ln_mix_g': ln_mix_g, 'ln_mix_b': ln_mix_b, 'ln_ffn_g': ln_ffn_g,
         'ln_ffn_b': ln_ffn_b, 'w_group': w_group, 'b_group': b_group, 'w_router': w_router,
         'b_router': b_router, 'w_e1': w_e1, 'w_e3': w_e3, 'w_e2': w_e2}
    W = _prep_weights(P)
    y_p, mem_k_p, mem_v_p, c_p, n_p, m_p, k_p, v_p = _prompt_trunk(x_prompt, mem_prompt, P, W)
    y_s, c_s, n_s, m_s, k_s, v_s = _sample_trunk(
        x_sample, cache_mem_k, cache_mem_v, (state_mlstm_C, state_mlstm_n, state_mlstm_m),
        cache_k, cache_v, page_table, P, W)
    return (y_p, y_s, mem_k_p, mem_v_p, c_p, n_p, m_p, k_p, v_p, c_s, n_s, m_s, k_s, v_s)
```

```python
import functools
import math

import jax
import jax.numpy as jnp
from jax import lax
from jax.experimental import pallas as pl
from jax.experimental.pallas import tpu as pltpu

F32 = jnp.float32
BF16 = jnp.bfloat16

D_MODEL = 1024
DEPTH = 4
N_A = 2
TOK_W = 768
MEM_W = 256
N_HEAD = 6
DH = 128
DQK = 64
H_M = 4
DH_M = 64
N_MEM = 256
CHUNK = 128
N_EXPERTS = 16
D_EXPERT = 256
ALPHA = (2.0 * DEPTH) ** 0.25
LN_EPS = 1e-5
PAGE = 128
LANES = 128
GATE_COL = 3328
A_COLS = 3456
VMEM_LIMIT = 48 * 1024 * 1024


def _cparams(sem):
    return pltpu.CompilerParams(dimension_semantics=sem, vmem_limit_bytes=VMEM_LIMIT)


def _bdot(a, b):
    return jnp.dot(a.astype(BF16), b.astype(BF16), preferred_element_type=F32)


def _bdot_nt(a, b):
    return lax.dot_general(a.astype(BF16), b.astype(BF16), (((1,), (1,)), ((), ())),
                           preferred_element_type=F32)


def _split3(a):
    hi = a.astype(BF16)
    r = a - hi.astype(F32)
    mid = r.astype(BF16)
    lo = (r - mid.astype(F32)).astype(BF16)
    return hi, mid, lo


def _split2(a):
    hi = a.astype(BF16)
    return hi, (a - hi.astype(F32)).astype(BF16)


def _dot_hp(a, b, nt=False):
    m = a.shape[0]
    a_hi, a_mid, a_lo = _split3(a)
    b_hi, b_mid, b_lo = _split3(b)
    dot = _bdot_nt if nt else _bdot
    a3 = jnp.concatenate([a_hi, a_mid, a_lo], axis=0)
    r1 = dot(a3, b_hi)
    r2 = dot(a3[:2 * m], b_mid)
    r3 = dot(a_hi, b_lo)
    return ((r3 + r2[m:]) + (r1[2 * m:] + r2[:m]) + r1[m:2 * m]) + r1[:m]


def _dot_x3(a, b, nt=False):
    m = a.shape[0]
    a_hi, a_lo = _split2(a)
    b_hi, b_lo = _split2(b)
    dot = _bdot_nt if nt else _bdot
    r1 = dot(jnp.concatenate([a_hi, a_lo], axis=0), b_hi)
    return (dot(a_hi, b_lo) + r1[m:]) + r1[:m]


def _layer_norm(y, g, b):
    mu = jnp.mean(y, axis=-1, keepdims=True)
    d = y - mu
    var = jnp.mean(d * d, axis=-1, keepdims=True)
    return d * lax.rsqrt(var + LN_EPS) * g + b


def _sigmoid(x):
    return 1.0 / (1.0 + jnp.exp(-x))


def _log_sigmoid(x):
    return jnp.minimum(x, 0.0) - jnp.log1p(jnp.exp(-jnp.abs(x)))


def _proj_kernel(x_ref, w_ref, o_ref, *, hp):
    if hp:
        o_ref[...] = _dot_hp(x_ref[...], w_ref[...])
    else:
        o_ref[...] = jnp.dot(x_ref[...].astype(BF16), w_ref[...], preferred_element_type=F32)


def _proj(x, w, tm, tn=None, hp=False):
    m, k = x.shape
    n = w.shape[1]
    tn = n if tn is None else tn
    return pl.pallas_call(
        functools.partial(_proj_kernel, hp=hp),
        grid=(m // tm, n // tn),
        in_specs=[pl.BlockSpec((tm, k), lambda i, j: (i, 0)),
                  pl.BlockSpec((k, tn), lambda i, j: (0, j))],
        out_specs=pl.BlockSpec((tm, tn), lambda i, j: (i, j)),
        out_shape=jax.ShapeDtypeStruct((m, n), F32),
        compiler_params=_cparams(("arbitrary", "arbitrary")),
        name="proj_hp" if hp else "proj",
    )(x, w)


def _proj_kv_kernel(x_ref, w_ref, kt_ref, vh_ref, kb_ref, vt_ref):
    y = jnp.dot(x_ref[...].astype(BF16), w_ref[...], preferred_element_type=F32)
    k = y[:, :TOK_W]
    v = y[:, TOK_W:]
    kt_ref[0] = k.T
    for h in range(N_HEAD):
        vh_ref[0, h] = v[:, h * DH:(h + 1) * DH]
    kb_ref[...] = k.astype(BF16)
    vt_ref[0] = v.T.astype(BF16)


def _proj_shared_kv(x, w, batch, seq, tm):
    m, k = x.shape
    n = w.shape[1]
    per_b = seq // tm
    return pl.pallas_call(
        _proj_kv_kernel,
        grid=(m // tm,),
        in_specs=[pl.BlockSpec((tm, k), lambda i: (i, 0)),
                  pl.BlockSpec((k, n), lambda i: (0, 0))],
        out_specs=[pl.BlockSpec((1, TOK_W, tm), lambda i: (i // per_b, 0, i % per_b)),
                   pl.BlockSpec((1, N_HEAD, tm, DH), lambda i: (i // per_b, 0, i % per_b, 0)),
                   pl.BlockSpec((tm, TOK_W), lambda i: (i, 0)),
                   pl.BlockSpec((1, TOK_W, tm), lambda i: (i, 0, 0))],
        out_shape=[jax.ShapeDtypeStruct((batch, TOK_W, seq), F32),
                   jax.ShapeDtypeStruct((batch, N_HEAD, seq, DH), F32),
                   jax.ShapeDtypeStruct((m, TOK_W), BF16),
                   jax.ShapeDtypeStruct((m // tm, TOK_W, tm), BF16)],
        compiler_params=_cparams(("arbitrary",)),
        name="proj_kv",
    )(x, w)


def _mlstm_chunk_kernel(q_ref, k_ref, v_ref, o_ref, g_ref, bias_ref,
                        tok_ref, c_ref, n_ref, m_ref):
    @pl.when(pl.program_id(1) == 0)
    def _():
        c_ref[...] = jnp.zeros_like(c_ref)
        n_ref[...] = jnp.zeros_like(n_ref)
        m_ref[...] = jnp.zeros_like(m_ref)

    L = CHUNK
    lane = lax.broadcasted_iota(jnp.int32, (L, LANES), 1)
    row = lax.broadcasted_iota(jnp.int32, (L, L), 0)
    col = lax.broadcasted_iota(jnp.int32, (L, L), 1)
    causal = col <= row

    gb = g_ref[...] + bias_ref[...]
    lf = _log_sigmoid(gb)
    tril = causal.astype(BF16)
    hi = lf.astype(BF16)
    r1 = lf - hi.astype(F32)
    mid = r1.astype(BF16)
    lo = (r1 - mid.astype(F32)).astype(BF16)
    fcum = (jnp.dot(tril, hi, preferred_element_type=F32)
            + jnp.dot(tril, mid, preferred_element_type=F32)
            + jnp.dot(tril, lo, preferred_element_type=F32))
    z = jnp.where(lane < N_HEAD, gb, fcum)
    zt = z.T

    m_row = m_ref[0]
    lane1 = lax.broadcasted_iota(jnp.int32, (1, LANES), 1)
    m_new_row = m_row
    scale = DH ** -0.5
    for h in range(N_HEAD):
        f_col = fcum[:, N_HEAD + h:N_HEAD + h + 1]
        f_row = zt[N_HEAD + h:N_HEAD + h + 1, :]
        ig_row = zt[h:h + 1, :]
        ig_col = gb[:, h:h + 1]
        m_prev = m_row[:, h:h + 1]
        log_d = jnp.where(causal, f_col - f_row + ig_row, -jnp.inf)
        log_prev = f_col + m_prev
        m_t = jnp.maximum(log_prev, jnp.max(log_d, axis=1, keepdims=True))
        dm = jnp.exp(log_d - m_t)
        prev_scale = jnp.exp(log_prev - m_t)

        sl = slice(h * DH, (h + 1) * DH)
        qh = q_ref[:, sl]
        kh = k_ref[:, sl] * scale
        vh = v_ref[:, sl]
        qb = qh.astype(BF16)
        vb = vh.astype(BF16)
        c_h = c_ref[0, h]
        n_h = n_ref[0, h:h + 1, :]
        qk = _bdot_nt(qb, kh) * dm
        num = _bdot(qk, vb) + prev_scale * _bdot(qb, c_h)
        den = (jnp.sum(qk, axis=1, keepdims=True)
               + prev_scale * jnp.sum(qh * n_h, axis=1, keepdims=True))
        hh = num / jnp.maximum(jnp.abs(den), jnp.exp(-m_t))
        tok_ref[:, sl] = _sigmoid(o_ref[:, sl]) * hh

        m_last = m_t[L - 1:L, :]
        f_last = f_col[L - 1:L, :]
        w_src = jnp.exp(f_last + ig_col - f_col - m_last)
        c_scale = jnp.exp(f_last + m_prev - m_last)
        kw = kh * w_src
        c_ref[0, h] = c_scale * c_h + _bdot(kw.T, vb)
        n_ref[0, h:h + 1, :] = c_scale * n_h + jnp.sum(kw, axis=0, keepdims=True)
        m_new_row = jnp.where(lane1 == h, m_last, m_new_row)
    m_ref[0] = m_new_row


def _mlstm_prompt(proj, bias_row, batch, seq):
    nc = seq // CHUNK
    m_tok = batch * seq

    def col_block(j):
        return pl.BlockSpec((CHUNK, TOK_W), lambda b, c: (b * nc + c, j))

    return pl.pallas_call(
        _mlstm_chunk_kernel,
        grid=(batch, nc),
        in_specs=[col_block(0), col_block(1), col_block(2), col_block(3),
                  pl.BlockSpec((CHUNK, LANES), lambda b, c: (b * nc + c, GATE_COL // LANES)),
                  pl.BlockSpec((1, LANES), lambda b, c: (0, 0))],
        out_specs=[pl.BlockSpec((CHUNK, TOK_W), lambda b, c: (b * nc + c, 0)),
                   pl.BlockSpec((1, N_HEAD, DH, DH), lambda b, c: (b, 0, 0, 0)),
                   pl.BlockSpec((1, N_HEAD, DH), lambda b, c: (b, 0, 0)),
                   pl.BlockSpec((1, 1, LANES), lambda b, c: (b, 0, 0))],
        out_shape=[jax.ShapeDtypeStruct((m_tok, TOK_W), F32),
                   jax.ShapeDtypeStruct((batch, N_HEAD, DH, DH), F32),
                   jax.ShapeDtypeStruct((batch, N_HEAD, DH), F32),
                   jax.ShapeDtypeStruct((batch, 1, LANES), F32)],
        compiler_params=_cparams(("arbitrary", "arbitrary")),
        name="mlstm_chunk",
    )(proj, proj, proj, proj, proj, bias_row)


def _mlstm_step_kernel(q_ref, k_ref, v_ref, o_ref, g_ref, bias_ref, c_ref, n_ref, m_ref,
                       tok_ref, c_out, n_out, m_out):
    gb = g_ref[0] + bias_ref[...]
    m_row = m_ref[0]
    lane1 = lax.broadcasted_iota(jnp.int32, (1, LANES), 1)
    row = lax.broadcasted_iota(jnp.int32, (DH, DH), 0)
    col = lax.broadcasted_iota(jnp.int32, (DH, DH), 1)
    eye = row == col
    m_new_row = m_row
    scale = DH ** -0.5
    for h in range(N_HEAD):
        ig = gb[:, h:h + 1]
        lf = _log_sigmoid(gb[:, N_HEAD + h:N_HEAD + h + 1])
        m_prev = m_row[:, h:h + 1]
        log_prev = lf + m_prev
        m_t = jnp.maximum(log_prev, ig)
        dm = jnp.exp(ig - m_t)
        prev_scale = jnp.exp(log_prev - m_t)

        sl = slice(h * DH, (h + 1) * DH)
        q_row = q_ref[0][:, sl]
        k_row = k_ref[0][:, sl] * scale
        v_row = v_ref[0][:, sl]
        c_h = c_ref[0, 0, h]
        n_h = n_ref[0, 0, h:h + 1, :]
        qk = jnp.sum(q_row * k_row, axis=1, keepdims=True) * dm
        q_c = _dot_hp(jnp.broadcast_to(q_row, (16, DH)), c_h)[0:1, :]
        num = qk * v_row + prev_scale * q_c
        den = qk + prev_scale * jnp.sum(q_row * n_h, axis=1, keepdims=True)
        hh = num / jnp.maximum(jnp.abs(den), jnp.exp(-m_t))
        tok_ref[0, :, sl] = _sigmoid(o_ref[0][:, sl]) * hh

        k_col = jnp.sum(jnp.where(eye, jnp.broadcast_to(k_row, (DH, DH)), 0.0), axis=1, keepdims=True)
        c_out[0, h] = prev_scale * c_h + (dm * k_col) * v_row
        n_out[0, h:h + 1, :] = prev_scale * n_h + dm * k_row
        m_new_row = jnp.where(lane1 == h, m_t, m_new_row)
    m_out[0] = m_new_row


def _mlstm_sample(proj3, bias_row, c_all, n_all, m_pad, li):
    batch = proj3.shape[0]

    def col_block(j):
        return pl.BlockSpec((1, 1, TOK_W), lambda b: (b, 0, j))

    return pl.pallas_call(
        _mlstm_step_kernel,
        grid=(batch,),
        in_specs=[col_block(0), col_block(1), col_block(2), col_block(3),
                  pl.BlockSpec((1, 1, LANES), lambda b: (b, 0, GATE_COL // LANES)),
                  pl.BlockSpec((1, LANES), lambda b: (0, 0)),
                  pl.BlockSpec((1, 1, N_HEAD, DH, DH), lambda b: (li, b, 0, 0, 0)),
                  pl.BlockSpec((1, 1, N_HEAD, DH), lambda b: (li, b, 0, 0)),
                  pl.BlockSpec((1, 1, LANES), lambda b: (b, 0, 0))],
        out_specs=[pl.BlockSpec((1, 1, TOK_W), lambda b: (b, 0, 0)),
                   pl.BlockSpec((1, N_HEAD, DH, DH), lambda b: (b, 0, 0, 0)),
                   pl.BlockSpec((1, N_HEAD, DH), lambda b: (b, 0, 0)),
                   pl.BlockSpec((1, 1, LANES), lambda b: (b, 0, 0))],
        out_shape=[jax.ShapeDtypeStruct((batch, 1, TOK_W), F32),
                   jax.ShapeDtypeStruct((batch, N_HEAD, DH, DH), F32),
                   jax.ShapeDtypeStruct((batch, N_HEAD, DH), F32),
                   jax.ShapeDtypeStruct((batch, 1, LANES), F32)],
        compiler_params=_cparams(("arbitrary",)),
        name="mlstm_step",
    )(proj3, proj3, proj3, proj3, proj3, bias_row, c_all, n_all, m_pad)


def _mem_attn_kernel(q_ref, k_ref, v_ref, o_ref):
    outs = []
    for h in range(H_M):
        sl = slice(h * DH_M, (h + 1) * DH_M)
        s = _bdot_nt(q_ref[:, sl], k_ref[0][:, sl]) * (DH_M ** -0.5)
        e = jnp.exp(s - jnp.max(s, axis=1, keepdims=True))
        p = e / jnp.sum(e, axis=1, keepdims=True)
        outs.append(_bdot(p, v_ref[0][:, sl]))
    o_ref[...] = jnp.concatenate(outs, axis=-1)


def _mem_attn_prompt(proj, q_col_block, mem_k, mem_v, seq, tq):
    m_tok = proj.shape[0]
    per_b = seq // tq
    return pl.pallas_call(
        _mem_attn_kernel,
        grid=(m_tok // tq,),
        in_specs=[pl.BlockSpec((tq, MEM_W), lambda i: (i, q_col_block)),
                  pl.BlockSpec((1, N_MEM, MEM_W), lambda i: (i // per_b, 0, 0)),
                  pl.BlockSpec((1, N_MEM, MEM_W), lambda i: (i // per_b, 0, 0))],
        out_specs=pl.BlockSpec((tq, MEM_W), lambda i: (i, 0)),
        out_shape=jax.ShapeDtypeStruct((m_tok, MEM_W), F32),
        compiler_params=_cparams(("arbitrary",)),
        name="mem_attn",
    )(proj, mem_k, mem_v)


MEM_STEP_BATCH = 4


def _mem_attn_step_kernel(q_ref, k_ref, v_ref, o_ref):
    rows = 16
    rid = lax.broadcasted_iota(jnp.int32, (rows, MEM_W), 0)
    lane = lax.broadcasted_iota(jnp.int32, (rows, MEM_W), 1)
    head_mask = (lane // DH_M) == rid
    for g in range(MEM_STEP_BATCH):
        q = jnp.broadcast_to(q_ref[g], (rows, MEM_W))
        qm = jnp.where(head_mask, q, 0.0)
        s = _dot_hp(qm, k_ref[0, g]) * (DH_M ** -0.5)
        e = jnp.exp(s - jnp.max(s, axis=1, keepdims=True))
        p = e / jnp.sum(e, axis=1, keepdims=True)
        o = _dot_hp(p, v_ref[0, g], nt=True)
        o_ref[g] = jnp.sum(jnp.where(head_mask, o, 0.0), axis=0, keepdims=True)


def _mem_attn_sample(mq3, cache_k, cache_v, li):
    batch = mq3.shape[0]
    gb = MEM_STEP_BATCH
    return pl.pallas_call(
        _mem_attn_step_kernel,
        grid=(batch // gb,),
        in_specs=[pl.BlockSpec((gb, 1, MEM_W), lambda b: (b, 0, 0)),
                  pl.BlockSpec((1, gb, N_MEM, MEM_W), lambda b: (li, b, 0, 0)),
                  pl.BlockSpec((1, gb, N_MEM, MEM_W), lambda b: (li, b, 0, 0))],
        out_specs=pl.BlockSpec((gb, 1, MEM_W), lambda b: (b, 0, 0)),
        out_shape=jax.ShapeDtypeStruct((batch, 1, MEM_W), F32),
        compiler_params=_cparams(("arbitrary",)),
        name="mem_attn_step",
    )(mq3, cache_k, cache_v)


def _routing(logits):
    lane = lax.broadcasted_iota(jnp.int32, logits.shape, 1)
    lane_f = lane.astype(F32)
    big = 1000.0
    is_g = (lane >= N_EXPERTS) & (lane < N_EXPERTS + 4)
    lg = jnp.where(is_g, logits, -jnp.inf)
    gmax = jnp.max(lg, axis=1, keepdims=True)
    gidx = jnp.min(jnp.where(lg == gmax, lane_f, big), axis=1, keepdims=True) - float(N_EXPERTS)
    p_g = 1.0 / jnp.sum(jnp.exp(lg - gmax), axis=1, keepdims=True)
    in_grp = (lane < N_EXPERTS) & ((lane >> 2).astype(F32) == gidx)
    le = jnp.where(in_grp, logits, -jnp.inf)
    v1 = jnp.max(le, axis=1, keepdims=True)
    i1 = jnp.min(jnp.where(le == v1, lane_f, big), axis=1, keepdims=True)
    le2 = jnp.where(lane_f == i1, -jnp.inf, le)
    v2 = jnp.max(le2, axis=1, keepdims=True)
    i2 = jnp.min(jnp.where(le2 == v2, lane_f, big), axis=1, keepdims=True)
    e2 = jnp.exp(v2 - v1)
    inv = 1.0 / (1.0 + e2)
    comb = jnp.where(lane_f == i1, inv * p_g, jnp.where(lane_f == i2, e2 * inv * p_g, 0.0))
    return comb, gidx


GROUP_LANE = 16
RANK_LANE = 17


def _mix_kernel(x_ref, tok_ref, mem_ref, wt_ref, wm_ref, g_ref, b_ref, wr_ref, br_ref,
                x1_ref, comb_ref, cnt_ref, *, hp):
    @pl.when(pl.program_id(0) == 0)
    def _():
        cnt_ref[...] = jnp.zeros_like(cnt_ref)

    dot = _dot_hp if hp else _bdot
    mix = dot(tok_ref[...], wt_ref[...]) + dot(mem_ref[...], wm_ref[...])
    x1 = _layer_norm(ALPHA * x_ref[...] + mix, g_ref[...], b_ref[...])
    x1_ref[...] = x1
    logits = dot(x1, wr_ref[...]) + br_ref[...]
    comb, gidx = _routing(logits)
    tm = comb.shape[0]
    lane_f = lax.broadcasted_iota(jnp.int32, comb.shape, 1).astype(F32)
    mine = lane_f == gidx
    onehot = jnp.where(mine, 1.0, 0.0)
    row = lax.broadcasted_iota(jnp.int32, (tm, tm), 0)
    col = lax.broadcasted_iota(jnp.int32, (tm, tm), 1)
    before = jnp.where(col < row, 1.0, 0.0).astype(BF16)
    prefix = jnp.dot(before, onehot.astype(BF16), preferred_element_type=F32)
    cnt = cnt_ref[...]
    rank = jnp.sum(jnp.where(mine, prefix + cnt, 0.0), axis=1, keepdims=True)
    cnt_ref[...] = cnt + jnp.sum(onehot, axis=0, keepdims=True)
    comb_ref[...] = jnp.where(lane_f == float(GROUP_LANE), gidx,
                              jnp.where(lane_f == float(RANK_LANE), rank, comb))


def _mix(x, tok, mem, wt, wm, g, b, wr, br, tm, hp=False):
    m_tok = x.shape[0]
    const = lambda i: (0, 0)
    rows = lambda i: (i, 0)
    return pl.pallas_call(
        functools.partial(_mix_kernel, hp=hp),
        grid=(m_tok // tm,),
        in_specs=[pl.BlockSpec((tm, D_MODEL), rows),
                  pl.BlockSpec((tm, TOK_W), rows),
                  pl.BlockSpec((tm, MEM_W), rows),
                  pl.BlockSpec((TOK_W, D_MODEL), const),
                  pl.BlockSpec((MEM_W, D_MODEL), const),
                  pl.BlockSpec((1, D_MODEL), const),
                  pl.BlockSpec((1, D_MODEL), const),
                  pl.BlockSpec((D_MODEL, LANES), const),
                  pl.BlockSpec((1, LANES), const)],
        out_specs=[pl.BlockSpec((tm, D_MODEL), rows),
                   pl.BlockSpec((tm, LANES), rows),
                   pl.BlockSpec((1, LANES), const)],
        out_shape=[jax.ShapeDtypeStruct((m_tok, D_MODEL), F32),
                   jax.ShapeDtypeStruct((m_tok, LANES), F32),
                   jax.ShapeDtypeStruct((1, LANES), F32)],
        compiler_params=_cparams(("arbitrary",)),
        name="mix",
    )(x, tok, mem, wt, wm, g, b, wr, br)


def _moe_kernel(x_ref, comb_ref, w1_ref, w3_ref, w2_ref, g_ref, b_ref, o_ref, xb_ref, acc_ref,
                *, hp):
    e = pl.program_id(1)

    @pl.when(e == 0)
    def _():
        xb_ref[...] = x_ref[...].astype(BF16)
        acc_ref[...] = jnp.zeros_like(acc_ref)

    if hp:
        x = x_ref[...]
        a = _dot_hp(x, w1_ref[0, 0])
        b = _dot_hp(x, w3_ref[0, 0])
    else:
        xb = xb_ref[...]
        a = jnp.dot(xb, w1_ref[0, 0].astype(BF16), preferred_element_type=F32)
        b = jnp.dot(xb, w3_ref[0, 0].astype(BF16), preferred_element_type=F32)
    comb = comb_ref[...]
    lane = lax.broadcasted_iota(jnp.int32, comb.shape, 1)
    ce = jnp.sum(jnp.where(lane == e, comb, 0.0), axis=1, keepdims=True)
    hid = (a * _sigmoid(a)) * b * ce
    if hp:
        acc_ref[...] += _dot_hp(hid, w2_ref[0, 0])
    else:
        acc_ref[...] += jnp.dot(hid.astype(BF16), w2_ref[0, 0].astype(BF16),
                                preferred_element_type=F32)

    @pl.when(e == N_EXPERTS - 1)
    def _():
        o_ref[...] = _layer_norm(ALPHA * x_ref[...] + acc_ref[...], g_ref[...], b_ref[...])


def _moe(x1, comb, w1, w3, w2, g, b, li, tm, hp=False):
    m_tok = x1.shape[0]
    rows = lambda i, e: (i, 0)
    const = lambda i, e: (0, 0)
    return pl.pallas_call(
        functools.partial(_moe_kernel, hp=hp),
        grid=(m_tok // tm, N_EXPERTS),
        in_specs=[pl.BlockSpec((tm, D_MODEL), rows),
                  pl.BlockSpec((tm, LANES), rows),
                  pl.BlockSpec((1, 1, D_MODEL, D_EXPERT), lambda i, e: (li, e, 0, 0)),
                  pl.BlockSpec((1, 1, D_MODEL, D_EXPERT), lambda i, e: (li, e, 0, 0)),
                  pl.BlockSpec((1, 1, D_EXPERT, D_MODEL), lambda i, e: (li, e, 0, 0)),
                  pl.BlockSpec((1, D_MODEL), const),
                  pl.BlockSpec((1, D_MODEL), const)],
        out_specs=pl.BlockSpec((tm, D_MODEL), rows),
        out_shape=jax.ShapeDtypeStruct((m_tok, D_MODEL), F32),
        scratch_shapes=[pltpu.VMEM((tm, D_MODEL), BF16), pltpu.VMEM((tm, D_MODEL), F32)],
        compiler_params=_cparams(("arbitrary", "arbitrary")),
        name="moe",
    )(x1, comb, w1, w3, w2, g, b)


SLOT_TILE = 256
N_GROUPS = 4
EXP_PER_GROUP = 4
MOE_VMEM_LIMIT = 56 * 1024 * 1024


N_XBUF = 3


def _dispatch_kernel(dest_ref, fill_ref, x_ref, xs_ref, inv_ref, xbuf, zrow, load_sem, sem, sem_fill,
                     *, tm, n_steps, n_fill):
    i = pl.program_id(0)
    base = i * tm
    slot = i % N_XBUF

    def load(step, buf):
        return pltpu.make_async_copy(x_ref.at[pl.ds(step * tm, tm)], xbuf.at[buf], load_sem.at[buf])

    @pl.when(i == 0)
    def _():
        load(0, 0).start()

    @pl.when(i + 1 < n_steps)
    def _():
        load(i + 1, (i + 1) % N_XBUF).start()

    load(i, slot).wait()

    par = i % 2

    def body(r, carry):
        d = dest_ref[base + r]
        pltpu.make_async_copy(xbuf.at[slot, pl.ds(r, 1)], xs_ref.at[pl.ds(d, 1)], sem.at[par]).start()
        inv_ref[d] = base + r
        return carry

    lax.fori_loop(0, tm, body, 0, unroll=8)

    def wait_step(p):
        pltpu.make_async_copy(xbuf.at[0], xs_ref.at[pl.ds(0, tm)], sem.at[p]).wait()

    @pl.when(i > 0)
    def _():
        wait_step(1 - par)

    @pl.when(i == n_steps - 1)
    def _():
        wait_step(par)
        zrow[...] = jnp.zeros_like(zrow)

        def fill(s, carry):
            pltpu.make_async_copy(zrow.at[pl.ds(0, 1)], xs_ref.at[pl.ds(s, 1)], sem_fill).start()
            inv_ref[s] = -1
            return carry

        for k in range(N_GROUPS + 1):
            lax.fori_loop(fill_ref[2 * k], fill_ref[2 * k + 1], fill, 0)
        pltpu.make_async_copy(xs_ref.at[pl.ds(0, n_fill)], xs_ref.at[pl.ds(0, n_fill)],
                              sem_fill).wait()


def _dispatch(dest, fill, x1, n_slots, tm):
    m_tok = x1.shape[0]
    n_steps = m_tok // tm
    grid_spec = pltpu.PrefetchScalarGridSpec(
        num_scalar_prefetch=2,
        grid=(n_steps,),
        in_specs=[pl.BlockSpec(memory_space=pl.ANY)],
        out_specs=[pl.BlockSpec(memory_space=pl.ANY),
                   pl.BlockSpec(memory_space=pltpu.SMEM)],
        scratch_shapes=[pltpu.VMEM((N_XBUF, tm, D_MODEL), F32),
                        pltpu.VMEM((8, D_MODEL), F32),
                        pltpu.SemaphoreType.DMA((N_XBUF,)),
                        pltpu.SemaphoreType.DMA((2,)),
                        pltpu.SemaphoreType.DMA(())],
    )
    return pl.pallas_call(
        functools.partial(_dispatch_kernel, tm=tm, n_steps=n_steps, n_fill=n_slots - m_tok),
        grid_spec=grid_spec,
        out_shape=[jax.ShapeDtypeStruct((n_slots, D_MODEL), F32),
                   jax.ShapeDtypeStruct((n_slots,), jnp.int32)],
        compiler_params=_cparams(("arbitrary",)),
        name="moe_dispatch",
    )(dest, fill, x1)


def _moe_sparse_kernel(blk_ref, grp_ref, val_ref, nrow_ref, inv_ref,
                       xs_ref, w1_ref, w3_ref, w2_ref, wr_ref, br_ref, g_ref, b_ref,
                       out_ref, w1b, w3b, w2b, ybuf, junk, state, sem, *, n_tiles):
    i = pl.program_id(0)
    ts = SLOT_TILE

    @pl.when(i == 0)
    def _():
        state[0] = -1
        state[1] = 0

    def wait_pending():
        @pl.when(state[1] == 1)
        def _():
            pltpu.make_async_copy(ybuf, junk, sem).wait()

    @pl.when(val_ref[i] == 1)
    def _():
        grp = grp_ref[i]

        @pl.when(grp != state[0])
        def _():
            for j in range(EXP_PER_GROUP):
                cols = slice(j * D_EXPERT, (j + 1) * D_EXPERT)
                w1b[:, cols] = w1_ref[0, 0, j].astype(BF16)
                w3b[:, cols] = w3_ref[0, 0, j].astype(BF16)
                w2b[cols, :] = w2_ref[0, 0, j].astype(BF16)
            state[0] = grp

        x = xs_ref[...]
        xb = x.astype(BF16)
        logits = jnp.dot(xb, wr_ref[...], preferred_element_type=F32) + br_ref[...]
        comb, _ = _routing(logits)
        lane = lax.broadcasted_iota(jnp.int32, comb.shape, 1)
        a = jnp.dot(xb, w1b[...], preferred_element_type=F32)
        b = jnp.dot(xb, w3b[...], preferred_element_type=F32)
        hid = []
        for j in range(EXP_PER_GROUP):
            cols = slice(j * D_EXPERT, (j + 1) * D_EXPERT)
            ce = jnp.sum(jnp.where(lane == grp * EXP_PER_GROUP + j, comb, 0.0), axis=1, keepdims=True)
            a_j = a[:, cols]
            hid.append(((a_j * _sigmoid(a_j)) * b[:, cols] * ce).astype(BF16))
        acc = jnp.dot(jnp.concatenate(hid, axis=1), w2b[...], preferred_element_type=F32)
        x2 = _layer_norm(ALPHA * x + acc, g_ref[...], b_ref[...])

        wait_pending()
        ybuf[...] = x2
        n = nrow_ref[i]
        base = blk_ref[i] * ts

        def body(r, carry):
            t = inv_ref[base + r]
            pltpu.make_async_copy(ybuf.at[pl.ds(r, 1)], out_ref.at[pl.ds(t, 1)], sem).start()
            return carry

        def pad_body(r, carry):
            pltpu.make_async_copy(ybuf.at[pl.ds(r, 1)], junk.at[pl.ds(r, 1)], sem).start()
            return carry

        full = n // 8

        def body8(gi, carry):
            for u in range(8):
                body(gi * 8 + u, carry)
            return carry

        lax.fori_loop(0, full, body8, 0)
        lax.fori_loop(full * 8, n, body, 0)
        lax.fori_loop(n, ts, pad_body, 0)
        state[1] = 1

    @pl.when(i == n_tiles - 1)
    def _():
        wait_pending()
        state[1] = 0


def _moe_sparse(x1, comb, cnt, P, W, li):
    m_tok = x1.shape[0]
    ts = SLOT_TILE
    n_tiles = m_tok // ts + N_GROUPS
    n_slots = n_tiles * ts
    grp_tok = comb[:, GROUP_LANE].astype(jnp.int32)
    rank_tok = comb[:, RANK_LANE].astype(jnp.int32)
    counts = cnt[0, :N_GROUPS].astype(jnp.int32)
    tiles_g = (counts + ts - 1) // ts
    tile_end = jnp.cumsum(tiles_g)
    slot_base = (tile_end - tiles_g) * ts
    dest = jnp.take(slot_base, grp_tok) + rank_tok
    total = tile_end[-1]
    tile_i = jnp.arange(n_tiles, dtype=jnp.int32)
    valid = (tile_i < total).astype(jnp.int32)
    blk = jnp.minimum(tile_i, total - 1)
    grp_tile = jnp.minimum(jnp.sum((tile_end[None, :] <= blk[:, None]).astype(jnp.int32), axis=1),
                           N_GROUPS - 1)
    n_rows = jnp.clip(jnp.take(slot_base + counts, grp_tile) - blk * ts, 0, ts) * valid

    fill_lo = jnp.concatenate([slot_base + counts, (total * ts)[None]])
    fill_hi = jnp.concatenate([slot_base + tiles_g * ts, jnp.full((1,), n_slots, jnp.int32)])
    fill = jnp.stack([fill_lo, fill_hi], axis=1).reshape(-1).astype(jnp.int32)

    xs, inv = _dispatch(dest, fill, x1, n_slots, 512)

    w1 = P['w_e1'].reshape(DEPTH, N_GROUPS, EXP_PER_GROUP, D_MODEL, D_EXPERT)
    w3 = P['w_e3'].reshape(DEPTH, N_GROUPS, EXP_PER_GROUP, D_MODEL, D_EXPERT)
    w2 = P['w_e2'].reshape(DEPTH, N_GROUPS, EXP_PER_GROUP, D_EXPERT, D_MODEL)
    const = lambda i, *_: (0, 0)
    grid_spec = pltpu.PrefetchScalarGridSpec(
        num_scalar_prefetch=5,
        grid=(n_tiles,),
        in_specs=[pl.BlockSpec((ts, D_MODEL), lambda i, blk, grp, val, nrow, inv: (blk[i], 0)),
                  pl.BlockSpec((1, 1, EXP_PER_GROUP, D_MODEL, D_EXPERT),
                               lambda i, blk, grp, val, nrow, inv: (li, grp[i], 0, 0, 0)),
                  pl.BlockSpec((1, 1, EXP_PER_GROUP, D_MODEL, D_EXPERT),
                               lambda i, blk, grp, val, nrow, inv: (li, grp[i], 0, 0, 0)),
                  pl.BlockSpec((1, 1, EXP_PER_GROUP, D_EXPERT, D_MODEL),
                               lambda i, blk, grp, val, nrow, inv: (li, grp[i], 0, 0, 0)),
                  pl.BlockSpec((D_MODEL, LANES), const),
                  pl.BlockSpec((1, LANES), const),
                  pl.BlockSpec((1, D_MODEL), const),
                  pl.BlockSpec((1, D_MODEL), const)],
        out_specs=pl.BlockSpec(memory_space=pl.ANY),
        scratch_shapes=[pltpu.VMEM((D_MODEL, EXP_PER_GROUP * D_EXPERT), BF16),
                        pltpu.VMEM((D_MODEL, EXP_PER_GROUP * D_EXPERT), BF16),
                        pltpu.VMEM((EXP_PER_GROUP * D_EXPERT, D_MODEL), BF16),
                        pltpu.VMEM((ts, D_MODEL), F32),
                        pltpu.VMEM((ts, D_MODEL), F32),
                        pltpu.SMEM((2,), jnp.int32),
                        pltpu.SemaphoreType.DMA(())],
    )
    return pl.pallas_call(
        functools.partial(_moe_sparse_kernel, n_tiles=n_tiles),
        grid_spec=grid_spec,
        out_shape=jax.ShapeDtypeStruct((m_tok, D_MODEL), F32),
        compiler_params=pltpu.CompilerParams(dimension_semantics=("arbitrary",),
                                             vmem_limit_bytes=MOE_VMEM_LIMIT),
        name="moe_sparse",
    )(blk, grp_tile, valid, n_rows, inv, xs, w1, w3, w2, W['w_r'][li], W['b_r'][li],
      P['ln_ffn_g'][li][None, :], P['ln_ffn_b'][li][None, :])


def _diff_lambda(lq_ref, lam_init):
    lq = lq_ref[0]
    a = jnp.sum(lq[0:1, :] * lq[1:2, :], axis=1, keepdims=True)
    b = jnp.sum(lq[2:3, :] * lq[3:4, :], axis=1, keepdims=True)
    return jnp.exp(a) - jnp.exp(b) + lam_init


def _sub_norm(o, g_row, lam_init):
    o = o * lax.rsqrt(jnp.mean(o * o, axis=-1, keepdims=True) + LN_EPS) * g_row
    return o * (1.0 - lam_init)


def _diff_attn_kernel(q1_ref, q2_ref, k1_ref, k2_ref, vt_ref, lq_ref, gs_ref, o_ref,
                      m_scr, l_scr, acc_scr, *, tq, lam_init):
    i = pl.program_id(2)
    lam = _diff_lambda(lq_ref, lam_init)
    q_scale = (DQK ** -0.5) * math.log2(math.e)
    qs = [(q1_ref[...] * q_scale).astype(BF16), (q2_ref[...] * q_scale).astype(BF16)]
    k_refs = [k1_ref, k2_ref]
    m_scr[...] = jnp.full_like(m_scr, -jnp.inf)
    l_scr[...] = jnp.zeros_like(l_scr)
    acc_scr[...] = jnp.zeros_like(acc_scr)
    key_i = lax.broadcasted_iota(jnp.int32, (tq, tq), 0)
    qry_i = lax.broadcasted_iota(jnp.int32, (tq, tq), 1)
    causal = key_i <= qry_i

    def block(j, masked):
        start = pl.multiple_of(j * tq, tq)
        sts = []
        for hh in range(2):
            for mm in range(2):
                kb = k_refs[mm][pl.ds(start, tq), hh * DQK:(hh + 1) * DQK]
                st = lax.dot_general(kb, qs[mm][:, hh * DQK:(hh + 1) * DQK],
                                     (((1,), (1,)), ((), ())), preferred_element_type=F32)
                if masked:
                    st = jnp.where(causal, st, -jnp.inf)
                sts.append(st)
        m_all = m_scr[...]
        l_all = l_scr[...]
        ps, alphas, m_rows, l_rows = [], [], [], []
        for idx in range(4):
            m_old = m_all[idx:idx + 1, :]
            m_new = jnp.maximum(m_old, jnp.max(sts[idx], axis=0, keepdims=True))
            alpha = jnp.exp2(m_old - m_new)
            p = jnp.exp2(sts[idx] - m_new)
            l_rows.append(alpha * l_all[idx:idx + 1, :] + jnp.sum(p, axis=0, keepdims=True))
            m_rows.append(m_new)
            alphas.append(alpha)
            ps.append(p.astype(BF16))
        m_scr[...] = jnp.concatenate(m_rows + [m_all[4:, :]], axis=0)
        l_scr[...] = jnp.concatenate(l_rows + [l_all[4:, :]], axis=0)
        for idx in range(4):
            vt = vt_ref[j, (idx // 2) * DH:(idx // 2 + 1) * DH, :]
            acc_scr[idx] = alphas[idx] * acc_scr[idx] + jnp.dot(vt, ps[idx],
                                                                preferred_element_type=F32)

    def body(j, carry):
        block(j, False)
        return carry

    lax.fori_loop(0, i, body, 0)
    block(i, True)

    for hh in range(2):
        o1 = acc_scr[hh * 2] / l_scr[hh * 2:hh * 2 + 1, :]
        o2 = acc_scr[hh * 2 + 1] / l_scr[hh * 2 + 1:hh * 2 + 2, :]
        o = o1 - lam * o2
        o = o * lax.rsqrt(jnp.mean(o * o, axis=0, keepdims=True) + LN_EPS)
        o_ref[:, hh * DH:(hh + 1) * DH] = o.T * gs_ref[...] * (1.0 - lam_init)


def _diff_attn_prompt(proj_b, k_b, vt_b, lam_qk, g_sub, j, lam_init, batch, seq, tq):
    m_tok = batch * seq
    nq = seq // tq
    kern = functools.partial(_diff_attn_kernel, tq=tq, lam_init=lam_init)
    return pl.pallas_call(
        kern,
        grid=(batch, 3, nq),
        in_specs=[pl.BlockSpec((tq, 2 * DQK), lambda b, hp, i: (b * nq + i, hp)),
                  pl.BlockSpec((tq, 2 * DQK), lambda b, hp, i: (b * nq + i, 3 + hp)),
                  pl.BlockSpec((seq, 2 * DQK), lambda b, hp, i: (b, hp)),
                  pl.BlockSpec((seq, 2 * DQK), lambda b, hp, i: (b, 3 + hp)),
                  pl.BlockSpec((nq, 2 * DH, tq), lambda b, hp, i: (b, hp, 0)),
                  pl.BlockSpec((1, 4, DQK), lambda b, hp, i: (j, 0, 0)),
                  pl.BlockSpec((1, DH), lambda b, hp, i: (0, 0))],
        out_specs=pl.BlockSpec((tq, 2 * DH), lambda b, hp, i: (b * nq + i, hp)),
        out_shape=jax.ShapeDtypeStruct((m_tok, TOK_W), F32),
        scratch_shapes=[pltpu.VMEM((8, tq), F32), pltpu.VMEM((8, tq), F32),
                        pltpu.VMEM((4, DH, tq), F32)],
        compiler_params=_cparams(("arbitrary", "arbitrary", "arbitrary")),
        name="diff_attn",
    )(proj_b, proj_b, k_b, k_b, vt_b, lam_qk, g_sub)


PAGES_PER_STEP = 16


def _diff_attn_step_kernel(pt_ref, q_ref, kn_ref, vn_ref, *rest, n_steps, lam_init):
    k_refs = rest[:PAGES_PER_STEP]
    v_refs = rest[PAGES_PER_STEP:2 * PAGES_PER_STEP]
    lq_ref, gs_ref, o_ref, qm_scr, m_scr, l_scr, acc_scr = rest[2 * PAGES_PER_STEP:]
    p_idx = pl.program_id(1)
    rows = 16
    rid = lax.broadcasted_iota(jnp.int32, (rows, TOK_W), 0)
    lane = lax.broadcasted_iota(jnp.int32, (rows, TOK_W), 1)

    @pl.when(p_idx == 0)
    def _():
        q = jnp.broadcast_to(q_ref[0] * (DQK ** -0.5), (rows, TOK_W))
        qm_scr[...] = jnp.where((lane // DQK) == rid, q, 0.0)
        m_scr[...] = jnp.full_like(m_scr, -jnp.inf)
        l_scr[...] = jnp.zeros_like(l_scr)
        acc_scr[...] = jnp.zeros_like(acc_scr)

    qm = qm_scr[...]
    q_hi, q_lo = _split2(qm)
    q2 = jnp.concatenate([q_hi, q_lo], axis=0)
    s_pages = []
    for k_ref in k_refs:
        k_hi, k_lo = _split2(k_ref[0])
        r1 = _bdot(q2, k_hi)
        s_pages.append((_bdot(q_hi, k_lo) + r1[rows:]) + r1[:rows])
    s = jnp.concatenate(s_pages, axis=1)
    m_old = m_scr[...]
    m_new = jnp.maximum(m_old, jnp.max(s, axis=1, keepdims=True))
    alpha = jnp.exp(m_old - m_new)
    p = jnp.exp(s - m_new)
    p_hi, p_lo = _split2(p)
    p2 = jnp.concatenate([p_hi, p_lo], axis=0)
    pv = []
    for h in range(N_HEAD):
        pv_h = None
        for g in range(PAGES_PER_STEP):
            cols = slice(g * PAGE, (g + 1) * PAGE)
            v_hi, v_lo = _split2(v_refs[g][0, h])
            r1 = _bdot(p2[:, cols], v_hi)
            term = (_bdot(p_hi[:, cols], v_lo) + r1[rows:]) + r1[:rows]
            pv_h = term if pv_h is None else pv_h + term
        pv.append(pv_h)
    l_scr[...] = alpha * l_scr[...] + jnp.sum(p, axis=1, keepdims=True)
    acc_scr[...] = alpha * acc_scr[...] + jnp.concatenate(pv, axis=1)
    m_scr[...] = m_new

    @pl.when(p_idx == n_steps - 1)
    def _():
        lam = _diff_lambda(lq_ref, lam_init)
        s_new = jnp.sum(qm * kn_ref[0], axis=1, keepdims=True)
        m_old2 = m_scr[...]
        m_fin = jnp.maximum(m_old2, s_new)
        a2 = jnp.exp(m_old2 - m_fin)
        p_new = jnp.exp(s_new - m_fin)
        l_fin = a2 * l_scr[...] + p_new
        acc = a2 * acc_scr[...] + p_new * vn_ref[0]
        r = acc / l_fin
        head_of_lane = lane // DH
        o1 = jnp.sum(jnp.where(head_of_lane == rid, r, 0.0), axis=0, keepdims=True)
        o2 = jnp.sum(jnp.where(head_of_lane == rid - N_HEAD, r, 0.0), axis=0, keepdims=True)
        o = o1 - lam * o2
        for h in range(N_HEAD):
            sl = slice(h * DH, (h + 1) * DH)
            o_ref[0, :, sl] = _sub_norm(o[:, sl], gs_ref[...], lam_init)


def _diff_attn_sample(page_table, q3, kn3, vn3, cache_k, cache_v, lam_qk, g_sub, j, lam_init):
    batch, n_pages = page_table.shape
    n_steps = n_pages // PAGES_PER_STEP
    kern = functools.partial(_diff_attn_step_kernel, n_steps=n_steps, lam_init=lam_init)
    row3 = lambda b, p, pt: (b, 0, 0)

    def k_spec(g):
        return pl.BlockSpec((1, TOK_W, PAGE), lambda b, p, pt: (pt[b, p * PAGES_PER_STEP + g], 0, 0))

    def v_spec(g):
        return pl.BlockSpec((1, N_HEAD, PAGE, DH),
                            lambda b, p, pt: (pt[b, p * PAGES_PER_STEP + g], 0, 0, 0))

    k_specs = [k_spec(g) for g in range(PAGES_PER_STEP)]
    v_specs = [v_spec(g) for g in range(PAGES_PER_STEP)]
    grid_spec = pltpu.PrefetchScalarGridSpec(
        num_scalar_prefetch=1,
        grid=(batch, n_steps),
        in_specs=[pl.BlockSpec((1, 1, TOK_W), row3),
                  pl.BlockSpec((1, 1, TOK_W), row3),
                  pl.BlockSpec((1, 1, TOK_W), row3)] + k_specs + v_specs + [
                  pl.BlockSpec((1, 4, DQK), lambda b, p, pt: (j, 0, 0)),
                  pl.BlockSpec((1, DH), lambda b, p, pt: (0, 0))],
        out_specs=pl.BlockSpec((1, 1, TOK_W), row3),
        scratch_shapes=[pltpu.VMEM((16, TOK_W), F32), pltpu.VMEM((16, 1), F32),
                        pltpu.VMEM((16, 1), F32), pltpu.VMEM((16, TOK_W), F32)],
    )
    return pl.pallas_call(
        kern,
        grid_spec=grid_spec,
        out_shape=jax.ShapeDtypeStruct((batch, 1, TOK_W), F32),
        compiler_params=_cparams(("arbitrary", "arbitrary")),
        name="diff_attn_step",
    )(page_table, q3, kn3, vn3, *([cache_k] * PAGES_PER_STEP), *([cache_v] * PAGES_PER_STEP),
      lam_qk, g_sub)


def _diff_lambda_init(li):
    return 0.8 - 0.6 * math.exp(-0.3 * li)


def _prep_weights(P):
    w = {}
    wa = P['w_in_a']
    gates = jnp.pad(wa[:, :, 4 * TOK_W:4 * TOK_W + 2 * N_HEAD], ((0, 0), (0, 0), (0, LANES - 2 * N_HEAD)))
    w['w_a32'] = jnp.concatenate([wa[:, :, :4 * TOK_W], wa[:, :, 4 * TOK_W + 2 * N_HEAD:], gates],
                                 axis=-1)
    w['w_a'] = w['w_a32'].astype(BF16)
    w['gate_bias'] = jnp.pad(jnp.concatenate([P['b_igate'], P['b_fgate']], axis=-1),
                             ((0, 0), (0, LANES - 2 * N_HEAD)))[:, None, :]
    w['w_b'] = P['w_in_b'].astype(BF16)
    w['w_kv'] = P['w_kv_shared'].astype(BF16)
    w['w_mem'] = jnp.transpose(P['w_mem_kv'], (1, 0, 2)).reshape(D_MODEL, DEPTH * 2 * MEM_W).astype(BF16)
    w['w_out_t32'] = P['w_out'][:, :TOK_W, :]
    w['w_out_m32'] = P['w_out'][:, TOK_W:, :]
    w['w_out_t'] = w['w_out_t32'].astype(BF16)
    w['w_out_m'] = w['w_out_m32'].astype(BF16)
    wr = jnp.concatenate([P['w_router'], P['w_group']], axis=-1)
    w['w_r32'] = jnp.pad(wr, ((0, 0), (0, 0), (0, LANES - wr.shape[-1])))
    w['w_r'] = w['w_r32'].astype(BF16)
    br = jnp.concatenate([P['b_router'], P['b_group']], axis=-1)
    w['b_r'] = jnp.pad(br, ((0, 0), (0, LANES - br.shape[-1])))[:, None, :]
    return w


def _layer_tail(x, tok, mem, li, P, W, tm_mix, tm_moe, hp=False):
    sfx = '32' if hp else ''
    x1, comb, cnt = _mix(x, tok, mem, W['w_out_t' + sfx][li], W['w_out_m' + sfx][li],
                         P['ln_mix_g'][li][None, :], P['ln_mix_b'][li][None, :],
                         W['w_r' + sfx][li], W['b_r'][li], tm_mix, hp)
    if hp:
        return _moe(x1, comb, P['w_e1'], P['w_e3'], P['w_e2'],
                    P['ln_ffn_g'][li][None, :], P['ln_ffn_b'][li][None, :], li, tm_moe, hp)
    return _moe_sparse(x1, comb, cnt, P, W, li)


def _prompt_trunk(x_prompt, mem_prompt, P, W):
    batch, seq, _ = x_prompt.shape
    m_tok = batch * seq
    x = x_prompt.reshape(m_tok, D_MODEL)
    mem = mem_prompt.reshape(batch * N_MEM, D_MODEL)
    kv_mem = _proj(mem, W['w_mem'], 512)
    kv_mem = jnp.transpose(kv_mem.reshape(batch, N_MEM, DEPTH, 2, MEM_W), (3, 2, 0, 1, 4))
    mem_k, mem_v = kv_mem[0], kv_mem[1]

    new_c, new_n, new_m = [], [], []
    k_t = v_h = k16 = vt16 = None
    for li in range(DEPTH):
        if li < N_A:
            proj = _proj(x, W['w_a'][li], 256)
            tok, c, n, m = _mlstm_prompt(proj, W['gate_bias'][li], batch, seq)
            new_c.append(c)
            new_n.append(n)
            new_m.append(m[:, 0, :N_HEAD])
            mq_block = (4 * TOK_W) // MEM_W
        else:
            if li == N_A:
                k_t, v_h, k16, vt16 = _proj_shared_kv(x, W['w_kv'], batch, seq, 256)
            j = li - N_A
            proj = _proj(x, W['w_b'][j], 512)
            tok = _diff_attn_prompt(proj, k16, vt16, P['lambda_qk'], P['subln_g'][j:j + 1], j,
                                    _diff_lambda_init(li), batch, seq, 256)
            mq_block = TOK_W // MEM_W
        mem_o = _mem_attn_prompt(proj, mq_block, mem_k[li], mem_v[li], seq, 512)
        x = _layer_tail(x, tok, mem_o, li, P, W, 512, 1024)
    y = x.reshape(batch, seq, D_MODEL)
    k_p = jnp.transpose(k_t.reshape(batch, 2, N_HEAD, DQK, seq), (0, 4, 1, 2, 3))
    v_p = jnp.transpose(v_h, (0, 2, 1, 3))
    shp = (DEPTH, batch, N_MEM, H_M, DH_M)
    return (y, mem_k.reshape(shp), mem_v.reshape(shp),
            jnp.stack(new_c), jnp.stack(new_n), jnp.stack(new_m), k_p, v_p)


def _sample_trunk(x_sample, cache_mem_k, cache_mem_v, state, cache_k, cache_v, page_table, P, W):
    batch = x_sample.shape[0]
    x = x_sample.reshape(batch, D_MODEL)
    c_all, n_all, m_all = state
    mem_k = jnp.transpose(cache_mem_k, (0, 1, 3, 4, 2)).reshape(DEPTH, batch, MEM_W, N_MEM)
    mem_v = jnp.transpose(cache_mem_v, (0, 1, 3, 4, 2)).reshape(DEPTH, batch, MEM_W, N_MEM)
    ck = jnp.transpose(cache_k, (0, 2, 3, 4, 1)).reshape(cache_k.shape[0], TOK_W, PAGE)
    cv = jnp.transpose(cache_v, (0, 2, 1, 3))

    new_c, new_n, new_m = [], [], []
    kv32 = None
    for li in range(DEPTH):
        if li < N_A:
            proj = _proj(x, W['w_a32'][li], batch, 384, hp=True)
            m_pad = jnp.pad(m_all[li], ((0, 0), (0, LANES - N_HEAD)))[:, None, :]
            tok3, c, n, m = _mlstm_sample(proj[:, None, :], W['gate_bias'][li], c_all, n_all, m_pad, li)
            new_c.append(c)
            new_n.append(n)
            new_m.append(m[:, 0, :N_HEAD])
            mq = proj[:, 4 * TOK_W:4 * TOK_W + MEM_W]
        else:
            if li == N_A:
                kv32 = _proj(x, P['w_kv_shared'], batch, 384, hp=True)
            j = li - N_A
            proj = _proj(x, P['w_in_b'][j], batch, 256, hp=True)
            tok3 = _diff_attn_sample(page_table, proj[:, None, :TOK_W], kv32[:, None, :TOK_W],
                                     kv32[:, None, TOK_W:], ck, cv, P['lambda_qk'],
                                     P['subln_g'][j:j + 1], j, _diff_lambda_init(li))
            mq = proj[:, TOK_W:]
        mem_o = _mem_attn_sample(mq[:, None, :], mem_k, mem_v, li)
        x = _layer_tail(x, tok3[:, 0, :], mem_o[:, 0, :], li, P, W, batch, batch, hp=True)
    y = x.reshape(batch, 1, D_MODEL)
    k_s = kv32[:, :TOK_W].reshape(batch, 1, 2, N_HEAD, DQK)
    v_s = kv32[:, TOK_W:].reshape(batch, 1, N_HEAD, DH)
    return y, jnp.stack(new_c), jnp.stack(new_n), jnp.stack(new_m), k_s, v_s


def kernel(x_prompt, x_sample, mem_prompt, cache_mem_k, cache_mem_v, state_mlstm_C, state_mlstm_n,
           state_mlstm_m, cache_k, cache_v, page_table, w_in_a, b_igate, b_fgate, w_in_b, lambda_qk,
           subln_g, w_kv_shared, w_mem_kv, w_out, ln_mix_g, ln_mix_b, ln_ffn_g, ln_ffn_b, w_group,
           b_group, w_router, b_router, w_e1, w_e3, w_e2):
    P = {'w_in_a': w_in_a, 'b_igate': b_igate, 'b_fgate': b_fgate, 'w_in_b': w_in_b,
         'lambda_qk': lambda_qk, 'subln_g': subln_g, 'w_kv_shared': w_kv_shared, 'w_mem_kv': w_mem_kv,
         'w_out': w_out, 'ln_mix_g': ln_mix_g, 'ln_mix_b': ln_mix_b, 'ln_ffn_g': ln_ffn_g,
         'ln_ffn_b': ln_ffn_b, 'w_group': w_group, 'b_group': b_group, 'w_router': w_router,
         'b_router': b_router, 'w_e1': w_e1, 'w_e3': w_e3, 'w_e2': w_e2}
    W = _prep_weights(P)
    y_p, mem_k_p, mem_v_p, c_p, n_p, m_p, k_p, v_p = _prompt_trunk(x_prompt, mem_prompt, P, W)
    y_s, c_s, n_s, m_s, k_s, v_s = _sample_trunk(
        x_sample, cache_mem_k, cache_mem_v, (state_mlstm_C, state_mlstm_n, state_mlstm_m),
        cache_k, cache_v, page_table, P, W)
    return (y_p, y_s, mem_k_p, mem_v_p, c_p, n_p, m_p, k_p, v_p, c_s, n_s, m_s, k_s, v_s)
```

```python
import functools
import math

import jax
import jax.numpy as jnp
from jax import lax
from jax.experimental import pallas as pl
from jax.experimental.pallas import tpu as pltpu

F32 = jnp.float32
BF16 = jnp.bfloat16

D_MODEL = 1024
DEPTH = 4
N_A = 2
TOK_W = 768
MEM_W = 256
N_HEAD = 6
DH = 128
DQK = 64
H_M = 4
DH_M = 64
N_MEM = 256
CHUNK = 128
N_EXPERTS = 16
D_EXPERT = 256
ALPHA = (2.0 * DEPTH) ** 0.25
LN_EPS = 1e-5
PAGE = 128
LANES = 128
GATE_COL = 3328
A_COLS = 3456
VMEM_LIMIT = 48 * 1024 * 1024


def _cparams(sem):
    return pltpu.CompilerParams(dimension_semantics=sem, vmem_limit_bytes=VMEM_LIMIT)


def _bdot(a, b):
    return jnp.dot(a.astype(BF16), b.astype(BF16), preferred_element_type=F32)


def _bdot_nt(a, b):
    return lax.dot_general(a.astype(BF16), b.astype(BF16), (((1,), (1,)), ((), ())),
                           preferred_element_type=F32)


def _split3(a):
    hi = a.astype(BF16)
    r = a - hi.astype(F32)
    mid = r.astype(BF16)
    lo = (r - mid.astype(F32)).astype(BF16)
    return hi, mid, lo


def _split2(a):
    hi = a.astype(BF16)
    return hi, (a - hi.astype(F32)).astype(BF16)


def _dot_hp(a, b, nt=False):
    m = a.shape[0]
    a_hi, a_mid, a_lo = _split3(a)
    b_hi, b_mid, b_lo = _split3(b)
    dot = _bdot_nt if nt else _bdot
    a3 = jnp.concatenate([a_hi, a_mid, a_lo], axis=0)
    r1 = dot(a3, b_hi)
    r2 = dot(a3[:2 * m], b_mid)
    r3 = dot(a_hi, b_lo)
    return ((r3 + r2[m:]) + (r1[2 * m:] + r2[:m]) + r1[m:2 * m]) + r1[:m]


def _dot_x3(a, b, nt=False):
    m = a.shape[0]
    a_hi, a_lo = _split2(a)
    b_hi, b_lo = _split2(b)
    dot = _bdot_nt if nt else _bdot
    r1 = dot(jnp.concatenate([a_hi, a_lo], axis=0), b_hi)
    return (dot(a_hi, b_lo) + r1[m:]) + r1[:m]


def _layer_norm(y, g, b):
    mu = jnp.mean(y, axis=-1, keepdims=True)
    d = y - mu
    var = jnp.mean(d * d, axis=-1, keepdims=True)
    return d * lax.rsqrt(var + LN_EPS) * g + b


def _sigmoid(x):
    return 1.0 / (1.0 + jnp.exp(-x))


def _log_sigmoid(x):
    return jnp.minimum(x, 0.0) - jnp.log1p(jnp.exp(-jnp.abs(x)))


def _proj_kernel(x_ref, w_ref, o_ref, *, hp):
    if hp:
        o_ref[...] = _dot_hp(x_ref[...], w_ref[...])
    else:
        o_ref[...] = jnp.dot(x_ref[...].astype(BF16), w_ref[...], preferred_element_type=F32)


def _proj(x, w, tm, tn=None, hp=False):
    m, k = x.shape
    n = w.shape[1]
    tn = n if tn is None else tn
    return pl.pallas_call(
        functools.partial(_proj_kernel, hp=hp),
        grid=(m // tm, n // tn),
        in_specs=[pl.BlockSpec((tm, k), lambda i, j: (i, 0)),
                  pl.BlockSpec((k, tn), lambda i, j: (0, j))],
        out_specs=pl.BlockSpec((tm, tn), lambda i, j: (i, j)),
        out_shape=jax.ShapeDtypeStruct((m, n), F32),
        compiler_params=_cparams(("arbitrary", "arbitrary")),
        name="proj_hp" if hp else "proj",
    )(x, w)


def _proj_kv_kernel(x_ref, w_ref, kt_ref, vh_ref, kb_ref, vt_ref):
    y = jnp.dot(x_ref[...].astype(BF16), w_ref[...], preferred_element_type=F32)
    k = y[:, :TOK_W]
    v = y[:, TOK_W:]
    kt_ref[0] = k.T
    for h in range(N_HEAD):
        vh_ref[0, h] = v[:, h * DH:(h + 1) * DH]
    kb_ref[...] = k.astype(BF16)
    vt_ref[0] = v.T.astype(BF16)


def _proj_shared_kv(x, w, batch, seq, tm):
    m, k = x.shape
    n = w.shape[1]
    per_b = seq // tm
    return pl.pallas_call(
        _proj_kv_kernel,
        grid=(m // tm,),
        in_specs=[pl.BlockSpec((tm, k), lambda i: (i, 0)),
                  pl.BlockSpec((k, n), lambda i: (0, 0))],
        out_specs=[pl.BlockSpec((1, TOK_W, tm), lambda i: (i // per_b, 0, i % per_b)),
                   pl.BlockSpec((1, N_HEAD, tm, DH), lambda i: (i // per_b, 0, i % per_b, 0)),
                   pl.BlockSpec((tm, TOK_W), lambda i: (i, 0)),
                   pl.BlockSpec((1, TOK_W, tm), lambda i: (i, 0, 0))],
        out_shape=[jax.ShapeDtypeStruct((batch, TOK_W, seq), F32),
                   jax.ShapeDtypeStruct((batch, N_HEAD, seq, DH), F32),
                   jax.ShapeDtypeStruct((m, TOK_W), BF16),
                   jax.ShapeDtypeStruct((m // tm, TOK_W, tm), BF16)],
        compiler_params=_cparams(("arbitrary",)),
        name="proj_kv",
    )(x, w)


def _mlstm_chunk_kernel(q_ref, k_ref, v_ref, o_ref, g_ref, bias_ref,
                        tok_ref, c_ref, n_ref, m_ref):
    @pl.when(pl.program_id(1) == 0)
    def _():
        c_ref[...] = jnp.zeros_like(c_ref)
        n_ref[...] = jnp.zeros_like(n_ref)
        m_ref[...] = jnp.zeros_like(m_ref)

    L = CHUNK
    lane = lax.broadcasted_iota(jnp.int32, (L, LANES), 1)
    row = lax.broadcasted_iota(jnp.int32, (L, L), 0)
    col = lax.broadcasted_iota(jnp.int32, (L, L), 1)
    causal = col <= row

    gb = g_ref[...] + bias_ref[...]
    lf = _log_sigmoid(gb)
    tril = causal.astype(BF16)
    hi = lf.astype(BF16)
    r1 = lf - hi.astype(F32)
    mid = r1.astype(BF16)
    lo = (r1 - mid.astype(F32)).astype(BF16)
    fcum = (jnp.dot(tril, hi, preferred_element_type=F32)
            + jnp.dot(tril, mid, preferred_element_type=F32)
            + jnp.dot(tril, lo, preferred_element_type=F32))
    z = jnp.where(lane < N_HEAD, gb, fcum)
    zt = z.T

    m_row = m_ref[0]
    lane1 = lax.broadcasted_iota(jnp.int32, (1, LANES), 1)
    m_new_row = m_row
    scale = DH ** -0.5
    for h in range(N_HEAD):
        f_col = fcum[:, N_HEAD + h:N_HEAD + h + 1]
        f_row = zt[N_HEAD + h:N_HEAD + h + 1, :]
        ig_row = zt[h:h + 1, :]
        ig_col = gb[:, h:h + 1]
        m_prev = m_row[:, h:h + 1]
        log_d = jnp.where(causal, f_col - f_row + ig_row, -jnp.inf)
        log_prev = f_col + m_prev
        m_t = jnp.maximum(log_prev, jnp.max(log_d, axis=1, keepdims=True))
        dm = jnp.exp(log_d - m_t)
        prev_scale = jnp.exp(log_prev - m_t)

        sl = slice(h * DH, (h + 1) * DH)
        qh = q_ref[:, sl]
        kh = k_ref[:, sl] * scale
        vh = v_ref[:, sl]
        qb = qh.astype(BF16)
        vb = vh.astype(BF16)
        c_h = c_ref[0, h]
        n_h = n_ref[0, h:h + 1, :]
        qk = _bdot_nt(qb, kh) * dm
        num = _bdot(qk, vb) + prev_scale * _bdot(qb, c_h)
        den = (jnp.sum(qk, axis=1, keepdims=True)
               + prev_scale * jnp.sum(qh * n_h, axis=1, keepdims=True))
        hh = num / jnp.maximum(jnp.abs(den), jnp.exp(-m_t))
        tok_ref[:, sl] = _sigmoid(o_ref[:, sl]) * hh

        m_last = m_t[L - 1:L, :]
        f_last = f_col[L - 1:L, :]
        w_src = jnp.exp(f_last + ig_col - f_col - m_last)
        c_scale = jnp.exp(f_last + m_prev - m_last)
        kw = kh * w_src
        c_ref[0, h] = c_scale * c_h + _bdot(kw.T, vb)
        n_ref[0, h:h + 1, :] = c_scale * n_h + jnp.sum(kw, axis=0, keepdims=True)
        m_new_row = jnp.where(lane1 == h, m_last, m_new_row)
    m_ref[0] = m_new_row


def _mlstm_prompt(proj, bias_row, batch, seq):
    nc = seq // CHUNK
    m_tok = batch * seq

    def col_block(j):
        return pl.BlockSpec((CHUNK, TOK_W), lambda b, c: (b * nc + c, j))

    return pl.pallas_call(
        _mlstm_chunk_kernel,
        grid=(batch, nc),
        in_specs=[col_block(0), col_block(1), col_block(2), col_block(3),
                  pl.BlockSpec((CHUNK, LANES), lambda b, c: (b * nc + c, GATE_COL // LANES)),
                  pl.BlockSpec((1, LANES), lambda b, c: (0, 0))],
        out_specs=[pl.BlockSpec((CHUNK, TOK_W), lambda b, c: (b * nc + c, 0)),
                   pl.BlockSpec((1, N_HEAD, DH, DH), lambda b, c: (b, 0, 0, 0)),
                   pl.BlockSpec((1, N_HEAD, DH), lambda b, c: (b, 0, 0)),
                   pl.BlockSpec((1, 1, LANES), lambda b, c: (b, 0, 0))],
        out_shape=[jax.ShapeDtypeStruct((m_tok, TOK_W), F32),
                   jax.ShapeDtypeStruct((batch, N_HEAD, DH, DH), F32),
                   jax.ShapeDtypeStruct((batch, N_HEAD, DH), F32),
                   jax.ShapeDtypeStruct((batch, 1, LANES), F32)],
        compiler_params=_cparams(("arbitrary", "arbitrary")),
        name="mlstm_chunk",
    )(proj, proj, proj, proj, proj, bias_row)


def _mlstm_step_kernel(q_ref, k_ref, v_ref, o_ref, g_ref, bias_ref, c_ref, n_ref, m_ref,
                       tok_ref, c_out, n_out, m_out):
    gb = g_ref[0] + bias_ref[...]
    m_row = m_ref[0]
    lane1 = lax.broadcasted_iota(jnp.int32, (1, LANES), 1)
    row = lax.broadcasted_iota(jnp.int32, (DH, DH), 0)
    col = lax.broadcasted_iota(jnp.int32, (DH, DH), 1)
    eye = row == col
    m_new_row = m_row
    scale = DH ** -0.5
    for h in range(N_HEAD):
        ig = gb[:, h:h + 1]
        lf = _log_sigmoid(gb[:, N_HEAD + h:N_HEAD + h + 1])
        m_prev = m_row[:, h:h + 1]
        log_prev = lf + m_prev
        m_t = jnp.maximum(log_prev, ig)
        dm = jnp.exp(ig - m_t)
        prev_scale = jnp.exp(log_prev - m_t)

        sl = slice(h * DH, (h + 1) * DH)
        q_row = q_ref[0][:, sl]
        k_row = k_ref[0][:, sl] * scale
        v_row = v_ref[0][:, sl]
        c_h = c_ref[0, 0, h]
        n_h = n_ref[0, 0, h:h + 1, :]
        qk = jnp.sum(q_row * k_row, axis=1, keepdims=True) * dm
        q_c = _dot_hp(jnp.broadcast_to(q_row, (16, DH)), c_h)[0:1, :]
        num = qk * v_row + prev_scale * q_c
        den = qk + prev_scale * jnp.sum(q_row * n_h, axis=1, keepdims=True)
        hh = num / jnp.maximum(jnp.abs(den), jnp.exp(-m_t))
        tok_ref[0, :, sl] = _sigmoid(o_ref[0][:, sl]) * hh

        k_col = jnp.sum(jnp.where(eye, jnp.broadcast_to(k_row, (DH, DH)), 0.0), axis=1, keepdims=True)
        c_out[0, h] = prev_scale * c_h + (dm * k_col) * v_row
        n_out[0, h:h + 1, :] = prev_scale * n_h + dm * k_row
        m_new_row = jnp.where(lane1 == h, m_t, m_new_row)
    m_out[0] = m_new_row


def _mlstm_sample(proj3, bias_row, c_all, n_all, m_pad, li):
    batch = proj3.shape[0]

    def col_block(j):
        return pl.BlockSpec((1, 1, TOK_W), lambda b: (b, 0, j))

    return pl.pallas_call(
        _mlstm_step_kernel,
        grid=(batch,),
        in_specs=[col_block(0), col_block(1), col_block(2), col_block(3),
                  pl.BlockSpec((1, 1, LANES), lambda b: (b, 0, GATE_COL // LANES)),
                  pl.BlockSpec((1, LANES), lambda b: (0, 0)),
                  pl.BlockSpec((1, 1, N_HEAD, DH, DH), lambda b: (li, b, 0, 0, 0)),
                  pl.BlockSpec((1, 1, N_HEAD, DH), lambda b: (li, b, 0, 0)),
                  pl.BlockSpec((1, 1, LANES), lambda b: (b, 0, 0))],
        out_specs=[pl.BlockSpec((1, 1, TOK_W), lambda b: (b, 0, 0)),
                   pl.BlockSpec((1, N_HEAD, DH, DH), lambda b: (b, 0, 0, 0)),
                   pl.BlockSpec((1, N_HEAD, DH), lambda b: (b, 0, 0)),
                   pl.BlockSpec((1, 1, LANES), lambda b: (b, 0, 0))],
        out_shape=[jax.ShapeDtypeStruct((batch, 1, TOK_W), F32),
                   jax.ShapeDtypeStruct((batch, N_HEAD, DH, DH), F32),
                   jax.ShapeDtypeStruct((batch, N_HEAD, DH), F32),
                   jax.ShapeDtypeStruct((batch, 1, LANES), F32)],
        compiler_params=_cparams(("arbitrary",)),
        name="mlstm_step",
    )(proj3, proj3, proj3, proj3, proj3, bias_row, c_all, n_all, m_pad)


def _mem_attn_kernel(q_ref, k_ref, v_ref, o_ref):
    outs = []
    for h in range(H_M):
        sl = slice(h * DH_M, (h + 1) * DH_M)
        s = _bdot_nt(q_ref[:, sl], k_ref[0][:, sl]) * (DH_M ** -0.5)
        e = jnp.exp(s - jnp.max(s, axis=1, keepdims=True))
        p = e / jnp.sum(e, axis=1, keepdims=True)
        outs.append(_bdot(p, v_ref[0][:, sl]))
    o_ref[...] = jnp.concatenate(outs, axis=-1)


def _mem_attn_prompt(proj, q_col_block, mem_k, mem_v, seq, tq):
    m_tok = proj.shape[0]
    per_b = seq // tq
    return pl.pallas_call(
        _mem_attn_kernel,
        grid=(m_tok // tq,),
        in_specs=[pl.BlockSpec((tq, MEM_W), lambda i: (i, q_col_block)),
                  pl.BlockSpec((1, N_MEM, MEM_W), lambda i: (i // per_b, 0, 0)),
                  pl.BlockSpec((1, N_MEM, MEM_W), lambda i: (i // per_b, 0, 0))],
        out_specs=pl.BlockSpec((tq, MEM_W), lambda i: (i, 0)),
        out_shape=jax.ShapeDtypeStruct((m_tok, MEM_W), F32),
        compiler_params=_cparams(("arbitrary",)),
        name="mem_attn",
    )(proj, mem_k, mem_v)


MEM_STEP_BATCH = 4


def _mem_attn_step_kernel(q_ref, k_ref, v_ref, o_ref):
    rows = 16
    rid = lax.broadcasted_iota(jnp.int32, (rows, MEM_W), 0)
    lane = lax.broadcasted_iota(jnp.int32, (rows, MEM_W), 1)
    head_mask = (lane // DH_M) == rid
    for g in range(MEM_STEP_BATCH):
        q = jnp.broadcast_to(q_ref[g], (rows, MEM_W))
        qm = jnp.where(head_mask, q, 0.0)
        s = _dot_hp(qm, k_ref[0, g]) * (DH_M ** -0.5)
        e = jnp.exp(s - jnp.max(s, axis=1, keepdims=True))
        p = e / jnp.sum(e, axis=1, keepdims=True)
        o = _dot_hp(p, v_ref[0, g], nt=True)
        o_ref[g] = jnp.sum(jnp.where(head_mask, o, 0.0), axis=0, keepdims=True)


def _mem_attn_sample(mq3, cache_k, cache_v, li):
    batch = mq3.shape[0]
    gb = MEM_STEP_BATCH
    return pl.pallas_call(
        _mem_attn_step_kernel,
        grid=(batch // gb,),
        in_specs=[pl.BlockSpec((gb, 1, MEM_W), lambda b: (b, 0, 0)),
                  pl.BlockSpec((1, gb, N_MEM, MEM_W), lambda b: (li, b, 0, 0)),
                  pl.BlockSpec((1, gb, N_MEM, MEM_W), lambda b: (li, b, 0, 0))],
        out_specs=pl.BlockSpec((gb, 1, MEM_W), lambda b: (b, 0, 0)),
        out_shape=jax.ShapeDtypeStruct((batch, 1, MEM_W), F32),
        compiler_params=_cparams(("arbitrary",)),
        name="mem_attn_step",
    )(mq3, cache_k, cache_v)


def _routing(logits):
    lane = lax.broadcasted_iota(jnp.int32, logits.shape, 1)
    lane_f = lane.astype(F32)
    big = 1000.0
    is_g = (lane >= N_EXPERTS) & (lane < N_EXPERTS + 4)
    lg = jnp.where(is_g, logits, -jnp.inf)
    gmax = jnp.max(lg, axis=1, keepdims=True)
    gidx = jnp.min(jnp.where(lg == gmax, lane_f, big), axis=1, keepdims=True) - float(N_EXPERTS)
    p_g = 1.0 / jnp.sum(jnp.exp(lg - gmax), axis=1, keepdims=True)
    in_grp = (lane < N_EXPERTS) & ((lane >> 2).astype(F32) == gidx)
    le = jnp.where(in_grp, logits, -jnp.inf)
    v1 = jnp.max(le, axis=1, keepdims=True)
    i1 = jnp.min(jnp.where(le == v1, lane_f, big), axis=1, keepdims=True)
    le2 = jnp.where(lane_f == i1, -jnp.inf, le)
    v2 = jnp.max(le2, axis=1, keepdims=True)
    i2 = jnp.min(jnp.where(le2 == v2, lane_f, big), axis=1, keepdims=True)
    e2 = jnp.exp(v2 - v1)
    inv = 1.0 / (1.0 + e2)
    comb = jnp.where(lane_f == i1, inv * p_g, jnp.where(lane_f == i2, e2 * inv * p_g, 0.0))
    return comb, gidx


GROUP_LANE = 16
RANK_LANE = 17


def _mix_kernel(x_ref, tok_ref, mem_ref, wt_ref, wm_ref, g_ref, b_ref, wr_ref, br_ref,
                x1_ref, comb_ref, cnt_ref, *, hp):
    @pl.when(pl.program_id(0) == 0)
    def _():
        cnt_ref[...] = jnp.zeros_like(cnt_ref)

    dot = _dot_hp if hp else _bdot
    mix = dot(tok_ref[...], wt_ref[...]) + dot(mem_ref[...], wm_ref[...])
    x1 = _layer_norm(ALPHA * x_ref[...] + mix, g_ref[...], b_ref[...])
    x1_ref[...] = x1
    logits = dot(x1, wr_ref[...]) + br_ref[...]
    comb, gidx = _routing(logits)
    tm = comb.shape[0]
    lane_f = lax.broadcasted_iota(jnp.int32, comb.shape, 1).astype(F32)
    mine = lane_f == gidx
    onehot = jnp.where(mine, 1.0, 0.0)
    row = lax.broadcasted_iota(jnp.int32, (tm, tm), 0)
    col = lax.broadcasted_iota(jnp.int32, (tm, tm), 1)
    before = jnp.where(col < row, 1.0, 0.0).astype(BF16)
    prefix = jnp.dot(before, onehot.astype(BF16), preferred_element_type=F32)
    cnt = cnt_ref[...]
    rank = jnp.sum(jnp.where(mine, prefix + cnt, 0.0), axis=1, keepdims=True)
    cnt_ref[...] = cnt + jnp.sum(onehot, axis=0, keepdims=True)
    comb_ref[...] = jnp.where(lane_f == float(GROUP_LANE), gidx,
                              jnp.where(lane_f == float(RANK_LANE), rank, comb))


def _mix(x, tok, mem, wt, wm, g, b, wr, br, tm, hp=False):
    m_tok = x.shape[0]
    const = lambda i: (0, 0)
    rows = lambda i: (i, 0)
    return pl.pallas_call(
        functools.partial(_mix_kernel, hp=hp),
        grid=(m_tok // tm,),
        in_specs=[pl.BlockSpec((tm, D_MODEL), rows),
                  pl.BlockSpec((tm, TOK_W), rows),
                  pl.BlockSpec((tm, MEM_W), rows),
                  pl.BlockSpec((TOK_W, D_MODEL), const),
                  pl.BlockSpec((MEM_W, D_MODEL), const),
                  pl.BlockSpec((1, D_MODEL), const),
                  pl.BlockSpec((1, D_MODEL), const),
                  pl.BlockSpec((D_MODEL, LANES), const),
                  pl.BlockSpec((1, LANES), const)],
        out_specs=[pl.BlockSpec((tm, D_MODEL), rows),
                   pl.BlockSpec((tm, LANES), rows),
                   pl.BlockSpec((1, LANES), const)],
        out_shape=[jax.ShapeDtypeStruct((m_tok, D_MODEL), F32),
                   jax.ShapeDtypeStruct((m_tok, LANES), F32),
                   jax.ShapeDtypeStruct((1, LANES), F32)],
        compiler_params=_cparams(("arbitrary",)),
        name="mix",
    )(x, tok, mem, wt, wm, g, b, wr, br)


def _moe_kernel(x_ref, comb_ref, w1_ref, w3_ref, w2_ref, g_ref, b_ref, o_ref, xb_ref, acc_ref,
                *, hp):
    e = pl.program_id(1)

    @pl.when(e == 0)
    def _():
        xb_ref[...] = x_ref[...].astype(BF16)
        acc_ref[...] = jnp.zeros_like(acc_ref)

    if hp:
        x = x_ref[...]
        a = _dot_hp(x, w1_ref[0, 0])
        b = _dot_hp(x, w3_ref[0, 0])
    else:
        xb = xb_ref[...]
        a = jnp.dot(xb, w1_ref[0, 0].astype(BF16), preferred_element_type=F32)
        b = jnp.dot(xb, w3_ref[0, 0].astype(BF16), preferred_element_type=F32)
    comb = comb_ref[...]
    lane = lax.broadcasted_iota(jnp.int32, comb.shape, 1)
    ce = jnp.sum(jnp.where(lane == e, comb, 0.0), axis=1, keepdims=True)
    hid = (a * _sigmoid(a)) * b * ce
    if hp:
        acc_ref[...] += _dot_hp(hid, w2_ref[0, 0])
    else:
        acc_ref[...] += jnp.dot(hid.astype(BF16), w2_ref[0, 0].astype(BF16),
                                preferred_element_type=F32)

    @pl.when(e == N_EXPERTS - 1)
    def _():
        o_ref[...] = _layer_norm(ALPHA * x_ref[...] + acc_ref[...], g_ref[...], b_ref[...])


def _moe(x1, comb, w1, w3, w2, g, b, li, tm, hp=False):
    m_tok = x1.shape[0]
    rows = lambda i, e: (i, 0)
    const = lambda i, e: (0, 0)
    return pl.pallas_call(
        functools.partial(_moe_kernel, hp=hp),
        grid=(m_tok // tm, N_EXPERTS),
        in_specs=[pl.BlockSpec((tm, D_MODEL), rows),
                  pl.BlockSpec((tm, LANES), rows),
                  pl.BlockSpec((1, 1, D_MODEL, D_EXPERT), lambda i, e: (li, e, 0, 0)),
                  pl.BlockSpec((1, 1, D_MODEL, D_EXPERT), lambda i, e: (li, e, 0, 0)),
                  pl.BlockSpec((1, 1, D_EXPERT, D_MODEL), lambda i, e: (li, e, 0, 0)),
                  pl.BlockSpec((1, D_MODEL), const),
                  pl.BlockSpec((1, D_MODEL), const)],
        out_specs=pl.BlockSpec((tm, D_MODEL), rows),
        out_shape=jax.ShapeDtypeStruct((m_tok, D_MODEL), F32),
        scratch_shapes=[pltpu.VMEM((tm, D_MODEL), BF16), pltpu.VMEM((tm, D_MODEL), F32)],
        compiler_params=_cparams(("arbitrary", "arbitrary")),
        name="moe",
    )(x1, comb, w1, w3, w2, g, b)


SLOT_TILE = 256
N_GROUPS = 4
EXP_PER_GROUP = 4
MOE_VMEM_LIMIT = 56 * 1024 * 1024


N_XBUF = 3


def _dispatch_kernel(dest_ref, fill_ref, x_ref, xs_ref, inv_ref, xbuf, zrow, load_sem, sem, sem_fill,
                     *, tm, n_steps, n_fill):
    i = pl.program_id(0)
    base = i * tm
    slot = i % N_XBUF

    def load(step, buf):
        return pltpu.make_async_copy(x_ref.at[pl.ds(step * tm, tm)], xbuf.at[buf], load_sem.at[buf])

    @pl.when(i == 0)
    def _():
        load(0, 0).start()

    @pl.when(i + 1 < n_steps)
    def _():
        load(i + 1, (i + 1) % N_XBUF).start()

    load(i, slot).wait()

    par = i % 2

    def body(r, carry):
        d = dest_ref[base + r]
        pltpu.make_async_copy(xbuf.at[slot, pl.ds(r, 1)], xs_ref.at[pl.ds(d, 1)], sem.at[par]).start()
        inv_ref[d] = base + r
        return carry

    lax.fori_loop(0, tm, body, 0, unroll=8)

    def wait_step(p):
        pltpu.make_async_copy(xbuf.at[0], xs_ref.at[pl.ds(0, tm)], sem.at[p]).wait()

    @pl.when(i > 0)
    def _():
        wait_step(1 - par)

    @pl.when(i == n_steps - 1)
    def _():
        wait_step(par)
        zrow[...] = jnp.zeros_like(zrow)

        def fill(s, carry):
            pltpu.make_async_copy(zrow.at[pl.ds(0, 1)], xs_ref.at[pl.ds(s, 1)], sem_fill).start()
            inv_ref[s] = -1
            return carry

        for k in range(N_GROUPS + 1):
            lax.fori_loop(fill_ref[2 * k], fill_ref[2 * k + 1], fill, 0)
        pltpu.make_async_copy(xs_ref.at[pl.ds(0, n_fill)], xs_ref.at[pl.ds(0, n_fill)],
                              sem_fill).wait()


def _dispatch(dest, fill, x1, n_slots, tm):
    m_tok = x1.shape[0]
    n_steps = m_tok // tm
    grid_spec = pltpu.PrefetchScalarGridSpec(
        num_scalar_prefetch=2,
        grid=(n_steps,),
        in_specs=[pl.BlockSpec(memory_space=pl.ANY)],
        out_specs=[pl.BlockSpec(memory_space=pl.ANY),
                   pl.BlockSpec(memory_space=pltpu.SMEM)],
        scratch_shapes=[pltpu.VMEM((N_XBUF, tm, D_MODEL), F32),
                        pltpu.VMEM((8, D_MODEL), F32),
                        pltpu.SemaphoreType.DMA((N_XBUF,)),
                        pltpu.SemaphoreType.DMA((2,)),
                        pltpu.SemaphoreType.DMA(())],
    )
    return pl.pallas_call(
        functools.partial(_dispatch_kernel, tm=tm, n_steps=n_steps, n_fill=n_slots - m_tok),
        grid_spec=grid_spec,
        out_shape=[jax.ShapeDtypeStruct((n_slots, D_MODEL), F32),
                   jax.ShapeDtypeStruct((n_slots,), jnp.int32)],
        compiler_params=_cparams(("arbitrary",)),
        name="moe_dispatch",
    )(dest, fill, x1)


def _moe_sparse_kernel(blk_ref, grp_ref, val_ref, nrow_ref, inv_ref,
                       xs_ref, w1_ref, w3_ref, w2_ref, wr_ref, br_ref, g_ref, b_ref,
                       out_ref, w1b, w3b, w2b, ybuf, junk, state, sem, *, n_tiles):
    i = pl.program_id(0)
    ts = SLOT_TILE

    @pl.when(i == 0)
    def _():
        state[0] = -1
        state[1] = 0

    def wait_pending():
        @pl.when(state[1] == 1)
        def _():
            pltpu.make_async_copy(ybuf, junk, sem).wait()

    @pl.when(val_ref[i] == 1)
    def _():
        grp = grp_ref[i]

        @pl.when(grp != state[0])
        def _():
            for j in range(EXP_PER_GROUP):
                cols = slice(j * D_EXPERT, (j + 1) * D_EXPERT)
                w1b[:, cols] = w1_ref[0, 0, j].astype(BF16)
                w3b[:, cols] = w3_ref[0, 0, j].astype(BF16)
                w2b[cols, :] = w2_ref[0, 0, j].astype(BF16)
            state[0] = grp

        x = xs_ref[...]
        xb = x.astype(BF16)
        logits = jnp.dot(xb, wr_ref[...], preferred_element_type=F32) + br_ref[...]
        comb, _ = _routing(logits)
        lane = lax.broadcasted_iota(jnp.int32, comb.shape, 1)
        a = jnp.dot(xb, w1b[...], preferred_element_type=F32)
        b = jnp.dot(xb, w3b[...], preferred_element_type=F32)
        hid = []
        for j in range(EXP_PER_GROUP):
            cols = slice(j * D_EXPERT, (j + 1) * D_EXPERT)
            ce = jnp.sum(jnp.where(lane == grp * EXP_PER_GROUP + j, comb, 0.0), axis=1, keepdims=True)
            a_j = a[:, cols]
            hid.append(((a_j * _sigmoid(a_j)) * b[:, cols] * ce).astype(BF16))
        acc = jnp.dot(jnp.concatenate(hid, axis=1), w2b[...], preferred_element_type=F32)
        x2 = _layer_norm(ALPHA * x + acc, g_ref[...], b_ref[...])

        wait_pending()
        ybuf[...] = x2
        n = nrow_ref[i]
        base = blk_ref[i] * ts

        def body(r, carry):
            t = inv_ref[base + r]
            pltpu.make_async_copy(ybuf.at[pl.ds(r, 1)], out_ref.at[pl.ds(t, 1)], sem).start()
            return carry

        def pad_body(r, carry):
            pltpu.make_async_copy(ybuf.at[pl.ds(r, 1)], junk.at[pl.ds(r, 1)], sem).start()
            return carry

        full = n // 8

        def body8(gi, carry):
            for u in range(8):
                body(gi * 8 + u, carry)
            return carry

        lax.fori_loop(0, full, body8, 0)
        lax.fori_loop(full * 8, n, body, 0)
        lax.fori_loop(n, ts, pad_body, 0)
        state[1] = 1

    @pl.when(i == n_tiles - 1)
    def _():
        wait_pending()
        state[1] = 0


def _moe_sparse(x1, comb, cnt, P, W, li):
    m_tok = x1.shape[0]
    ts = SLOT_TILE
    n_tiles = m_tok // ts + N_GROUPS
    n_slots = n_tiles * ts
    grp_tok = comb[:, GROUP_LANE].astype(jnp.int32)
    rank_tok = comb[:, RANK_LANE].astype(jnp.int32)
    counts = cnt[0, :N_GROUPS].astype(jnp.int32)
    tiles_g = (counts + ts - 1) // ts
    tile_end = jnp.cumsum(tiles_g)
    slot_base = (tile_end - tiles_g) * ts
    dest = jnp.take(slot_base, grp_tok) + rank_tok
    total = tile_end[-1]
    tile_i = jnp.arange(n_tiles, dtype=jnp.int32)
    valid = (tile_i < total).astype(jnp.int32)
    blk = jnp.minimum(tile_i, total - 1)
    grp_tile = jnp.minimum(jnp.sum((tile_end[None, :] <= blk[:, None]).astype(jnp.int32), axis=1),
                           N_GROUPS - 1)
    n_rows = jnp.clip(jnp.take(slot_base + counts, grp_tile) - blk * ts, 0, ts) * valid

    fill_lo = jnp.concatenate([slot_base + counts, (total * ts)[None]])
    fill_hi = jnp.concatenate([slot_base + tiles_g * ts, jnp.full((1,), n_slots, jnp.int32)])
    fill = jnp.stack([fill_lo, fill_hi], axis=1).reshape(-1).astype(jnp.int32)

    xs, inv = _dispatch(dest, fill, x1, n_slots, 512)

    w1 = P['w_e1'].reshape(DEPTH, N_GROUPS, EXP_PER_GROUP, D_MODEL, D_EXPERT)
    w3 = P['w_e3'].reshape(DEPTH, N_GROUPS, EXP_PER_GROUP, D_MODEL, D_EXPERT)
    w2 = P['w_e2'].reshape(DEPTH, N_GROUPS, EXP_PER_GROUP, D_EXPERT, D_MODEL)
    const = lambda i, *_: (0, 0)
    grid_spec = pltpu.PrefetchScalarGridSpec(
        num_scalar_prefetch=5,
        grid=(n_tiles,),
        in_specs=[pl.BlockSpec((ts, D_MODEL), lambda i, blk, grp, val, nrow, inv: (blk[i], 0)),
                  pl.BlockSpec((1, 1, EXP_PER_GROUP, D_MODEL, D_EXPERT),
                               lambda i, blk, grp, val, nrow, inv: (li, grp[i], 0, 0, 0)),
                  pl.BlockSpec((1, 1, EXP_PER_GROUP, D_MODEL, D_EXPERT),
                               lambda i, blk, grp, val, nrow, inv: (li, grp[i], 0, 0, 0)),
                  pl.BlockSpec((1, 1, EXP_PER_GROUP, D_EXPERT, D_MODEL),
                               lambda i, blk, grp, val, nrow, inv: (li, grp[i], 0, 0, 0)),
                  pl.BlockSpec((D_MODEL, LANES), const),
                  pl.BlockSpec((1, LANES), const),
                  pl.BlockSpec((1, D_MODEL), const),
                  pl.BlockSpec((1, D_MODEL), const)],
        out_specs=pl.BlockSpec(memory_space=pl.ANY),
        scratch_shapes=[pltpu.VMEM((D_MODEL, EXP_PER_GROUP * D_EXPERT), BF16),
                        pltpu.VMEM((D_MODEL, EXP_PER_GROUP * D_EXPERT), BF16),
                        pltpu.VMEM((EXP_PER_GROUP * D_EXPERT, D_MODEL), BF16),
                        pltpu.VMEM((ts, D_MODEL), F32),
                        pltpu.VMEM((ts, D_MODEL), F32),
                        pltpu.SMEM((2,), jnp.int32),
                        pltpu.SemaphoreType.DMA(())],
    )
    return pl.pallas_call(
        functools.partial(_moe_sparse_kernel, n_tiles=n_tiles),
        grid_spec=grid_spec,
        out_shape=jax.ShapeDtypeStruct((m_tok, D_MODEL), F32),
        compiler_params=pltpu.CompilerParams(dimension_semantics=("arbitrary",),
                                             vmem_limit_bytes=MOE_VMEM_LIMIT),
        name="moe_sparse",
    )(blk, grp_tile, valid, n_rows, inv, xs, w1, w3, w2, W['w_r'][li], W['b_r'][li],
      P['ln_ffn_g'][li][None, :], P['ln_ffn_b'][li][None, :])


def _diff_lambda(lq_ref, lam_init):
    lq = lq_ref[0]
    a = jnp.sum(lq[0:1, :] * lq[1:2, :], axis=1, keepdims=True)
    b = jnp.sum(lq[2:3, :] * lq[3:4, :], axis=1, keepdims=True)
    return jnp.exp(a) - jnp.exp(b) + lam_init


def _sub_norm(o, g_row, lam_init):
    o = o * lax.rsqrt(jnp.mean(o * o, axis=-1, keepdims=True) + LN_EPS) * g_row
    return o * (1.0 - lam_init)


def _diff_attn_kernel(q1_ref, q2_ref, k1_ref, k2_ref, vt_ref, lq_ref, gs_ref, o_ref,
                      m_scr, l_scr, acc_scr, *, tq, lam_init):
    i = pl.program_id(2)
    lam = _diff_lambda(lq_ref, lam_init)
    q_scale = (DQK ** -0.5) * math.log2(math.e)
    qs = [(q1_ref[...] * q_scale).astype(BF16), (q2_ref[...] * q_scale).astype(BF16)]
    k_refs = [k1_ref, k2_ref]
    m_scr[...] = jnp.full_like(m_scr, -jnp.inf)
    l_scr[...] = jnp.zeros_like(l_scr)
    acc_scr[...] = jnp.zeros_like(acc_scr)
    key_i = lax.broadcasted_iota(jnp.int32, (tq, tq), 0)
    qry_i = lax.broadcasted_iota(jnp.int32, (tq, tq), 1)
    causal = key_i <= qry_i

    def block(j, masked):
        start = pl.multiple_of(j * tq, tq)
        sts = []
        for hh in range(2):
            for mm in range(2):
                kb = k_refs[mm][pl.ds(start, tq), hh * DQK:(hh + 1) * DQK]
                st = lax.dot_general(kb, qs[mm][:, hh * DQK:(hh + 1) * DQK],
                                     (((1,), (1,)), ((), ())), preferred_element_type=F32)
                if masked:
                    st = jnp.where(causal, st, -jnp.inf)
                sts.append(st)
        m_all = m_scr[...]
        l_all = l_scr[...]
        ps, alphas, m_rows, l_rows = [], [], [], []
        for idx in range(4):
            m_old = m_all[idx:idx + 1, :]
            m_new = jnp.maximum(m_old, jnp.max(sts[idx], axis=0, keepdims=True))
            alpha = jnp.exp2(m_old - m_new)
            p = jnp.exp2(sts[idx] - m_new)
            l_rows.append(alpha * l_all[idx:idx + 1, :] + jnp.sum(p, axis=0, keepdims=True))
            m_rows.append(m_new)
            alphas.append(alpha)
            ps.append(p.astype(BF16))
        m_scr[...] = jnp.concatenate(m_rows + [m_all[4:, :]], axis=0)
        l_scr[...] = jnp.concatenate(l_rows + [l_all[4:, :]], axis=0)
        for idx in range(4):
            vt = vt_ref[j, (idx // 2) * DH:(idx // 2 + 1) * DH, :]
            acc_scr[idx] = alphas[idx] * acc_scr[idx] + jnp.dot(vt, ps[idx],
                                                                preferred_element_type=F32)

    def body(j, carry):
        block(j, False)
        return carry

    lax.fori_loop(0, i, body, 0)
    block(i, True)

    for hh in range(2):
        o1 = acc_scr[hh * 2] / l_scr[hh * 2:hh * 2 + 1, :]
        o2 = acc_scr[hh * 2 + 1] / l_scr[hh * 2 + 1:hh * 2 + 2, :]
        o = o1 - lam * o2
        o = o * lax.rsqrt(jnp.mean(o * o, axis=0, keepdims=True) + LN_EPS)
        o_ref[:, hh * DH:(hh + 1) * DH] = o.T * gs_ref[...] * (1.0 - lam_init)


def _diff_attn_prompt(proj_b, k_b, vt_b, lam_qk, g_sub, j, lam_init, batch, seq, tq):
    m_tok = batch * seq
    nq = seq // tq
    kern = functools.partial(_diff_attn_kernel, tq=tq, lam_init=lam_init)
    return pl.pallas_call(
        kern,
        grid=(batch, 3, nq),
        in_specs=[pl.BlockSpec((tq, 2 * DQK), lambda b, hp, i: (b * nq + i, hp)),
                  pl.BlockSpec((tq, 2 * DQK), lambda b, hp, i: (b * nq + i, 3 + hp)),
                  pl.BlockSpec((seq, 2 * DQK), lambda b, hp, i: (b, hp)),
                  pl.BlockSpec((seq, 2 * DQK), lambda b, hp, i: (b, 3 + hp)),
                  pl.BlockSpec((nq, 2 * DH, tq), lambda b, hp, i: (b, hp, 0)),
                  pl.BlockSpec((1, 4, DQK), lambda b, hp, i: (j, 0, 0)),
                  pl.BlockSpec((1, DH), lambda b, hp, i: (0, 0))],
        out_specs=pl.BlockSpec((tq, 2 * DH), lambda b, hp, i: (b * nq + i, hp)),
        out_shape=jax.ShapeDtypeStruct((m_tok, TOK_W), F32),
        scratch_shapes=[pltpu.VMEM((8, tq), F32), pltpu.VMEM((8, tq), F32),
                        pltpu.VMEM((4, DH, tq), F32)],
        compiler_params=_cparams(("arbitrary", "arbitrary", "arbitrary")),
        name="diff_attn",
    )(proj_b, proj_b, k_b, k_b, vt_b, lam_qk, g_sub)


PAGES_PER_STEP = 16


def _diff_attn_step_kernel(pt_ref, q_ref, kn_ref, vn_ref, *rest, n_steps, lam_init):
    k_refs = rest[:PAGES_PER_STEP]
    v_refs = rest[PAGES_PER_STEP:2 * PAGES_PER_STEP]
    lq_ref, gs_ref, o_ref, qm_scr, m_scr, l_scr, acc_scr = rest[2 * PAGES_PER_STEP:]
    p_idx = pl.program_id(1)
    rows = 16
    rid = lax.broadcasted_iota(jnp.int32, (rows, TOK_W), 0)
    lane = lax.broadcasted_iota(jnp.int32, (rows, TOK_W), 1)

    @pl.when(p_idx == 0)
    def _():
        q = jnp.broadcast_to(q_ref[0] * (DQK ** -0.5), (rows, TOK_W))
        qm_scr[...] = jnp.where((lane // DQK) == rid, q, 0.0)
        m_scr[...] = jnp.full_like(m_scr, -jnp.inf)
        l_scr[...] = jnp.zeros_like(l_scr)
        acc_scr[...] = jnp.zeros_like(acc_scr)

    qm = qm_scr[...]
    q_hi, q_lo = _split2(qm)
    q2 = jnp.concatenate([q_hi, q_lo], axis=0)
    s_pages = []
    for k_ref in k_refs:
        k_hi, k_lo = _split2(k_ref[0])
        r1 = _bdot(q2, k_hi)
        s_pages.append((_bdot(q_hi, k_lo) + r1[rows:]) + r1[:rows])
    s = jnp.concatenate(s_pages, axis=1)
    m_old = m_scr[...]
    m_new = jnp.maximum(m_old, jnp.max(s, axis=1, keepdims=True))
    alpha = jnp.exp(m_old - m_new)
    p = jnp.exp(s - m_new)
    p_hi, p_lo = _split2(p)
    p2 = jnp.concatenate([p_hi, p_lo], axis=0)
    pv = []
    for h in range(N_HEAD):
        pv_h = None
        for g in range(PAGES_PER_STEP):
            cols = slice(g * PAGE, (g + 1) * PAGE)
            v_hi, v_lo = _split2(v_refs[g][0, h])
            r1 = _bdot(p2[:, cols], v_hi)
            term = (_bdot(p_hi[:, cols], v_lo) + r1[rows:]) + r1[:rows]
            pv_h = term if pv_h is None else pv_h + term
        pv.append(pv_h)
    l_scr[...] = alpha * l_scr[...] + jnp.sum(p, axis=1, keepdims=True)
    acc_scr[...] = alpha * acc_scr[...] + jnp.concatenate(pv, axis=1)
    m_scr[...] = m_new

    @pl.when(p_idx == n_steps - 1)
    def _():
        lam = _diff_lambda(lq_ref, lam_init)
        s_new = jnp.sum(qm * kn_ref[0], axis=1, keepdims=True)
        m_old2 = m_scr[...]
        m_fin = jnp.maximum(m_old2, s_new)
        a2 = jnp.exp(m_old2 - m_fin)
        p_new = jnp.exp(s_new - m_fin)
        l_fin = a2 * l_scr[...] + p_new
        acc = a2 * acc_scr[...] + p_new * vn_ref[0]
        r = acc / l_fin
        head_of_lane = lane // DH
        o1 = jnp.sum(jnp.where(head_of_lane == rid, r, 0.0), axis=0, keepdims=True)
        o2 = jnp.sum(jnp.where(head_of_lane == rid - N_HEAD, r, 0.0), axis=0, keepdims=True)
        o = o1 - lam * o2
        for h in range(N_HEAD):
            sl = slice(h * DH, (h + 1) * DH)
            o_ref[0, :, sl] = _sub_norm(o[:, sl], gs_ref[...], lam_init)


def _diff_attn_sample(page_table, q3, kn3, vn3, cache_k, cache_v, lam_qk, g_sub, j, lam_init):
    batch, n_pages = page_table.shape
    n_steps = n_pages // PAGES_PER_STEP
    kern = functools.partial(_diff_attn_step_kernel, n_steps=n_steps, lam_init=lam_init)
    row3 = lambda b, p, pt: (b, 0, 0)

    def k_spec(g):
        return pl.BlockSpec((1, TOK_W, PAGE), lambda b, p, pt: (pt[b, p * PAGES_PER_STEP + g], 0, 0))

    def v_spec(g):
        return pl.BlockSpec((1, N_HEAD, PAGE, DH),
                            lambda b, p, pt: (pt[b, p * PAGES_PER_STEP + g], 0, 0, 0))

    k_specs = [k_spec(g) for g in range(PAGES_PER_STEP)]
    v_specs = [v_spec(g) for g in range(PAGES_PER_STEP)]
    grid_spec = pltpu.PrefetchScalarGridSpec(
        num_scalar_prefetch=1,
        grid=(batch, n_steps),
        in_specs=[pl.BlockSpec((1, 1, TOK_W), row3),
                  pl.BlockSpec((1, 1, TOK_W), row3),
                  pl.BlockSpec((1, 1, TOK_W), row3)] + k_specs + v_specs + [
                  pl.BlockSpec((1, 4, DQK), lambda b, p, pt: (j, 0, 0)),
                  pl.BlockSpec((1, DH), lambda b, p, pt: (0, 0))],
        out_specs=pl.BlockSpec((1, 1, TOK_W), row3),
        scratch_shapes=[pltpu.VMEM((16, TOK_W), F32), pltpu.VMEM((16, 1), F32),
                        pltpu.VMEM((16, 1), F32), pltpu.VMEM((16, TOK_W), F32)],
    )
    return pl.pallas_call(
        kern,
        grid_spec=grid_spec,
        out_shape=jax.ShapeDtypeStruct((batch, 1, TOK_W), F32),
        compiler_params=_cparams(("arbitrary", "arbitrary")),
        name="diff_attn_step",
    )(page_table, q3, kn3, vn3, *([cache_k] * PAGES_PER_STEP), *([cache_v] * PAGES_PER_STEP),
      lam_qk, g_sub)


def _diff_lambda_init(li):
    return 0.8 - 0.6 * math.exp(-0.3 * li)


def _prep_weights(P):
    w = {}
    wa = P['w_in_a']
    gates = jnp.pad(wa[:, :, 4 * TOK_W:4 * TOK_W + 2 * N_HEAD], ((0, 0), (0, 0), (0, LANES - 2 * N_HEAD)))
    w['w_a32'] = jnp.concatenate([wa[:, :, :4 * TOK_W], wa[:, :, 4 * TOK_W + 2 * N_HEAD:], gates],
                                 axis=-1)
    w['w_a'] = w['w_a32'].astype(BF16)
    w['gate_bias'] = jnp.pad(jnp.concatenate([P['b_igate'], P['b_fgate']], axis=-1),
                             ((0, 0), (0, LANES - 2 * N_HEAD)))[:, None, :]
    w['w_b'] = P['w_in_b'].astype(BF16)
    w['w_kv'] = P['w_kv_shared'].astype(BF16)
    w['w_mem'] = jnp.transpose(P['w_mem_kv'], (1, 0, 2)).reshape(D_MODEL, DEPTH * 2 * MEM_W).astype(BF16)
    w['w_out_t32'] = P['w_out'][:, :TOK_W, :]
    w['w_out_m32'] = P['w_out'][:, TOK_W:, :]
    w['w_out_t'] = w['w_out_t32'].astype(BF16)
    w['w_out_m'] = w['w_out_m32'].astype(BF16)
    wr = jnp.concatenate([P['w_router'], P['w_group']], axis=-1)
    w['w_r32'] = jnp.pad(wr, ((0, 0), (0, 0), (0, LANES - wr.shape[-1])))
    w['w_r'] = w['w_r32'].astype(BF16)
    br = jnp.concatenate([P['b_router'], P['b_group']], axis=-1)
    w['b_r'] = jnp.pad(br, ((0, 0), (0, LANES - br.shape[-1])))[:, None, :]
    return w


def _layer_tail(x, tok, mem, li, P, W, tm_mix, tm_moe, hp=False):
    sfx = '32' if hp else ''
    x1, comb, cnt = _mix(x, tok, mem, W['w_out_t' + sfx][li], W['w_out_m' + sfx][li],
                         P['ln_mix_g'][li][None, :], P['ln_mix_b'][li][None, :],
                         W['w_r' + sfx][li], W['b_r'][li], tm_mix, hp)
    if hp:
        return _moe(x1, comb, P['w_e1'], P['w_e3'], P['w_e2'],
                    P['ln_ffn_g'][li][None, :], P['ln_ffn_b'][li][None, :], li, tm_moe, hp)
    return _moe_sparse(x1, comb, cnt, P, W, li)


def _prompt_trunk(x_prompt, mem_prompt, P, W):
    batch, seq, _ = x_prompt.shape
    m_tok = batch * seq
    x = x_prompt.reshape(m_tok, D_MODEL)
    mem = mem_prompt.reshape(batch * N_MEM, D_MODEL)
    kv_mem = _proj(mem, W['w_mem'], 512)
    kv_mem = jnp.transpose(kv_mem.reshape(batch, N_MEM, DEPTH, 2, MEM_W), (3, 2, 0, 1, 4))
    mem_k, mem_v = kv_mem[0], kv_mem[1]

    new_c, new_n, new_m = [], [], []
    k_t = v_h = k16 = vt16 = None
    for li in range(DEPTH):
        if li < N_A:
            proj = _proj(x, W['w_a'][li], 256)
            tok, c, n, m = _mlstm_prompt(proj, W['gate_bias'][li], batch, seq)
            new_c.append(c)
            new_n.append(n)
            new_m.append(m[:, 0, :N_HEAD])
            mq_block = (4 * TOK_W) // MEM_W
        else:
            if li == N_A:
                k_t, v_h, k16, vt16 = _proj_shared_kv(x, W['w_kv'], batch, seq, 512)
            j = li - N_A
            proj = _proj(x, W['w_b'][j], 512)
            tok = _diff_attn_prompt(proj, k16, vt16, P['lambda_qk'], P['subln_g'][j:j + 1], j,
                                    _diff_lambda_init(li), batch, seq, 512)
            mq_block = TOK_W // MEM_W
        mem_o = _mem_attn_prompt(proj, mq_block, mem_k[li], mem_v[li], seq, 512)
        x = _layer_tail(x, tok, mem_o, li, P, W, 512, 1024)
    y = x.reshape(batch, seq, D_MODEL)
    k_p = jnp.transpose(k_t.reshape(batch, 2, N_HEAD, DQK, seq), (0, 4, 1, 2, 3))
    v_p = jnp.transpose(v_h, (0, 2, 1, 3))
    shp = (DEPTH, batch, N_MEM, H_M, DH_M)
    return (y, mem_k.reshape(shp), mem_v.reshape(shp),
            jnp.stack(new_c), jnp.stack(new_n), jnp.stack(new_m), k_p, v_p)


def _sample_trunk(x_sample, cache_mem_k, cache_mem_v, state, cache_k, cache_v, page_table, P, W):
    batch = x_sample.shape[0]
    x = x_sample.reshape(batch, D_MODEL)
    c_all, n_all, m_all = state
    mem_k = jnp.transpose(cache_mem_k, (0, 1, 3, 4, 2)).reshape(DEPTH, batch, MEM_W, N_MEM)
    mem_v = jnp.transpose(cache_mem_v, (0, 1, 3, 4, 2)).reshape(DEPTH, batch, MEM_W, N_MEM)
    ck = jnp.transpose(cache_k, (0, 2, 3, 4, 1)).reshape(cache_k.shape[0], TOK_W, PAGE)
    cv = jnp.transpose(cache_v, (0, 2, 1, 3))

    new_c, new_n, new_m = [], [], []
    kv32 = None
    for li in range(DEPTH):
        if li < N_A:
            proj = _proj(x, W['w_a32'][li], batch, 384, hp=True)
            m_pad = jnp.pad(m_all[li], ((0, 0), (0, LANES - N_HEAD)))[:, None, :]
            tok3, c, n, m = _mlstm_sample(proj[:, None, :], W['gate_bias'][li], c_all, n_all, m_pad, li)
            new_c.append(c)
            new_n.append(n)
            new_m.append(m[:, 0, :N_HEAD])
            mq = proj[:, 4 * TOK_W:4 * TOK_W + MEM_W]
        else:
            if li == N_A:
                kv32 = _proj(x, P['w_kv_shared'], batch, 384, hp=True)
            j = li - N_A
            proj = _proj(x, P['w_in_b'][j], batch, 256, hp=True)
            tok3 = _diff_attn_sample(page_table, proj[:, None, :TOK_W], kv32[:, None, :TOK_W],
                                     kv32[:, None, TOK_W:], ck, cv, P['lambda_qk'],
                                     P['subln_g'][j:j + 1], j, _diff_lambda_init(li))
            mq = proj[:, TOK_W:]
        mem_o = _mem_attn_sample(mq[:, None, :], mem_k, mem_v, li)
        x = _layer_tail(x, tok3[:, 0, :], mem_o[:, 0, :], li, P, W, batch, batch, hp=True)
    y = x.reshape(batch, 1, D_MODEL)
    k_s = kv32[:, :TOK_W].reshape(batch, 1, 2, N_HEAD, DQK)
    v_s = kv32[:, TOK_W:].reshape(batch, 1, N_HEAD, DH)
    return y, jnp.stack(new_c), jnp.stack(new_n), jnp.stack(new_m), k_s, v_s


def kernel(x_prompt, x_sample, mem_prompt, cache_mem_k, cache_mem_v, state_mlstm_C, state_mlstm_n,
           state_mlstm_m, cache_k, cache_v, page_table, w_in_a, b_igate, b_fgate, w_in_b, lambda_qk,
           subln_g, w_kv_shared, w_mem_kv, w_out, ln_mix_g, ln_mix_b, ln_ffn_g, ln_ffn_b, w_group,
           b_group, w_router, b_router, w_e1, w_e3, w_e2):
    P = {'w_in_a': w_in_a, 'b_igate': b_igate, 'b_fgate': b_fgate, 'w_in_b': w_in_b,
         'lambda_qk': lambda_qk, 'subln_g': subln_g, 'w_kv_shared': w_kv_shared, 'w_mem_kv': w_mem_kv,
         'w_out': w_out, 'ln_mix_g': ln_mix_g, 'ln_mix_b': ln_mix_b, 'ln_ffn_g': ln_ffn_g,
         'ln_ffn_b': ln_ffn_b, 'w_group': w_group, 'b_group': b_group, 'w_router': w_router,
         'b_router': b_router, 'w_e1': w_e1, 'w_e3': w_e3, 'w_e2': w_e2}
    W = _prep_weights(P)
    y_p, mem_k_p, mem_v_p, c_p, n_p, m_p, k_p, v_p = _prompt_trunk(x_prompt, mem_prompt, P, W)
    y_s, c_s, n_s, m_s, k_s, v_s = _sample_trunk(
        x_sample, cache_mem_k, cache_mem_v, (state_mlstm_C, state_mlstm_n, state_mlstm_m),
        cache_k, cache_v, page_table, P, W)
    return (y_p, y_s, mem_k_p, mem_v_p, c_p, n_p, m_p, k_p, v_p, c_s, n_s, m_s, k_s, v_s)
```
